```python
import math
import jax
import jax.numpy as jnp
from jax import lax
import numpy as np

D_MODEL = 4096
BATCH = 8
SEQ = 2048
DEPTH = 4

N_HEADS = 32
HEAD_DIM = D_MODEL // N_HEADS
N_KV_GROUPS = 4
GROUP_SIZE = N_HEADS // N_KV_GROUPS
N_A_LAYERS = DEPTH // 2
CMP_BLOCK = 32
CMP_STRIDE = 16
CMP_HIDDEN = HEAD_DIM
SLC_BLOCK = 64
N_SELECT = 16
WINDOW = 512
Q_BLOCK = 128
N_BUCKETS = 32
MAX_DISTANCE = 128
RMS_EPS = 1e-6
FORCE_SCORE = 1e6
NSA_IN = 4 * N_HEADS * HEAD_DIM + 6 * N_KV_GROUPS * HEAD_DIM + 3 * N_HEADS
SB_IN = 2 * N_HEADS * HEAD_DIM

kernel_name = 'hybrid_nsa_stickbreaking_yoco'


def rms_norm(x, g):
    xf = x.astype(jnp.float32)
    y = xf * lax.rsqrt(jnp.mean(xf * xf, axis=-1, keepdims=True) + RMS_EPS)
    return (y * g.astype(jnp.float32)).astype(x.dtype)


def t5_bucket(dist):
    max_exact = N_BUCKETS // 2
    d = jnp.maximum(dist, 0)
    log_ratio = jnp.log(jnp.maximum(d, max_exact).astype(jnp.float32) / max_exact)
    large = max_exact + (log_ratio / math.log(MAX_DISTANCE / max_exact)
                         * (N_BUCKETS - max_exact)).astype(jnp.int32)
    return jnp.where(d < max_exact, d, jnp.minimum(large, N_BUCKETS - 1))


def static_bias(rel_bias, dist):
    q_len, k_len = dist.shape
    b = rel_bias[t5_bucket(dist)].astype(jnp.float32)
    return b.transpose(2, 0, 1).reshape(N_KV_GROUPS, GROUP_SIZE, q_len, k_len)


def masked_softmax(s, mask):
    s = jnp.where(mask, s, -jnp.inf)
    m = jnp.max(s, axis=-1, keepdims=True)
    m = jnp.where(jnp.isfinite(m), m, 0.0)
    e = jnp.where(mask, jnp.exp(s - m), 0.0)
    return e / jnp.maximum(jnp.sum(e, axis=-1, keepdims=True), 1e-30)


def compress(k, pos, w1, w2):
    b, g, t, dh = k.shape
    n_cmp = (t - CMP_BLOCK) // CMP_STRIDE + 1
    idx = np.arange(n_cmp)[:, None] * CMP_STRIDE + np.arange(CMP_BLOCK)[None, :]
    blocks = (k[:, :, idx, :] + pos).reshape(b, g, n_cmp, CMP_BLOCK * dh)
    return jax.nn.silu(blocks @ w1) @ w2


def nsa_layer(h, w_in, cmp_pos, cmp_k_w1, cmp_k_w2, cmp_v_w1, cmp_v_w2, w_out, rel_bias):
    B, T, _ = h.shape
    G, R, Dh = N_KV_GROUPS, GROUP_SIZE, HEAD_DIM
    HD, GD = N_HEADS * HEAD_DIM, N_KV_GROUPS * HEAD_DIM
    scale = 1.0 / math.sqrt(Dh)
    cuts = [int(c) for c in np.cumsum([HD, GD, GD, GD, GD, GD, GD, 3 * N_HEADS, HD, HD])]
    (q, kc_raw, vc_raw, ks_raw, vs_raw, kw_raw, vw_raw, gate_logits,
     z_c, z_s, z_w) = jnp.split(h @ w_in, cuts, axis=-1)

    def to_groups(t):
        return t.reshape(B, T, G, Dh).transpose(0, 2, 1, 3)

    q = q.reshape(B, T, G, R, Dh).transpose(0, 2, 3, 1, 4)

    n_cmp = (T - CMP_BLOCK) // CMP_STRIDE + 1
    cmp_start = (np.arange(n_cmp) * CMP_STRIDE).astype(np.int32)
    cmp_end = (cmp_start + CMP_BLOCK - 1).astype(np.int32)
    kc = compress(to_groups(kc_raw), cmp_pos, cmp_k_w1, cmp_k_w2)
    vc = compress(to_groups(vc_raw), cmp_pos, cmp_v_w1, cmp_v_w2)

    n_slc = T // SLC_BLOCK
    n_sel = min(N_SELECT, n_slc)
    slc_start = (np.arange(n_slc) * SLC_BLOCK).astype(np.int32)
    overlap = jnp.asarray(((cmp_start[:, None] < slc_start[None, :] + SLC_BLOCK)
                           & (cmp_end[:, None] >= slc_start[None, :])).astype(np.float32))
    ks_blk = to_groups(ks_raw).reshape(B, G, n_slc, SLC_BLOCK, Dh)
    vs_blk = to_groups(vs_raw).reshape(B, G, n_slc, SLC_BLOCK, Dh)
    b_idx = jnp.arange(B)[:, None, None, None]
    g_idx = jnp.arange(G)[None, :, None, None]
    bias_by_group = rel_bias.reshape(N_BUCKETS, G, R).transpose(1, 0, 2)

    pad = ((0, 0), (0, 0), (WINDOW, 0), (0, 0))
    kw = jnp.pad(to_groups(kw_raw), pad)
    vw = jnp.pad(to_groups(vw_raw), pad)

    def query_block(i):
        q0 = i * Q_BLOCK
        qi = lax.dynamic_slice_in_dim(q, q0, Q_BLOCK, axis=3)
        tpos = q0 + jnp.arange(Q_BLOCK)

        dist_c = tpos[:, None] - cmp_end[None, :]
        s_c = (jnp.einsum('bgrqd,bgnd->bgrqn', qi, kc, preferred_element_type=jnp.float32) * scale
               + static_bias(rel_bias, dist_c))
        p_c = masked_softmax(s_c, dist_c >= 0)
        o_c = jnp.einsum('bgrqn,bgnd->bgrqd', p_c.astype(vc.dtype), vc)

        imp = jnp.einsum('bgrqn,ns->bgqs', p_c, overlap)
        j = jnp.arange(n_slc)[None, :]
        cur = (tpos // SLC_BLOCK)[:, None]
        forced = (j == 0) | (j == cur) | (j == cur - 1)
        valid = slc_start[None, :] <= tpos[:, None]
        imp = jnp.where(forced, FORCE_SCORE, jnp.where(valid, imp, -1.0))
        _, sel = lax.top_k(imp, n_sel)
        n_key = n_sel * SLC_BLOCK
        ks = ks_blk[b_idx, g_idx, sel].reshape(B, G, Q_BLOCK, n_key, Dh)
        vs = vs_blk[b_idx, g_idx, sel].reshape(B, G, Q_BLOCK, n_key, Dh)
        kpos_s = (sel[..., None] * SLC_BLOCK + jnp.arange(SLC_BLOCK)).reshape(B, G, Q_BLOCK, n_key)
        dist_s = tpos[None, None, :, None] - kpos_s
        bias_s = jnp.moveaxis(bias_by_group[g_idx, t5_bucket(dist_s)], -1, 2).astype(jnp.float32)
        s_s = (jnp.einsum('bgrqd,bgqkd->bgrqk', qi, ks, preferred_element_type=jnp.float32) * scale
               + bias_s)
        p_s = masked_softmax(s_s, (dist_s >= 0)[:, :, None])
        o_s = jnp.einsum('bgrqk,bgqkd->bgrqd', p_s.astype(vs.dtype), vs)

        kwi = lax.dynamic_slice_in_dim(kw, q0, Q_BLOCK + WINDOW, axis=2)
        vwi = lax.dynamic_slice_in_dim(vw, q0, Q_BLOCK + WINDOW, axis=2)
        kpos_w = q0 - WINDOW + jnp.arange(Q_BLOCK + WINDOW)
        dist_w = tpos[:, None] - kpos_w[None, :]
        mask_w = (dist_w >= 0) & (dist_w < WINDOW) & (kpos_w[None, :] >= 0)
        s_w = (jnp.einsum('bgrqd,bgkd->bgrqk', qi, kwi, preferred_element_type=jnp.float32) * scale
               + static_bias(rel_bias, dist_w))
        p_w = masked_softmax(s_w, mask_w)
        o_w = jnp.einsum('bgrqk,bgkd->bgrqd', p_w.astype(vwi.dtype), vwi)
        return o_c, o_s, o_w

    o_c, o_s, o_w = lax.map(query_block, jnp.arange(T // Q_BLOCK))

    def gated(o, z):
        o = o.transpose(1, 0, 4, 2, 3, 5).reshape(B, T, N_HEADS, Dh)
        return o * jax.nn.silu(z.reshape(B, T, N_HEADS, Dh))

    gates = jax.nn.sigmoid(gate_logits).reshape(B, T, 3, N_HEADS, 1)
    mixed = (gates[:, :, 0] * gated(o_c, z_c) + gates[:, :, 1] * gated(o_s, z_s)
             + gates[:, :, 2] * gated(o_w, z_w))
    return mixed.reshape(B, T, HD) @ w_out


def stick_breaking_layer(h, w_in, w_out, k_sh, v_sh):
    B, T, _ = h.shape
    H, Dh = N_HEADS, HEAD_DIM
    scale = 1.0 / math.sqrt(Dh)
    q, z = jnp.split(h @ w_in, 2, axis=-1)
    q = q.reshape(B, T, H, Dh).transpose(0, 2, 1, 3)
    kpos = jnp.arange(T)

    def query_block(i):
        q0 = i * Q_BLOCK
        qi = lax.dynamic_slice_in_dim(q, q0, Q_BLOCK, axis=2)
        tpos = q0 + jnp.arange(Q_BLOCK)
        logits = jnp.einsum('bhqd,bhkd->bhqk', qi, k_sh, preferred_element_type=jnp.float32) * scale
        mask = kpos[None, :] < tpos[:, None]
        neg_log_keep = jnp.where(mask, jax.nn.softplus(logits), 0.0)
        between = lax.cumsum(neg_log_keep, axis=3, reverse=True) - neg_log_keep
        a = jnp.where(mask, jnp.exp(jax.nn.log_sigmoid(logits) - between), 0.0)
        return jnp.einsum('bhqk,bhkd->bhqd', a.astype(v_sh.dtype), v_sh)

    o = lax.map(query_block, jnp.arange(T // Q_BLOCK))
    o = o.transpose(1, 0, 3, 2, 4).reshape(B, T, H * Dh)
    return (o * jax.nn.silu(z)) @ w_out


def setup_inputs(seed: int = 0) -> dict:
    key = jax.random.key(seed)
    keys = iter(jax.random.split(key, 48))
    HD = N_HEADS * HEAD_DIM

    def normal(shape, scale):
        return jax.random.normal(next(keys), shape, jnp.float32) * scale

    def gain():
        return 1.0 + normal((D_MODEL,), 0.01)

    inputs = {'x': normal((BATCH, SEQ, D_MODEL), 1.0),
              'rel_bias': normal((N_BUCKETS, N_HEADS), 0.5)}
    for l in range(N_A_LAYERS):
        p = f'a{l}_'
        inputs[p + 'norm'] = gain()
        inputs[p + 'w_in'] = normal((D_MODEL, NSA_IN), D_MODEL ** -0.5)
        inputs[p + 'cmp_pos'] = normal((CMP_BLOCK, HEAD_DIM), 0.1)
        inputs[p + 'cmp_k_w1'] = normal((CMP_BLOCK * HEAD_DIM, CMP_HIDDEN), (CMP_BLOCK * HEAD_DIM) ** -0.5)
        inputs[p + 'cmp_k_w2'] = normal((CMP_HIDDEN, HEAD_DIM), CMP_HIDDEN ** -0.5)
        inputs[p + 'cmp_v_w1'] = normal((CMP_BLOCK * HEAD_DIM, CMP_HIDDEN), (CMP_BLOCK * HEAD_DIM) ** -0.5)
        inputs[p + 'cmp_v_w2'] = normal((CMP_HIDDEN, HEAD_DIM), CMP_HIDDEN ** -0.5)
        inputs[p + 'w_out'] = normal((HD, D_MODEL), HD ** -0.5)
    inputs['kv_norm'] = gain()
    inputs['w_kv'] = normal((D_MODEL, 2 * HD), D_MODEL ** -0.5)
    for l in range(N_A_LAYERS, DEPTH):
        p = f'b{l}_'
        inputs[p + 'norm'] = gain()
        inputs[p + 'w_in'] = normal((D_MODEL, SB_IN), D_MODEL ** -0.5)
        inputs[p + 'w_out'] = normal((HD, D_MODEL), HD ** -0.5)
    inputs['final_norm'] = gain()
    return inputs


def reference(x, rel_bias,
              a0_norm, a0_w_in, a0_cmp_pos, a0_cmp_k_w1, a0_cmp_k_w2, a0_cmp_v_w1, a0_cmp_v_w2, a0_w_out,
              a1_norm, a1_w_in, a1_cmp_pos, a1_cmp_k_w1, a1_cmp_k_w2, a1_cmp_v_w1, a1_cmp_v_w2, a1_w_out,
              kv_norm, w_kv,
              b2_norm, b2_w_in, b2_w_out,
              b3_norm, b3_w_in, b3_w_out,
              final_norm):
    nsa_params = [
        (a0_norm, a0_w_in, a0_cmp_pos, a0_cmp_k_w1, a0_cmp_k_w2, a0_cmp_v_w1, a0_cmp_v_w2, a0_w_out),
        (a1_norm, a1_w_in, a1_cmp_pos, a1_cmp_k_w1, a1_cmp_k_w2, a1_cmp_v_w1, a1_cmp_v_w2, a1_w_out),
    ]
    sb_params = [(b2_norm, b2_w_in, b2_w_out), (b3_norm, b3_w_in, b3_w_out)]
    B, T, _ = x.shape
    k_sh = v_sh = None
    for layer in range(DEPTH):
        if layer < N_A_LAYERS:
            norm, w_in, pos, kw1, kw2, vw1, vw2, w_out = nsa_params[layer]
            x = x + nsa_layer(rms_norm(x, norm), w_in, pos, kw1, kw2, vw1, vw2, w_out, rel_bias)
        else:
            if layer == N_A_LAYERS:
                k_flat, v_flat = jnp.split(rms_norm(x, kv_norm) @ w_kv, 2, axis=-1)
                k_sh = k_flat.reshape(B, T, N_HEADS, HEAD_DIM).transpose(0, 2, 1, 3)
                v_sh = v_flat.reshape(B, T, N_HEADS, HEAD_DIM).transpose(0, 2, 1, 3)
            norm, w_in, w_out = sb_params[layer - N_A_LAYERS]
            x = x + stick_breaking_layer(rms_norm(x, norm), w_in, w_out, k_sh, v_sh)
    return rms_norm(x, final_norm)
```

```python
import functools
import math

import numpy as np
import jax
import jax.numpy as jnp
from jax import lax
from jax.experimental import pallas as pl
from jax.experimental.pallas import tpu as pltpu

N_HEADS = 32
HEAD_DIM = 128
N_KV_GROUPS = 4
CMP_BLOCK = 32
CMP_STRIDE = 16
SLC_BLOCK = 64
N_SELECT = 16
WINDOW = 512
N_BUCKETS = 32
MAX_DISTANCE = 128
RMS_EPS = 1e-6
FORCE_SCORE = 1e6

LANES = 128
NEG_BIG = -1e30
VMEM_LIMIT = 56 * 1024 * 1024

F32 = jnp.float32
BF16 = jnp.bfloat16


def _cparams(sem):
    return pltpu.CompilerParams(dimension_semantics=sem, vmem_limit_bytes=VMEM_LIMIT)


def _dot(a, b):
    return jnp.dot(a, b, preferred_element_type=F32)


def _dot_nt(a, b):
    return lax.dot_general(a, b, (((1,), (1,)), ((), ())), preferred_element_type=F32)


def _rmsnorm_kernel(x_ref, g_ref, o_ref):
    x = x_ref[...]
    ms = jnp.mean(x * x, axis=-1, keepdims=True)
    o_ref[...] = (x * lax.rsqrt(ms + RMS_EPS) * g_ref[...]).astype(o_ref.dtype)


def _rmsnorm(x, g, out_dtype, tm=512):
    m, d = x.shape
    tm = min(tm, m)
    return pl.pallas_call(
        _rmsnorm_kernel,
        grid=(m // tm,),
        in_specs=[pl.BlockSpec((tm, d), lambda i: (i, 0)),
                  pl.BlockSpec((1, d), lambda i: (0, 0))],
        out_specs=pl.BlockSpec((tm, d), lambda i: (i, 0)),
        out_shape=jax.ShapeDtypeStruct((m, d), out_dtype),
        compiler_params=_cparams(("parallel",)),
        name="rmsnorm",
    )(x, g.reshape(1, d))


def _matmul_kernel(x_ref, w_ref, o_ref):
    o_ref[...] = _dot(x_ref[...], w_ref[...]).astype(o_ref.dtype)


def _matmul_res_kernel(x_ref, w_ref, r_ref, o_ref):
    o_ref[...] = (r_ref[...] + _dot(x_ref[...], w_ref[...])).astype(o_ref.dtype)


def _matmul(x, w, out_dtype, res=None, tm=512, tn=1024):
    m, k = x.shape
    n = w.shape[1]
    tm, tn = min(tm, m), min(tn, n)
    assert m % tm == 0 and n % tn == 0
    in_specs = [pl.BlockSpec((tm, k), lambda j, i: (i, 0)),
                pl.BlockSpec((k, tn), lambda j, i: (0, j))]
    args = [x, w]
    body = _matmul_kernel
    if res is not None:
        in_specs.append(pl.BlockSpec((tm, tn), lambda j, i: (i, j)))
        args.append(res)
        body = _matmul_res_kernel
    return pl.pallas_call(
        body,
        grid=(n // tn, m // tm),
        in_specs=in_specs,
        out_specs=pl.BlockSpec((tm, tn), lambda j, i: (i, j)),
        out_shape=jax.ShapeDtypeStruct((m, n), out_dtype),
        compiler_params=_cparams(("parallel", "parallel")),
        name="matmul_res" if res is not None else "matmul",
    )(*args)


def _t5_bucket_np(dist):
    max_exact = N_BUCKETS // 2
    d = np.maximum(dist, 0)
    log_ratio = np.log(np.maximum(d, max_exact).astype(np.float32) / np.float32(max_exact))
    large = max_exact + (log_ratio / np.float32(math.log(MAX_DISTANCE / max_exact))
                         * np.float32(N_BUCKETS - max_exact)).astype(np.int32)
    return np.where(d < max_exact, d, np.minimum(large, N_BUCKETS - 1)).astype(np.int32)


def _bias_kernel(rb_ref, idx_ref, o_ref):
    h = pl.program_id(0)
    idx = idx_ref[...]
    acc = jnp.zeros(idx.shape, F32)
    for b in range(N_BUCKETS):
        acc = jnp.where(idx == b, rb_ref[h, b], acc)
    o_ref[0] = acc


def _bias_table(rel_bias_t, bucket_idx):
    h = rel_bias_t.shape[0]
    rows = bucket_idx.shape[0]
    return pl.pallas_call(
        _bias_kernel,
        grid=(h,),
        in_specs=[pl.BlockSpec(memory_space=pltpu.SMEM),
                  pl.BlockSpec((rows, LANES), lambda i: (0, 0))],
        out_specs=pl.BlockSpec((1, rows, LANES), lambda i: (i, 0, 0)),
        out_shape=jax.ShapeDtypeStruct((h, rows, LANES), F32),
        compiler_params=_cparams(("arbitrary",)),
        name="bias_table",
    )(rel_bias_t, jnp.asarray(bucket_idx))


def _compress_kernel(k2_ref, v2_ref, pos_ref, kw1_ref, kw2_ref, vw1_ref, vw2_ref, kc_ref, vc_ref):
    half = pos_ref.shape[1] // 2
    nch = k2_ref.shape[2]
    row = lax.broadcasted_iota(jnp.int32, (nch, HEAD_DIM), 0)

    def one(x_ref, w1_ref, w2_ref, o_ref):
        x = x_ref[0, 0]
        lo = (x + pos_ref[:, :half]).astype(BF16)
        hi = pltpu.roll(x + pos_ref[:, half:], nch - 1, 0).astype(BF16)
        h = _dot(lo, w1_ref[:half, :]) + _dot(hi, w1_ref[half:, :])
        h = h * jax.nn.sigmoid(h)
        o = _dot(h.astype(BF16), w2_ref[...])
        o_ref[0, 0] = jnp.where(row < nch - 1, o, 0.0).astype(o_ref.dtype)

    one(k2_ref, kw1_ref, kw2_ref, kc_ref)
    one(v2_ref, vw1_ref, vw2_ref, vc_ref)


def _compress(k2, v2, pos, kw1, kw2, vw1, vw2):
    b, g, nch, wid = k2.shape
    blk = pl.BlockSpec((1, 1, nch, wid), lambda i, j: (i, j, 0, 0))
    oblk = pl.BlockSpec((1, 1, nch, HEAD_DIM), lambda i, j: (i, j, 0, 0))

    def const(a):
        return pl.BlockSpec(a.shape, lambda i, j: (0,) * a.ndim)

    out = jax.ShapeDtypeStruct((b, g, nch, HEAD_DIM), BF16)
    return pl.pallas_call(
        _compress_kernel,
        grid=(b, g),
        in_specs=[blk, blk, const(pos), const(kw1), const(kw2), const(vw1), const(vw2)],
        out_specs=[oblk, oblk],
        out_shape=[out, out],
        compiler_params=_cparams(("parallel", "parallel")),
        name="compress",
    )(k2, v2, pos, kw1, kw2, vw1, vw2)


def _split3(x):
    a = x.astype(BF16)
    r = x - a.astype(F32)
    b = r.astype(BF16)
    c = (r - b.astype(F32)).astype(BF16)
    return a, b, c


def _nsa_kernel(q_ref, kc_ref, vc_ref, ks_ref, vs_ref, kw_ref, vw_ref, bc_ref, bt_ref, gl_ref,
                zc_ref, zs_ref, zw_ref, ovl_ref, exp_ref, o_ref, mask_scr, *, group, n_slc, n_sel):
    R = group
    QB = LANES
    RQ = R * QB
    i = pl.program_id(1)
    q0 = i * QB
    scale = 1.0 / math.sqrt(HEAD_DIM)

    q = q_ref[0]
    q_all = jnp.concatenate([q[:, r * HEAD_DIM:(r + 1) * HEAD_DIM] for r in range(R)], axis=0)

    qq = lax.broadcasted_iota(jnp.int32, (R, QB, LANES), 1).reshape(RQ, LANES)
    lane = lax.broadcasted_iota(jnp.int32, (RQ, LANES), 1)
    tpos = q0 + qq

    dist_c = tpos - (lane * CMP_STRIDE + (CMP_BLOCK - 1))
    mask_c = dist_c >= 0
    s = _dot_nt(q_all, kc_ref[0, 0]) * scale + bc_ref[...].reshape(RQ, LANES)
    s = jnp.where(mask_c, s, NEG_BIG)
    m = jnp.max(s, axis=1, keepdims=True)
    e = jnp.where(mask_c, jnp.exp(s - m), 0.0)
    p_c = e / jnp.maximum(jnp.sum(e, axis=1, keepdims=True), 1e-30)
    o_c = _dot(p_c.astype(BF16), vc_ref[0, 0])

    psum = p_c[0:QB]
    for r in range(1, R):
        psum = psum + p_c[r * QB:(r + 1) * QB]
    ovl = ovl_ref[...]
    imp = sum(_dot(part, ovl) for part in _split3(psum))
    j = lax.broadcasted_iota(jnp.int32, (QB, LANES), 1)
    t = q0 + lax.broadcasted_iota(jnp.int32, (QB, LANES), 0)
    cur = t // SLC_BLOCK
    forced = (j == 0) | (j == cur) | (j == cur - 1)
    valid = j * SLC_BLOCK <= t
    score = jnp.where(forced, FORCE_SCORE, jnp.where(valid, imp, -1.0))
    score = jnp.where(j < n_slc, score, -2.0)
    rank = jnp.zeros((QB, LANES), F32)
    for jp in range(n_slc):
        col = score[:, jp:jp + 1]
        beats = (col > score) | ((col == score) & (j > jp))
        rank = rank + jnp.where(beats, 1.0, 0.0)
    sel = jnp.where((rank < n_sel) & (j < n_slc), 1.0, 0.0).astype(BF16)
    mask_scr[...] = _dot(sel, exp_ref[...])

    def bias_tile(delta):
        dd = jnp.minimum(delta, 2)
        return bt_ref[:, pl.ds(pl.multiple_of(dd * QB, QB), QB), :].reshape(RQ, LANES)

    def flash_step(kt, carry, k_ref, v_ref, valid_fn):
        m, l, acc = carry
        k0 = pl.multiple_of(kt * QB, QB)
        k = k_ref[0, pl.ds(k0, QB), :]
        v = v_ref[0, pl.ds(k0, QB), :]
        s = _dot_nt(q_all, k) * scale + bias_tile(i - kt)
        ok = valid_fn(k0, tpos - (k0 + lane))
        s = jnp.where(ok, s, NEG_BIG)
        m_new = jnp.maximum(m, jnp.max(s, axis=1, keepdims=True))
        alpha = jnp.exp(m - m_new)
        p = jnp.where(ok, jnp.exp(s - m_new), 0.0)
        l = alpha * l + jnp.sum(p, axis=1, keepdims=True)
        acc = alpha * acc + _dot(p.astype(BF16), v)
        return m_new, l, acc

    init = (jnp.full((RQ, 1), NEG_BIG, F32), jnp.zeros((RQ, 1), F32), jnp.zeros((RQ, HEAD_DIM), F32))

    def sel_valid(k0, dist):
        blk = mask_scr[:, pl.ds(k0, QB)]
        blk = jnp.concatenate([blk] * R, axis=0)
        return (blk > 0.5) & (dist >= 0)

    _, l_s, acc_s = lax.fori_loop(
        0, i + 1, lambda kt, c: flash_step(kt, c, ks_ref, vs_ref, sel_valid), init)
    o_s = acc_s / jnp.maximum(l_s, 1e-30)

    def win_valid(k0, dist):
        return (dist >= 0) & (dist < WINDOW)

    _, l_w, acc_w = lax.fori_loop(
        jnp.maximum(i - WINDOW // QB, 0), i + 1,
        lambda kt, c: flash_step(kt, c, kw_ref, vw_ref, win_valid), init)
    o_w = acc_w / jnp.maximum(l_w, 1e-30)

    gates = jax.nn.sigmoid(gl_ref[0])
    for r in range(R):
        rows = slice(r * QB, (r + 1) * QB)
        cols = slice(r * HEAD_DIM, (r + 1) * HEAD_DIM)
        mixed = jnp.zeros((QB, HEAD_DIM), F32)
        for c, (o, z_ref) in enumerate(((o_c, zc_ref), (o_s, zs_ref), (o_w, zw_ref))):
            z = z_ref[0, :, cols]
            mixed = mixed + gates[:, c * R + r:c * R + r + 1] * (o[rows] * (z * jax.nn.sigmoid(z)))
        o_ref[0, :, cols] = mixed.astype(o_ref.dtype)


def _nsa_attention(qkv, kc, vc, bias_c, bias_t, gl, z, ovl, expand, *, batch, seq):
    G, R = N_KV_GROUPS, N_HEADS // N_KV_GROUPS
    QB = LANES
    nqb = seq // QB
    hd_blocks = N_HEADS
    n_slc = seq // SLC_BLOCK
    n_sel = min(N_SELECT, n_slc)

    def kv_spec(which):
        return pl.BlockSpec((1, seq, HEAD_DIM), lambda g, i, b: (b, 0, hd_blocks + which * G + g))

    def z_spec(c):
        return pl.BlockSpec((1, QB, R * HEAD_DIM), lambda g, i, b: (b, i, c * G + g))

    cmp_spec = pl.BlockSpec((1, 1, LANES, HEAD_DIM), lambda g, i, b: (b, g, 0, 0))
    in_specs = [
        pl.BlockSpec((1, QB, R * HEAD_DIM), lambda g, i, b: (b, i, g)),
        cmp_spec, cmp_spec,
        kv_spec(2), kv_spec(3), kv_spec(4), kv_spec(5),
        pl.BlockSpec((R, QB, LANES), lambda g, i, b: (g, i, 0)),
        pl.BlockSpec((R, 3 * QB, LANES), lambda g, i, b: (g, 0, 0)),
        pl.BlockSpec((1, QB, LANES), lambda g, i, b: (b, i, g)),
        z_spec(0), z_spec(1), z_spec(2),
        pl.BlockSpec(ovl.shape, lambda g, i, b: (0, 0)),
        pl.BlockSpec(expand.shape, lambda g, i, b: (0, 0)),
    ]
    return pl.pallas_call(
        functools.partial(_nsa_kernel, group=R, n_slc=n_slc, n_sel=n_sel),
        grid=(G, nqb, batch),
        in_specs=in_specs,
        out_specs=pl.BlockSpec((1, QB, R * HEAD_DIM), lambda g, i, b: (b, i, g)),
        out_shape=jax.ShapeDtypeStruct((batch, seq, N_HEADS * HEAD_DIM), BF16),
        scratch_shapes=[pltpu.VMEM((QB, seq), F32)],
        compiler_params=_cparams(("parallel", "parallel", "parallel")),
        name="nsa_attention",
    )(qkv, kc, vc, qkv, qkv, qkv, qkv, bias_c, bias_t, gl, z, z, z, ovl, expand)


def _sb_kernel(q_ref, k_ref, v_ref, z_ref, tri_ref, o_ref, *, tq, tk):
    i = pl.program_id(2)
    scale = 1.0 / math.sqrt(HEAD_DIM)
    q = q_ref[0]
    tpos = i * tq + lax.broadcasted_iota(jnp.int32, (tq, tk), 0)
    lane = lax.broadcasted_iota(jnp.int32, (tq, tk), 1)
    tri = tri_ref[...]
    n_tiles = (i + 1) * (tq // tk)

    def body(step, carry):
        later, acc = carry
        k0 = pl.multiple_of((n_tiles - 1 - step) * tk, tk)
        k = k_ref[0, pl.ds(k0, tk), :]
        v = v_ref[0, pl.ds(k0, tk), :]
        logit = _dot_nt(q, k) * scale
        ok = (k0 + lane) < tpos
        sp = jnp.maximum(logit, 0.0) + jnp.log1p(jnp.exp(-jnp.abs(logit)))
        sp = jnp.where(ok, sp, 0.0)
        hi = sp.astype(BF16)
        lo = (sp - hi.astype(F32)).astype(BF16)
        between = _dot(hi, tri) + _dot(lo, tri) + later
        a = jnp.where(ok, jnp.exp(logit - sp - between), 0.0)
        acc = acc + _dot(a.astype(BF16), v)
        later = later + jnp.sum(sp, axis=1, keepdims=True)
        return later, acc

    _, acc = lax.fori_loop(0, n_tiles, body,
                           (jnp.zeros((tq, 1), F32), jnp.zeros((tq, HEAD_DIM), F32)))
    z = z_ref[0]
    o_ref[0] = (acc * (z * jax.nn.sigmoid(z))).astype(o_ref.dtype)


def _sb_attention(q, kv, z, tri, *, batch, seq, tq=256, tk=128):
    H = N_HEADS
    tq = min(tq, seq)
    return pl.pallas_call(
        functools.partial(_sb_kernel, tq=tq, tk=tk),
        grid=(batch, H, seq // tq),
        in_specs=[
            pl.BlockSpec((1, tq, HEAD_DIM), lambda b, h, i: (b, i, h)),
            pl.BlockSpec((1, seq, HEAD_DIM), lambda b, h, i: (b, 0, h)),
            pl.BlockSpec((1, seq, HEAD_DIM), lambda b, h, i: (b, 0, H + h)),
            pl.BlockSpec((1, tq, HEAD_DIM), lambda b, h, i: (b, i, h)),
            pl.BlockSpec(tri.shape, lambda b, h, i: (0, 0)),
        ],
        out_specs=pl.BlockSpec((1, tq, HEAD_DIM), lambda b, h, i: (b, i, h)),
        out_shape=jax.ShapeDtypeStruct((batch, seq, H * HEAD_DIM), BF16),
        compiler_params=_cparams(("parallel", "parallel", "parallel")),
        name="sb_attention",
    )(q, kv, kv, z, tri)


def _static_tables(seq):
    QB = LANES
    nch = seq // CMP_STRIDE
    n_slc = seq // SLC_BLOCK
    assert nch == LANES and n_slc <= LANES
    tq = np.arange(seq)[:, None]
    n = np.arange(LANES)[None, :]
    idx_c = _t5_bucket_np(tq - (n * CMP_STRIDE + CMP_BLOCK - 1))
    rows = np.arange(3 * QB)[:, None]
    idx_t = _t5_bucket_np((rows // QB) * QB + (rows % QB) - n)
    cmp_start = np.arange(LANES) * CMP_STRIDE
    slc_start = np.arange(LANES) * SLC_BLOCK
    ovl = ((cmp_start[:, None] < slc_start[None, :] + SLC_BLOCK)
           & (cmp_start[:, None] + CMP_BLOCK - 1 >= slc_start[None, :]))
    ovl = ovl & (np.arange(LANES)[:, None] < nch - 1) & (np.arange(LANES)[None, :] < n_slc)
    expand = (np.arange(seq)[None, :] // SLC_BLOCK) == np.arange(LANES)[:, None]
    return idx_c, idx_t, ovl.astype(np.float32), expand.astype(np.float32)


def _nsa_layer(xf, tabs, norm, w_in, cmp_pos, kw1, kw2, vw1, vw2, w_out, *, batch, seq):
    bias_c, bias_t, ovl, expand = tabs
    d = xf.shape[1]
    H, G, Dh = N_HEADS, N_KV_GROUPS, HEAD_DIM
    R = H // G
    HD, GD = H * Dh, G * Dh
    n_qkv = HD + 6 * GD
    hn = _rmsnorm(xf, norm, BF16)
    w_qkv = w_in[:, :n_qkv].astype(BF16)
    w_gate = w_in[:, n_qkv:n_qkv + 3 * H].reshape(d, 3, G, R).transpose(0, 2, 1, 3).reshape(d, G, 3 * R)
    w_gate = jnp.pad(w_gate, ((0, 0), (0, 0), (0, LANES - 3 * R))).reshape(d, G * LANES).astype(BF16)
    w_z = w_in[:, n_qkv + 3 * H:].astype(BF16)
    qkv = _matmul(hn, w_qkv, BF16)
    gl = _matmul(hn, w_gate, F32)
    z = _matmul(hn, w_z, F32)

    nch = seq // CMP_STRIDE

    def chunks(raw):
        t = raw.astype(F32).reshape(batch, nch, CMP_STRIDE, G, Dh).transpose(0, 3, 1, 2, 4)
        return t.reshape(batch, G, nch, CMP_STRIDE * Dh)

    kc, vc = _compress(chunks(qkv[:, HD:HD + GD]), chunks(qkv[:, HD + GD:HD + 2 * GD]),
                       cmp_pos.reshape(1, CMP_BLOCK * Dh),
                       kw1.astype(BF16), kw2.astype(BF16), vw1.astype(BF16), vw2.astype(BF16))
    mixed = _nsa_attention(qkv.reshape(batch, seq, n_qkv), kc, vc, bias_c, bias_t,
                           gl.reshape(batch, seq, G * LANES), z.reshape(batch, seq, 3 * HD),
                           ovl, expand, batch=batch, seq=seq)
    return _matmul(mixed.reshape(batch * seq, HD), w_out.astype(BF16), F32, res=xf)


def _sb_layer(xf, kv, tri, norm, w_in, w_out, *, batch, seq):
    HD = N_HEADS * HEAD_DIM
    hn = _rmsnorm(xf, norm, BF16)
    q = _matmul(hn, w_in[:, :HD].astype(BF16), BF16)
    z = _matmul(hn, w_in[:, HD:].astype(BF16), F32)
    o = _sb_attention(q.reshape(batch, seq, HD), kv, z.reshape(batch, seq, HD), tri, batch=batch, seq=seq)
    return _matmul(o.reshape(batch * seq, HD), w_out.astype(BF16), F32, res=xf)


def kernel(x, rel_bias, a0_norm, a0_w_in, a0_cmp_pos, a0_cmp_k_w1, a0_cmp_k_w2, a0_cmp_v_w1, a0_cmp_v_w2, a0_w_out, a1_norm, a1_w_in, a1_cmp_pos, a1_cmp_k_w1, a1_cmp_k_w2, a1_cmp_v_w1, a1_cmp_v_w2, a1_w_out, kv_norm, w_kv, b2_norm, b2_w_in, b2_w_out, b3_norm, b3_w_in, b3_w_out, final_norm):
    batch, seq, d = x.shape
    HD = N_HEADS * HEAD_DIM
    xf = x.reshape(batch * seq, d)

    idx_c, idx_t, ovl, expand = _static_tables(seq)
    rel_bias_t = rel_bias.T
    tabs = (_bias_table(rel_bias_t, idx_c), _bias_table(rel_bias_t, idx_t),
            jnp.asarray(ovl, BF16), jnp.asarray(expand, BF16))

    xf = _nsa_layer(xf, tabs, a0_norm, a0_w_in, a0_cmp_pos, a0_cmp_k_w1, a0_cmp_k_w2, a0_cmp_v_w1, a0_cmp_v_w2,
                    a0_w_out, batch=batch, seq=seq)
    xf = _nsa_layer(xf, tabs, a1_norm, a1_w_in, a1_cmp_pos, a1_cmp_k_w1, a1_cmp_k_w2, a1_cmp_v_w1, a1_cmp_v_w2,
                    a1_w_out, batch=batch, seq=seq)

    kv = _matmul(_rmsnorm(xf, kv_norm, BF16), w_kv.astype(BF16), BF16).reshape(batch, seq, 2 * HD)
    sb_tk = LANES
    tri = jnp.asarray(np.arange(sb_tk)[:, None] > np.arange(sb_tk)[None, :], BF16)
    xf = _sb_layer(xf, kv, tri, b2_norm, b2_w_in, b2_w_out, batch=batch, seq=seq)
    xf = _sb_layer(xf, kv, tri, b3_norm, b3_w_in, b3_w_out, batch=batch, seq=seq)

    return _rmsnorm(xf, final_norm, F32).reshape(batch, seq, d)
```

```python
import functools
import math

import numpy as np
import jax
import jax.numpy as jnp
from jax import lax
from jax.experimental import pallas as pl
from jax.experimental.pallas import tpu as pltpu

N_HEADS = 32
HEAD_DIM = 128
N_KV_GROUPS = 4
CMP_BLOCK = 32
CMP_STRIDE = 16
SLC_BLOCK = 64
N_SELECT = 16
WINDOW = 512
N_BUCKETS = 32
MAX_DISTANCE = 128
RMS_EPS = 1e-6
FORCE_SCORE = 1e6

LANES = 128
NEG_BIG = -1e30
VMEM_LIMIT = 56 * 1024 * 1024

F32 = jnp.float32
BF16 = jnp.bfloat16


def _cparams(sem):
    return pltpu.CompilerParams(dimension_semantics=sem, vmem_limit_bytes=VMEM_LIMIT)


def _dot(a, b):
    return jnp.dot(a, b, preferred_element_type=F32)


def _dot_nt(a, b):
    return lax.dot_general(a, b, (((1,), (1,)), ((), ())), preferred_element_type=F32)


def _rmsnorm_kernel(x_ref, g_ref, o_ref):
    x = x_ref[...]
    ms = jnp.mean(x * x, axis=-1, keepdims=True)
    o_ref[...] = (x * lax.rsqrt(ms + RMS_EPS) * g_ref[...]).astype(o_ref.dtype)


def _rmsnorm(x, g, out_dtype, tm=512):
    m, d = x.shape
    tm = min(tm, m)
    return pl.pallas_call(
        _rmsnorm_kernel,
        grid=(m // tm,),
        in_specs=[pl.BlockSpec((tm, d), lambda i: (i, 0)),
                  pl.BlockSpec((1, d), lambda i: (0, 0))],
        out_specs=pl.BlockSpec((tm, d), lambda i: (i, 0)),
        out_shape=jax.ShapeDtypeStruct((m, d), out_dtype),
        compiler_params=_cparams(("parallel",)),
        name="rmsnorm",
    )(x, g.reshape(1, d))


def _matmul_kernel(x_ref, w_ref, o_ref):
    o_ref[...] = _dot(x_ref[...], w_ref[...]).astype(o_ref.dtype)


def _matmul_res_kernel(x_ref, w_ref, r_ref, o_ref):
    o_ref[...] = (r_ref[...] + _dot(x_ref[...], w_ref[...])).astype(o_ref.dtype)


def _matmul(x, w, out_dtype, res=None, tm=512, tn=1024):
    m, k = x.shape
    n = w.shape[1]
    tm, tn = min(tm, m), min(tn, n)
    assert m % tm == 0 and n % tn == 0
    in_specs = [pl.BlockSpec((tm, k), lambda j, i: (i, 0)),
                pl.BlockSpec((k, tn), lambda j, i: (0, j))]
    args = [x, w]
    body = _matmul_kernel
    if res is not None:
        in_specs.append(pl.BlockSpec((tm, tn), lambda j, i: (i, j)))
        args.append(res)
        body = _matmul_res_kernel
    return pl.pallas_call(
        body,
        grid=(n // tn, m // tm),
        in_specs=in_specs,
        out_specs=pl.BlockSpec((tm, tn), lambda j, i: (i, j)),
        out_shape=jax.ShapeDtypeStruct((m, n), out_dtype),
        compiler_params=_cparams(("parallel", "parallel")),
        name="matmul_res" if res is not None else "matmul",
    )(*args)


def _t5_bucket_np(dist):
    max_exact = N_BUCKETS // 2
    d = np.maximum(dist, 0)
    log_ratio = np.log(np.maximum(d, max_exact).astype(np.float32) / np.float32(max_exact))
    large = max_exact + (log_ratio / np.float32(math.log(MAX_DISTANCE / max_exact))
                         * np.float32(N_BUCKETS - max_exact)).astype(np.int32)
    return np.where(d < max_exact, d, np.minimum(large, N_BUCKETS - 1)).astype(np.int32)


def _bias_kernel(rb_ref, idx_ref, o_ref):
    h = pl.program_id(0)
    idx = idx_ref[...]
    acc = jnp.zeros(idx.shape, F32)
    for b in range(N_BUCKETS):
        acc = jnp.where(idx == b, rb_ref[h, b], acc)
    o_ref[0] = acc


def _bias_table(rel_bias_t, bucket_idx):
    h = rel_bias_t.shape[0]
    rows = bucket_idx.shape[0]
    return pl.pallas_call(
        _bias_kernel,
        grid=(h,),
        in_specs=[pl.BlockSpec(memory_space=pltpu.SMEM),
                  pl.BlockSpec((rows, LANES), lambda i: (0, 0))],
        out_specs=pl.BlockSpec((1, rows, LANES), lambda i: (i, 0, 0)),
        out_shape=jax.ShapeDtypeStruct((h, rows, LANES), F32),
        compiler_params=_cparams(("arbitrary",)),
        name="bias_table",
    )(rel_bias_t, jnp.asarray(bucket_idx))


def _compress_kernel(k2_ref, v2_ref, pos_ref, kw1_ref, kw2_ref, vw1_ref, vw2_ref, kc_ref, vc_ref):
    half = pos_ref.shape[1] // 2
    nch = k2_ref.shape[2]
    row = lax.broadcasted_iota(jnp.int32, (nch, HEAD_DIM), 0)

    def one(x_ref, w1_ref, w2_ref, o_ref):
        x = x_ref[0, 0]
        lo = (x + pos_ref[:, :half]).astype(BF16)
        hi = pltpu.roll(x + pos_ref[:, half:], nch - 1, 0).astype(BF16)
        h = _dot(lo, w1_ref[:half, :]) + _dot(hi, w1_ref[half:, :])
        h = h * jax.nn.sigmoid(h)
        o = _dot(h.astype(BF16), w2_ref[...])
        o_ref[0, 0] = jnp.where(row < nch - 1, o, 0.0).astype(o_ref.dtype)

    one(k2_ref, kw1_ref, kw2_ref, kc_ref)
    one(v2_ref, vw1_ref, vw2_ref, vc_ref)


def _compress(k2, v2, pos, kw1, kw2, vw1, vw2):
    b, g, nch, wid = k2.shape
    blk = pl.BlockSpec((1, 1, nch, wid), lambda i, j: (i, j, 0, 0))
    oblk = pl.BlockSpec((1, 1, nch, HEAD_DIM), lambda i, j: (i, j, 0, 0))

    def const(a):
        return pl.BlockSpec(a.shape, lambda i, j: (0,) * a.ndim)

    out = jax.ShapeDtypeStruct((b, g, nch, HEAD_DIM), BF16)
    return pl.pallas_call(
        _compress_kernel,
        grid=(b, g),
        in_specs=[blk, blk, const(pos), const(kw1), const(kw2), const(vw1), const(vw2)],
        out_specs=[oblk, oblk],
        out_shape=[out, out],
        compiler_params=_cparams(("parallel", "parallel")),
        name="compress",
    )(k2, v2, pos, kw1, kw2, vw1, vw2)


def _split3(x):
    a = x.astype(BF16)
    r = x - a.astype(F32)
    b = r.astype(BF16)
    c = (r - b.astype(F32)).astype(BF16)
    return a, b, c


def _nsa_kernel(q_ref, kc_ref, vc_ref, ks_ref, vs_ref, kw_ref, vw_ref, bc_ref, bt_ref, gl_ref,
                zc_ref, zs_ref, zw_ref, ovl_ref, exp_ref, o_ref, mask_scr, *, group, n_slc, n_sel):
    R = group
    QB = LANES
    RQ = R * QB
    i = pl.program_id(1)
    q0 = i * QB
    scale = 1.0 / math.sqrt(HEAD_DIM)

    q = q_ref[0]
    q_all = jnp.concatenate([q[:, r * HEAD_DIM:(r + 1) * HEAD_DIM] for r in range(R)], axis=0)

    qq = lax.broadcasted_iota(jnp.int32, (R, QB, LANES), 1).reshape(RQ, LANES)
    lane = lax.broadcasted_iota(jnp.int32, (RQ, LANES), 1)
    tpos = q0 + qq

    dist_c = tpos - (lane * CMP_STRIDE + (CMP_BLOCK - 1))
    mask_c = dist_c >= 0
    s = _dot_nt(q_all, kc_ref[0, 0]) * scale + bc_ref[...].reshape(RQ, LANES)
    s = jnp.where(mask_c, s, NEG_BIG)
    m = jnp.max(s, axis=1, keepdims=True)
    e = jnp.where(mask_c, jnp.exp(s - m), 0.0)
    p_c = e / jnp.maximum(jnp.sum(e, axis=1, keepdims=True), 1e-30)
    o_c = _dot(p_c.astype(BF16), vc_ref[0, 0])

    psum = p_c[0:QB]
    for r in range(1, R):
        psum = psum + p_c[r * QB:(r + 1) * QB]
    ovl = ovl_ref[...]
    imp = sum(_dot(part, ovl) for part in _split3(psum))
    j = lax.broadcasted_iota(jnp.int32, (QB, LANES), 1)
    t = q0 + lax.broadcasted_iota(jnp.int32, (QB, LANES), 0)
    cur = t // SLC_BLOCK
    forced = (j == 0) | (j == cur) | (j == cur - 1)
    valid = j * SLC_BLOCK <= t
    score = jnp.where(forced, FORCE_SCORE, jnp.where(valid, imp, -1.0))
    score = jnp.where(j < n_slc, score, -2.0)
    rank = jnp.zeros((QB, LANES), F32)
    for jp in range(n_slc):
        col = score[:, jp:jp + 1]
        beats = (col > score) | ((col == score) & (j > jp))
        rank = rank + jnp.where(beats, 1.0, 0.0)
    sel = jnp.where((rank < n_sel) & (j < n_slc), 1.0, 0.0).astype(BF16)
    mask_scr[...] = _dot(sel, exp_ref[...])

    def bias_tile(delta):
        dd = jnp.minimum(delta, 2)
        return bt_ref[:, pl.ds(pl.multiple_of(dd * QB, QB), QB), :].reshape(RQ, LANES)

    def flash_step(kt, carry, k_ref, v_ref, valid_fn):
        m, l, acc = carry
        k0 = pl.multiple_of(kt * QB, QB)
        k = k_ref[0, pl.ds(k0, QB), :]
        v = v_ref[0, pl.ds(k0, QB), :]
        s = _dot_nt(q_all, k) * scale + bias_tile(i - kt)
        ok = valid_fn(k0, tpos - (k0 + lane))
        s = jnp.where(ok, s, NEG_BIG)
        m_new = jnp.maximum(m, jnp.max(s, axis=1, keepdims=True))
        alpha = jnp.exp(m - m_new)
        p = jnp.where(ok, jnp.exp(s - m_new), 0.0)
        l = alpha * l + jnp.sum(p, axis=1, keepdims=True)
        acc = alpha * acc + _dot(p.astype(BF16), v)
        return m_new, l, acc

    init = (jnp.full((RQ, 1), NEG_BIG, F32), jnp.zeros((RQ, 1), F32), jnp.zeros((RQ, HEAD_DIM), F32))

    def sel_valid(k0, dist):
        blk = mask_scr[:, pl.ds(k0, QB)]
        blk = jnp.concatenate([blk] * R, axis=0)
        return (blk > 0.5) & (dist >= 0)

    _, l_s, acc_s = lax.fori_loop(
        0, i + 1, lambda kt, c: flash_step(kt, c, ks_ref, vs_ref, sel_valid), init)
    o_s = acc_s / jnp.maximum(l_s, 1e-30)

    def win_valid(k0, dist):
        return (dist >= 0) & (dist < WINDOW)

    _, l_w, acc_w = lax.fori_loop(
        jnp.maximum(i - WINDOW // QB, 0), i + 1,
        lambda kt, c: flash_step(kt, c, kw_ref, vw_ref, win_valid), init)
    o_w = acc_w / jnp.maximum(l_w, 1e-30)

    gates = jax.nn.sigmoid(gl_ref[0])
    for r in range(R):
        rows = slice(r * QB, (r + 1) * QB)
        cols = slice(r * HEAD_DIM, (r + 1) * HEAD_DIM)
        mixed = jnp.zeros((QB, HEAD_DIM), F32)
        for c, (o, z_ref) in enumerate(((o_c, zc_ref), (o_s, zs_ref), (o_w, zw_ref))):
            z = z_ref[0, :, cols]
            mixed = mixed + gates[:, c * R + r:c * R + r + 1] * (o[rows] * (z * jax.nn.sigmoid(z)))
        o_ref[0, :, cols] = mixed.astype(o_ref.dtype)


def _nsa_attention(qkv, kc, vc, bias_c, bias_t, gl, z, ovl, expand, *, batch, seq):
    G, R = N_KV_GROUPS, N_HEADS // N_KV_GROUPS
    QB = LANES
    nqb = seq // QB
    hd_blocks = N_HEADS
    n_slc = seq // SLC_BLOCK
    n_sel = min(N_SELECT, n_slc)

    def kv_spec(which):
        return pl.BlockSpec((1, seq, HEAD_DIM), lambda g, i, b: (b, 0, hd_blocks + which * G + g))

    def z_spec(c):
        return pl.BlockSpec((1, QB, R * HEAD_DIM), lambda g, i, b: (b, i, c * G + g))

    cmp_spec = pl.BlockSpec((1, 1, LANES, HEAD_DIM), lambda g, i, b: (b, g, 0, 0))
    in_specs = [
        pl.BlockSpec((1, QB, R * HEAD_DIM), lambda g, i, b: (b, i, g)),
        cmp_spec, cmp_spec,
        kv_spec(2), kv_spec(3), kv_spec(4), kv_spec(5),
        pl.BlockSpec((R, QB, LANES), lambda g, i, b: (g, i, 0)),
        pl.BlockSpec((R, 3 * QB, LANES), lambda g, i, b: (g, 0, 0)),
        pl.BlockSpec((1, QB, LANES), lambda g, i, b: (b, i, g)),
        z_spec(0), z_spec(1), z_spec(2),
        pl.BlockSpec(ovl.shape, lambda g, i, b: (0, 0)),
        pl.BlockSpec(expand.shape, lambda g, i, b: (0, 0)),
    ]
    return pl.pallas_call(
        functools.partial(_nsa_kernel, group=R, n_slc=n_slc, n_sel=n_sel),
        grid=(G, nqb, batch),
        in_specs=in_specs,
        out_specs=pl.BlockSpec((1, QB, R * HEAD_DIM), lambda g, i, b: (b, i, g)),
        out_shape=jax.ShapeDtypeStruct((batch, seq, N_HEADS * HEAD_DIM), BF16),
        scratch_shapes=[pltpu.VMEM((QB, seq), F32)],
        compiler_params=_cparams(("parallel", "parallel", "parallel")),
        name="nsa_attention",
    )(qkv, kc, vc, qkv, qkv, qkv, qkv, bias_c, bias_t, gl, z, z, z, ovl, expand)


LOG2E = math.log2(math.e)


def _sb_kernel(qt_ref, k_ref, vt_ref, z_ref, tri_ref, o_ref, *, tq, hp):
    i = pl.program_id(2)
    tk = tq
    tri = tri_ref[...]
    n_sub = tk // LANES

    def chunk(g, k0, carry, diagonal):
        later, acc = carry
        k = k_ref[0, pl.ds(k0, tk), g * HEAD_DIM:(g + 1) * HEAD_DIM]
        vt = vt_ref[0, g, :, pl.ds(k0, tk)]
        logit = _dot(k, qt_ref[0, g])
        sp = jnp.maximum(logit, 0.0) + jnp.log(1.0 + jnp.exp2(-jnp.abs(logit))) * LOG2E
        if diagonal:
            ok = (lax.broadcasted_iota(jnp.int32, (tk, tq), 0) < lax.broadcasted_iota(jnp.int32, (tk, tq), 1))
            sp = jnp.where(ok, sp, 0.0)
        hi = sp.astype(BF16)
        lo = (sp - hi.astype(F32)).astype(BF16)
        parts = [None] * n_sub
        for c in reversed(range(n_sub)):
            rows = slice(c * LANES, (c + 1) * LANES)
            within = _dot(tri, jnp.concatenate([hi[rows], lo[rows]], axis=0))
            parts[c] = within + later
            later = later + within[0:1] + sp[c * LANES:c * LANES + 1]
        between = jnp.concatenate(parts, axis=0)
        a = jnp.exp2(logit - sp - between)
        if diagonal:
            a = jnp.where(ok, a, 0.0)
        acc = acc + _dot(vt, a.astype(BF16))
        return later, acc

    def all_heads(k0, carries, diagonal):
        return tuple(chunk(g, k0, carries[g], diagonal) for g in range(hp))

    zero = (jnp.zeros((1, tq), F32), jnp.zeros((HEAD_DIM, tq), F32))
    carries = all_heads(pl.multiple_of(i * tq, tq), (zero,) * hp, True)
    carries = lax.fori_loop(
        0, i, lambda step, c: all_heads(pl.multiple_of((i - 1 - step) * tk, tk), c, False), carries)
    for g in range(hp):
        cols = slice(g * HEAD_DIM, (g + 1) * HEAD_DIM)
        z = z_ref[0, :, cols]
        o_ref[0, :, cols] = (carries[g][1].T * (z * jax.nn.sigmoid(z))).astype(o_ref.dtype)


def _sb_attention(qt, kv, vt, z, tri, *, batch, seq, tq=256, hp=4):
    H = N_HEADS
    tq = min(tq, seq)
    wid = hp * HEAD_DIM
    return pl.pallas_call(
        functools.partial(_sb_kernel, tq=tq, hp=hp),
        grid=(batch, H // hp, seq // tq),
        in_specs=[
            pl.BlockSpec((1, hp, HEAD_DIM, tq), lambda b, h, i: (b, h, 0, i)),
            pl.BlockSpec((1, seq, wid), lambda b, h, i: (b, 0, h)),
            pl.BlockSpec((1, hp, HEAD_DIM, seq), lambda b, h, i: (b, h, 0, 0)),
            pl.BlockSpec((1, tq, wid), lambda b, h, i: (b, i, h)),
            pl.BlockSpec(tri.shape, lambda b, h, i: (0, 0)),
        ],
        out_specs=pl.BlockSpec((1, tq, wid), lambda b, h, i: (b, i, h)),
        out_shape=jax.ShapeDtypeStruct((batch, seq, H * HEAD_DIM), BF16),
        compiler_params=_cparams(("parallel", "parallel", "parallel")),
        name="sb_attention",
    )(qt, kv, vt, z, tri)


def _static_tables(seq):
    QB = LANES
    nch = seq // CMP_STRIDE
    n_slc = seq // SLC_BLOCK
    assert nch == LANES and n_slc <= LANES
    tq = np.arange(seq)[:, None]
    n = np.arange(LANES)[None, :]
    idx_c = _t5_bucket_np(tq - (n * CMP_STRIDE + CMP_BLOCK - 1))
    rows = np.arange(3 * QB)[:, None]
    idx_t = _t5_bucket_np((rows // QB) * QB + (rows % QB) - n)
    cmp_start = np.arange(LANES) * CMP_STRIDE
    slc_start = np.arange(LANES) * SLC_BLOCK
    ovl = ((cmp_start[:, None] < slc_start[None, :] + SLC_BLOCK)
           & (cmp_start[:, None] + CMP_BLOCK - 1 >= slc_start[None, :]))
    ovl = ovl & (np.arange(LANES)[:, None] < nch - 1) & (np.arange(LANES)[None, :] < n_slc)
    expand = (np.arange(seq)[None, :] // SLC_BLOCK) == np.arange(LANES)[:, None]
    return idx_c, idx_t, ovl.astype(np.float32), expand.astype(np.float32)


def _nsa_layer(xf, tabs, norm, w_in, cmp_pos, kw1, kw2, vw1, vw2, w_out, *, batch, seq):
    bias_c, bias_t, ovl, expand = tabs
    d = xf.shape[1]
    H, G, Dh = N_HEADS, N_KV_GROUPS, HEAD_DIM
    R = H // G
    HD, GD = H * Dh, G * Dh
    n_qkv = HD + 6 * GD
    hn = _rmsnorm(xf, norm, BF16)
    w_qkv = w_in[:, :n_qkv].astype(BF16)
    w_gate = w_in[:, n_qkv:n_qkv + 3 * H].reshape(d, 3, G, R).transpose(0, 2, 1, 3).reshape(d, G, 3 * R)
    w_gate = jnp.pad(w_gate, ((0, 0), (0, 0), (0, LANES - 3 * R))).reshape(d, G * LANES).astype(BF16)
    w_z = w_in[:, n_qkv + 3 * H:].astype(BF16)
    qkv = _matmul(hn, w_qkv, BF16)
    gl = _matmul(hn, w_gate, F32)
    z = _matmul(hn, w_z, F32)

    nch = seq // CMP_STRIDE

    def chunks(raw):
        t = raw.astype(F32).reshape(batch, nch, CMP_STRIDE, G, Dh).transpose(0, 3, 1, 2, 4)
        return t.reshape(batch, G, nch, CMP_STRIDE * Dh)

    kc, vc = _compress(chunks(qkv[:, HD:HD + GD]), chunks(qkv[:, HD + GD:HD + 2 * GD]),
                       cmp_pos.reshape(1, CMP_BLOCK * Dh),
                       kw1.astype(BF16), kw2.astype(BF16), vw1.astype(BF16), vw2.astype(BF16))
    mixed = _nsa_attention(qkv.reshape(batch, seq, n_qkv), kc, vc, bias_c, bias_t,
                           gl.reshape(batch, seq, G * LANES), z.reshape(batch, seq, 3 * HD),
                           ovl, expand, batch=batch, seq=seq)
    return _matmul(mixed.reshape(batch * seq, HD), w_out.astype(BF16), F32, res=xf)


def _sb_layer(xf, kv, vt, tri, norm, w_in, w_out, *, batch, seq):
    H, Dh = N_HEADS, HEAD_DIM
    HD = H * Dh
    hn = _rmsnorm(xf, norm, BF16)
    q = _matmul(hn, (w_in[:, :HD] * (LOG2E / math.sqrt(Dh))).astype(BF16), BF16)
    z = _matmul(hn, w_in[:, HD:].astype(BF16), F32)
    qt = q.reshape(batch, seq, H, Dh).transpose(0, 2, 3, 1)
    o = _sb_attention(qt, kv, vt, z.reshape(batch, seq, HD), tri, batch=batch, seq=seq)
    return _matmul(o.reshape(batch * seq, HD), w_out.astype(BF16), F32, res=xf)


def kernel(x, rel_bias, a0_norm, a0_w_in, a0_cmp_pos, a0_cmp_k_w1, a0_cmp_k_w2, a0_cmp_v_w1, a0_cmp_v_w2, a0_w_out, a1_norm, a1_w_in, a1_cmp_pos, a1_cmp_k_w1, a1_cmp_k_w2, a1_cmp_v_w1, a1_cmp_v_w2, a1_w_out, kv_norm, w_kv, b2_norm, b2_w_in, b2_w_out, b3_norm, b3_w_in, b3_w_out, final_norm):
    batch, seq, d = x.shape
    HD = N_HEADS * HEAD_DIM
    xf = x.reshape(batch * seq, d)

    idx_c, idx_t, ovl, expand = _static_tables(seq)
    rel_bias_t = rel_bias.T
    tabs = (_bias_table(rel_bias_t, idx_c), _bias_table(rel_bias_t, idx_t),
            jnp.asarray(ovl, BF16), jnp.asarray(expand, BF16))

    xf = _nsa_layer(xf, tabs, a0_norm, a0_w_in, a0_cmp_pos, a0_cmp_k_w1, a0_cmp_k_w2, a0_cmp_v_w1, a0_cmp_v_w2,
                    a0_w_out, batch=batch, seq=seq)
    xf = _nsa_layer(xf, tabs, a1_norm, a1_w_in, a1_cmp_pos, a1_cmp_k_w1, a1_cmp_k_w2, a1_cmp_v_w1, a1_cmp_v_w2,
                    a1_w_out, batch=batch, seq=seq)

    kv = _matmul(_rmsnorm(xf, kv_norm, BF16), w_kv.astype(BF16), BF16).reshape(batch, seq, 2 * HD)
    vt = kv[:, :, HD:].reshape(batch, seq, N_HEADS, HEAD_DIM).transpose(0, 2, 3, 1)
    m = np.arange(LANES)[None, :] > np.arange(LANES)[:, None]
    tri = jnp.asarray(np.concatenate([m, m], axis=1), BF16)
    xf = _sb_layer(xf, kv, vt, tri, b2_norm, b2_w_in, b2_w_out, batch=batch, seq=seq)
    xf = _sb_layer(xf, kv, vt, tri, b3_norm, b3_w_in, b3_w_out, batch=batch, seq=seq)

    return _rmsnorm(xf, final_norm, F32).reshape(batch, seq, d)
```

```python
import functools
import math

import numpy as np
import jax
import jax.numpy as jnp
from jax import lax
from jax.experimental import pallas as pl
from jax.experimental.pallas import tpu as pltpu

N_HEADS = 32
HEAD_DIM = 128
N_KV_GROUPS = 4
CMP_BLOCK = 32
CMP_STRIDE = 16
SLC_BLOCK = 64
N_SELECT = 16
WINDOW = 512
N_BUCKETS = 32
MAX_DISTANCE = 128
RMS_EPS = 1e-6
FORCE_SCORE = 1e6

LANES = 128
NEG_BIG = -1e30
VMEM_LIMIT = 56 * 1024 * 1024
LOG2E = math.log2(math.e)

F32 = jnp.float32
BF16 = jnp.bfloat16


def _cparams(sem):
    return pltpu.CompilerParams(dimension_semantics=sem, vmem_limit_bytes=VMEM_LIMIT)


def _dot(a, b):
    return jnp.dot(a, b, preferred_element_type=F32)


def _rmsnorm_kernel(x_ref, g_ref, o_ref):
    x = x_ref[...]
    ms = jnp.mean(x * x, axis=-1, keepdims=True)
    o_ref[...] = (x * lax.rsqrt(ms + RMS_EPS) * g_ref[...]).astype(o_ref.dtype)


def _rmsnorm(x, g, out_dtype, tm=512):
    m, d = x.shape
    tm = min(tm, m)
    return pl.pallas_call(
        _rmsnorm_kernel,
        grid=(m // tm,),
        in_specs=[pl.BlockSpec((tm, d), lambda i: (i, 0)),
                  pl.BlockSpec((1, d), lambda i: (0, 0))],
        out_specs=pl.BlockSpec((tm, d), lambda i: (i, 0)),
        out_shape=jax.ShapeDtypeStruct((m, d), out_dtype),
        compiler_params=_cparams(("parallel",)),
        name="rmsnorm",
    )(x, g.reshape(1, d))


def _matmul_kernel(x_ref, w_ref, o_ref):
    o_ref[...] = _dot(x_ref[...], w_ref[...]).astype(o_ref.dtype)


def _matmul_res_kernel(x_ref, w_ref, r_ref, o_ref):
    o_ref[...] = (r_ref[...] + _dot(x_ref[...], w_ref[...])).astype(o_ref.dtype)


def _matmul(x, w, out_dtype, res=None, tm=512, tn=1024):
    m, k = x.shape
    n = w.shape[1]
    tm, tn = min(tm, m), min(tn, n)
    assert m % tm == 0 and n % tn == 0
    in_specs = [pl.BlockSpec((tm, k), lambda j, i: (i, 0)),
                pl.BlockSpec((k, tn), lambda j, i: (0, j))]
    args = [x, w]
    body = _matmul_kernel
    if res is not None:
        in_specs.append(pl.BlockSpec((tm, tn), lambda j, i: (i, j)))
        args.append(res)
        body = _matmul_res_kernel
    return pl.pallas_call(
        body,
        grid=(n // tn, m // tm),
        in_specs=in_specs,
        out_specs=pl.BlockSpec((tm, tn), lambda j, i: (i, j)),
        out_shape=jax.ShapeDtypeStruct((m, n), out_dtype),
        compiler_params=_cparams(("parallel", "parallel")),
        name="matmul_res" if res is not None else "matmul",
    )(*args)


MASKED_BUCKET = N_BUCKETS


def _t5_bucket_np(dist):
    max_exact = N_BUCKETS // 2
    d = np.maximum(dist, 0)
    log_ratio = np.log(np.maximum(d, max_exact).astype(np.float32) / np.float32(max_exact))
    large = max_exact + (log_ratio / np.float32(math.log(MAX_DISTANCE / max_exact))
                         * np.float32(N_BUCKETS - max_exact)).astype(np.int32)
    return np.where(d < max_exact, d, np.minimum(large, N_BUCKETS - 1)).astype(np.int32)


def _bias_kernel(rb_ref, idx_ref, o_ref):
    h = pl.program_id(0)
    idx = idx_ref[...]
    acc = jnp.zeros(idx.shape, F32)
    for b in range(N_BUCKETS + 1):
        acc = jnp.where(idx == b, rb_ref[h, b], acc)
    o_ref[...] = acc


def _bias_table(rel_bias_ext, bucket_idx):
    h = rel_bias_ext.shape[0]
    rows = bucket_idx.shape[0]
    return pl.pallas_call(
        _bias_kernel,
        grid=(h,),
        in_specs=[pl.BlockSpec(memory_space=pltpu.SMEM),
                  pl.BlockSpec((rows, LANES), lambda i: (0, 0))],
        out_specs=pl.BlockSpec((rows, LANES), lambda i: (0, i)),
        out_shape=jax.ShapeDtypeStruct((rows, h * LANES), F32),
        compiler_params=_cparams(("arbitrary",)),
        name="bias_table",
    )(rel_bias_ext, jnp.asarray(bucket_idx))


def _compress_kernel(k2_ref, v2_ref, pos_ref, kw1_ref, kw2_ref, vw1_ref, vw2_ref, kc_ref, vct_ref):
    half = pos_ref.shape[1] // 2
    nch = k2_ref.shape[2]
    row = lax.broadcasted_iota(jnp.int32, (nch, HEAD_DIM), 0)

    def one(x_ref, w1_ref, w2_ref):
        x = x_ref[0, 0]
        lo = (x + pos_ref[:, :half]).astype(BF16)
        hi = pltpu.roll(x + pos_ref[:, half:], nch - 1, 0).astype(BF16)
        h = _dot(lo, w1_ref[:half, :]) + _dot(hi, w1_ref[half:, :])
        h = h * jax.nn.sigmoid(h)
        o = _dot(h.astype(BF16), w2_ref[...])
        return jnp.where(row < nch - 1, o, 0.0)

    kc_ref[0, 0] = one(k2_ref, kw1_ref, kw2_ref).astype(kc_ref.dtype)
    vct_ref[0, 0] = one(v2_ref, vw1_ref, vw2_ref).T.astype(vct_ref.dtype)


def _compress(k2, v2, pos, kw1, kw2, vw1, vw2):
    b, g, nch, wid = k2.shape
    blk = pl.BlockSpec((1, 1, nch, wid), lambda i, j: (i, j, 0, 0))
    oblk = pl.BlockSpec((1, 1, nch, HEAD_DIM), lambda i, j: (i, j, 0, 0))

    def const(a):
        return pl.BlockSpec(a.shape, lambda i, j: (0,) * a.ndim)

    out = jax.ShapeDtypeStruct((b, g, nch, HEAD_DIM), BF16)
    return pl.pallas_call(
        _compress_kernel,
        grid=(b, g),
        in_specs=[blk, blk, const(pos), const(kw1), const(kw2), const(vw1), const(vw2)],
        out_specs=[oblk, oblk],
        out_shape=[out, out],
        compiler_params=_cparams(("parallel", "parallel")),
        name="compress",
    )(k2, v2, pos, kw1, kw2, vw1, vw2)


def _split3(x):
    a = x.astype(BF16)
    r = x - a.astype(F32)
    b = r.astype(BF16)
    c = (r - b.astype(F32)).astype(BF16)
    return a, b, c


def _nsa_kernel(qt_ref, kc_ref, vct_ref, ks_ref, vst_ref, kw_ref, vwt_ref, bc_ref, bt_ref, gl_ref,
                zc_ref, zs_ref, zw_ref, ovl_ref, o_ref, sel_scr, acc_scr, *, group, n_slc, n_sel):
    R = group
    QB = LANES
    W = R * QB
    i = pl.program_id(1)
    q0 = i * QB
    qt = qt_ref[0, 0, 0]

    qq = lax.broadcasted_iota(jnp.int32, (QB, W), 1) & (QB - 1)
    row = lax.broadcasted_iota(jnp.int32, (QB, W), 0)
    mask_c = (q0 + qq) - (row * CMP_STRIDE + (CMP_BLOCK - 1)) >= 0
    s = jnp.where(mask_c, _dot(kc_ref[0, 0], qt) + bc_ref[...], NEG_BIG)
    m = jnp.max(s, axis=0, keepdims=True)
    e = jnp.where(mask_c, jnp.exp2(s - m), 0.0)
    p_c = e * (1.0 / jnp.maximum(jnp.sum(e, axis=0, keepdims=True), 1e-30))
    o_c = _dot(vct_ref[0, 0], p_c.astype(BF16))

    psum = p_c[:, 0:QB]
    for r in range(1, R):
        psum = psum + p_c[:, r * QB:(r + 1) * QB]
    ovl = ovl_ref[...]
    imp = sum(_dot(ovl, part) for part in _split3(psum))[:n_slc]
    j = lax.broadcasted_iota(jnp.int32, (n_slc, QB), 0)
    t = q0 + lax.broadcasted_iota(jnp.int32, (n_slc, QB), 1)
    cur = t // SLC_BLOCK
    forced = (j == 0) | (j == cur) | (j == cur - 1)
    score = jnp.where(forced, FORCE_SCORE, jnp.where(j * SLC_BLOCK <= t, imp, -1.0))
    rank = jnp.zeros((n_slc, QB), F32)
    for jp in range(n_slc):
        other = score[jp:jp + 1, :]
        beats = (other > score) | ((other == score) & (j > jp))
        rank = rank + jnp.where(beats, 1.0, 0.0)
    sel_scr[...] = jnp.where(rank < n_sel, 0.0, NEG_BIG)

    def softmax_sweep(k_ref, vt_ref, lo, tile_of, block_mask):
        def scores(kt, tile):
            k0 = pl.multiple_of(kt * QB, QB)
            s = _dot(k_ref[0, pl.ds(k0, QB), :], qt)
            return s + bt_ref[pl.ds(pl.multiple_of(tile * QB, QB), QB), :], k0

        s, k0 = scores(i, 0)
        m = jnp.max(s, axis=0, keepdims=True)
        p = jnp.exp2(s - m)
        l = jnp.sum(p, axis=0, keepdims=True)
        acc_scr[...] = _dot(vt_ref[0, 0, :, pl.ds(k0, QB)], p.astype(BF16))

        def body(kt, carry):
            m, l = carry
            s, k0 = scores(kt, tile_of(i - kt))
            if block_mask:
                half = SLC_BLOCK
                blk = kt * (QB // SLC_BLOCK)
                add = jnp.concatenate(
                    [jnp.broadcast_to(sel_scr[pl.ds(blk + c, 1), :], (half, QB)) for c in range(QB // half)], axis=0)
                s = s + jnp.concatenate([add] * R, axis=1)
            m_new = jnp.maximum(m, jnp.max(s, axis=0, keepdims=True))
            alpha = jnp.exp2(m - m_new)
            p = jnp.exp2(s - m_new)
            l = alpha * l + jnp.sum(p, axis=0, keepdims=True)
            acc_scr[...] = alpha * acc_scr[...] + _dot(vt_ref[0, 0, :, pl.ds(k0, QB)], p.astype(BF16))
            return m_new, l

        _, l = lax.fori_loop(lo, i, body, (m, l))
        return acc_scr[...] * (1.0 / l)

    o_s = softmax_sweep(ks_ref, vst_ref, 0, lambda d: jnp.minimum(d, 2), True)
    far = WINDOW // QB
    o_w = softmax_sweep(kw_ref, vwt_ref, jnp.maximum(i - far, 0),
                        lambda d: jnp.where(d == far, 3, jnp.minimum(d, 2)), False)

    gates = jax.nn.sigmoid(gl_ref[0]).T
    for r in range(R):
        cols = slice(r * QB, (r + 1) * QB)
        mixed = jnp.zeros((QB, HEAD_DIM), F32)
        for c, (o, z_ref) in enumerate(((o_c, zc_ref), (o_s, zs_ref), (o_w, zw_ref))):
            z = z_ref[0, :, cols]
            gated = o[:, cols] * gates[c * R + r:c * R + r + 1, :]
            mixed = mixed + gated.T * (z * jax.nn.sigmoid(z))
        o_ref[0, :, cols] = mixed.astype(o_ref.dtype)


def _nsa_attention(qt, qkv, kc, vct, vst, vwt, bias_c, bias_t, gl, z, ovl, *, batch, seq):
    G, R = N_KV_GROUPS, N_HEADS // N_KV_GROUPS
    QB = LANES
    W = R * QB
    nqb = seq // QB
    n_slc = seq // SLC_BLOCK
    n_sel = min(N_SELECT, n_slc)

    def k_spec(which):
        return pl.BlockSpec((1, seq, HEAD_DIM), lambda g, i, b: (b, 0, N_HEADS + which * G + g))

    def z_spec(c):
        return pl.BlockSpec((1, QB, R * HEAD_DIM), lambda g, i, b: (b, i, c * G + g))

    vt_spec = pl.BlockSpec((1, 1, HEAD_DIM, seq), lambda g, i, b: (b, g, 0, 0))
    cmp_spec = pl.BlockSpec((1, 1, LANES, HEAD_DIM), lambda g, i, b: (b, g, 0, 0))
    in_specs = [
        pl.BlockSpec((1, 1, 1, HEAD_DIM, W), lambda g, i, b: (b, g, i, 0, 0)),
        cmp_spec, cmp_spec,
        k_spec(2), vt_spec, k_spec(4), vt_spec,
        pl.BlockSpec((QB, W), lambda g, i, b: (i, g)),
        pl.BlockSpec((4 * QB, W), lambda g, i, b: (0, g)),
        pl.BlockSpec((1, QB, LANES), lambda g, i, b: (b, i, g)),
        z_spec(0), z_spec(1), z_spec(2),
        pl.BlockSpec(ovl.shape, lambda g, i, b: (0, 0)),
    ]
    return pl.pallas_call(
        functools.partial(_nsa_kernel, group=R, n_slc=n_slc, n_sel=n_sel),
        grid=(G, nqb, batch),
        in_specs=in_specs,
        out_specs=pl.BlockSpec((1, QB, R * HEAD_DIM), lambda g, i, b: (b, i, g)),
        out_shape=jax.ShapeDtypeStruct((batch, seq, N_HEADS * HEAD_DIM), BF16),
        scratch_shapes=[pltpu.VMEM((n_slc, QB), F32), pltpu.VMEM((HEAD_DIM, W), F32)],
        compiler_params=_cparams(("parallel", "parallel", "parallel")),
        name="nsa_attention",
    )(qt, kc, vct, qkv, vst, qkv, vwt, bias_c, bias_t, gl, z, z, z, ovl)


def _sb_kernel(qt_ref, k_ref, vt_ref, z_ref, tri_ref, o_ref, *, tq, hp):
    i = pl.program_id(2)
    tk = tq
    tri = tri_ref[...]
    n_sub = tk // LANES

    def chunk(g, k0, carry, diagonal):
        later, acc = carry
        k = k_ref[0, pl.ds(k0, tk), g * HEAD_DIM:(g + 1) * HEAD_DIM]
        vt = vt_ref[0, g, :, pl.ds(k0, tk)]
        logit = _dot(k, qt_ref[0, g])
        sp = jnp.maximum(logit, 0.0) + jnp.log(1.0 + jnp.exp2(-jnp.abs(logit))) * LOG2E
        if diagonal:
            ok = (lax.broadcasted_iota(jnp.int32, (tk, tq), 0) < lax.broadcasted_iota(jnp.int32, (tk, tq), 1))
            sp = jnp.where(ok, sp, 0.0)
        hi = sp.astype(BF16)
        lo = (sp - hi.astype(F32)).astype(BF16)
        parts = [None] * n_sub
        for c in reversed(range(n_sub)):
            rows = slice(c * LANES, (c + 1) * LANES)
            within = _dot(tri, jnp.concatenate([hi[rows], lo[rows]], axis=0))
            parts[c] = within + later
            later = later + within[0:1] + sp[c * LANES:c * LANES + 1]
        between = jnp.concatenate(parts, axis=0)
        a = jnp.exp2(logit - sp - between)
        if diagonal:
            a = jnp.where(ok, a, 0.0)
        acc = acc + _dot(vt, a.astype(BF16))
        return later, acc

    def all_heads(k0, carries, diagonal):
        return tuple(chunk(g, k0, carries[g], diagonal) for g in range(hp))

    zero = (jnp.zeros((1, tq), F32), jnp.zeros((HEAD_DIM, tq), F32))
    carries = all_heads(pl.multiple_of(i * tq, tq), (zero,) * hp, True)
    carries = lax.fori_loop(
        0, i, lambda step, c: all_heads(pl.multiple_of((i - 1 - step) * tk, tk), c, False), carries)
    for g in range(hp):
        cols = slice(g * HEAD_DIM, (g + 1) * HEAD_DIM)
        z = z_ref[0, :, cols]
        o_ref[0, :, cols] = (carries[g][1].T * (z * jax.nn.sigmoid(z))).astype(o_ref.dtype)


def _sb_attention(qt, kv, vt, z, tri, *, batch, seq, tq=256, hp=4):
    H = N_HEADS
    tq = min(tq, seq)
    wid = hp * HEAD_DIM
    return pl.pallas_call(
        functools.partial(_sb_kernel, tq=tq, hp=hp),
        grid=(batch, H // hp, seq // tq),
        in_specs=[
            pl.BlockSpec((1, hp, HEAD_DIM, tq), lambda b, h, i: (b, h, 0, i)),
            pl.BlockSpec((1, seq, wid), lambda b, h, i: (b, 0, h)),
            pl.BlockSpec((1, hp, HEAD_DIM, seq), lambda b, h, i: (b, h, 0, 0)),
            pl.BlockSpec((1, tq, wid), lambda b, h, i: (b, i, h)),
            pl.BlockSpec(tri.shape, lambda b, h, i: (0, 0)),
        ],
        out_specs=pl.BlockSpec((1, tq, wid), lambda b, h, i: (b, i, h)),
        out_shape=jax.ShapeDtypeStruct((batch, seq, H * HEAD_DIM), BF16),
        compiler_params=_cparams(("parallel", "parallel", "parallel")),
        name="sb_attention",
    )(qt, kv, vt, z, tri)


def _static_tables(seq):
    QB = LANES
    nch = seq // CMP_STRIDE
    n_slc = seq // SLC_BLOCK
    assert nch == LANES and n_slc <= LANES
    q = np.arange(QB)[None, :]
    rows = np.arange(seq)[:, None]
    idx_c = _t5_bucket_np((rows // QB) * QB + q - ((rows % QB) * CMP_STRIDE + CMP_BLOCK - 1))
    k = np.arange(QB)[:, None]
    far = WINDOW // QB
    tiles = []
    for delta, keep in ((0, k <= q), (1, None), (2, None), (far, k > q)):
        idx = _t5_bucket_np(delta * QB + q - k)
        tiles.append(idx if keep is None else np.where(keep, idx, MASKED_BUCKET))
    idx_t = np.concatenate(tiles, axis=0).astype(np.int32)
    cmp_start = np.arange(LANES) * CMP_STRIDE
    slc_start = np.arange(LANES) * SLC_BLOCK
    ovl = ((cmp_start[None, :] < slc_start[:, None] + SLC_BLOCK)
           & (cmp_start[None, :] + CMP_BLOCK - 1 >= slc_start[:, None]))
    ovl = ovl & (np.arange(LANES)[None, :] < nch - 1) & (np.arange(LANES)[:, None] < n_slc)
    return idx_c, idx_t, ovl.astype(np.float32)


def _nsa_layer(xf, tabs, norm, w_in, cmp_pos, kw1, kw2, vw1, vw2, w_out, *, batch, seq):
    bias_c, bias_t, ovl = tabs
    d = xf.shape[1]
    H, G, Dh = N_HEADS, N_KV_GROUPS, HEAD_DIM
    R = H // G
    HD, GD = H * Dh, G * Dh
    QB = LANES
    nqb = seq // QB
    n_qkv = HD + 6 * GD
    hn = _rmsnorm(xf, norm, BF16)
    w_qkv = jnp.concatenate([w_in[:, :HD] * (LOG2E / math.sqrt(Dh)), w_in[:, HD:n_qkv]], axis=1).astype(BF16)
    w_gate = w_in[:, n_qkv:n_qkv + 3 * H].reshape(d, 3, G, R).transpose(0, 2, 1, 3).reshape(d, G, 3 * R)
    w_gate = jnp.pad(w_gate, ((0, 0), (0, 0), (0, LANES - 3 * R))).reshape(d, G * LANES).astype(BF16)
    w_z = w_in[:, n_qkv + 3 * H:].astype(BF16)
    qkv = _matmul(hn, w_qkv, BF16)
    gl = _matmul(hn, w_gate, F32)
    z = _matmul(hn, w_z, F32)

    nch = seq // CMP_STRIDE

    def chunks(raw):
        t = raw.astype(F32).reshape(batch, nch, CMP_STRIDE, G, Dh).transpose(0, 3, 1, 2, 4)
        return t.reshape(batch, G, nch, CMP_STRIDE * Dh)

    def keys_on_lanes(raw):
        return raw.reshape(batch, seq, G, Dh).transpose(0, 2, 3, 1)

    kc, vct = _compress(chunks(qkv[:, HD:HD + GD]), chunks(qkv[:, HD + GD:HD + 2 * GD]),
                        cmp_pos.reshape(1, CMP_BLOCK * Dh),
                        kw1.astype(BF16), kw2.astype(BF16), vw1.astype(BF16), vw2.astype(BF16))
    qt = qkv[:, :HD].reshape(batch, nqb, QB, G, R, Dh).transpose(0, 3, 1, 5, 4, 2).reshape(batch, G, nqb, Dh, R * QB)
    mixed = _nsa_attention(qt, qkv.reshape(batch, seq, n_qkv), kc, vct,
                           keys_on_lanes(qkv[:, HD + 3 * GD:HD + 4 * GD]),
                           keys_on_lanes(qkv[:, HD + 5 * GD:HD + 6 * GD]),
                           bias_c, bias_t, gl.reshape(batch, seq, G * LANES), z.reshape(batch, seq, 3 * HD),
                           ovl, batch=batch, seq=seq)
    return _matmul(mixed.reshape(batch * seq, HD), w_out.astype(BF16), F32, res=xf)


def _sb_layer(xf, kv, vt, tri, norm, w_in, w_out, *, batch, seq):
    H, Dh = N_HEADS, HEAD_DIM
    HD = H * Dh
    hn = _rmsnorm(xf, norm, BF16)
    q = _matmul(hn, (w_in[:, :HD] * (LOG2E / math.sqrt(Dh))).astype(BF16), BF16)
    z = _matmul(hn, w_in[:, HD:].astype(BF16), F32)
    qt = q.reshape(batch, seq, H, Dh).transpose(0, 2, 3, 1)
    o = _sb_attention(qt, kv, vt, z.reshape(batch, seq, HD), tri, batch=batch, seq=seq)
    return _matmul(o.reshape(batch * seq, HD), w_out.astype(BF16), F32, res=xf)


def kernel(x, rel_bias, a0_norm, a0_w_in, a0_cmp_pos, a0_cmp_k_w1, a0_cmp_k_w2, a0_cmp_v_w1, a0_cmp_v_w2, a0_w_out, a1_norm, a1_w_in, a1_cmp_pos, a1_cmp_k_w1, a1_cmp_k_w2, a1_cmp_v_w1, a1_cmp_v_w2, a1_w_out, kv_norm, w_kv, b2_norm, b2_w_in, b2_w_out, b3_norm, b3_w_in, b3_w_out, final_norm):
    batch, seq, d = x.shape
    HD = N_HEADS * HEAD_DIM
    xf = x.reshape(batch * seq, d)

    idx_c, idx_t, ovl = _static_tables(seq)
    rel_bias_ext = jnp.concatenate([rel_bias.T * LOG2E, jnp.full((N_HEADS, 1), NEG_BIG, F32)], axis=1)
    tabs = (_bias_table(rel_bias_ext, idx_c), _bias_table(rel_bias_ext, idx_t), jnp.asarray(ovl, BF16))

    xf = _nsa_layer(xf, tabs, a0_norm, a0_w_in, a0_cmp_pos, a0_cmp_k_w1, a0_cmp_k_w2, a0_cmp_v_w1, a0_cmp_v_w2,
                    a0_w_out, batch=batch, seq=seq)
    xf = _nsa_layer(xf, tabs, a1_norm, a1_w_in, a1_cmp_pos, a1_cmp_k_w1, a1_cmp_k_w2, a1_cmp_v_w1, a1_cmp_v_w2,
                    a1_w_out, batch=batch, seq=seq)

    kv = _matmul(_rmsnorm(xf, kv_norm, BF16), w_kv.astype(BF16), BF16).reshape(batch, seq, 2 * HD)
    vt = kv[:, :, HD:].reshape(batch, seq, N_HEADS, HEAD_DIM).transpose(0, 2, 3, 1)
    m = np.arange(LANES)[None, :] > np.arange(LANES)[:, None]
    tri = jnp.asarray(np.concatenate([m, m], axis=1), BF16)
    xf = _sb_layer(xf, kv, vt, tri, b2_norm, b2_w_in, b2_w_out, batch=batch, seq=seq)
    xf = _sb_layer(xf, kv, vt, tri, b3_norm, b3_w_in, b3_w_out, batch=batch, seq=seq)

    return _rmsnorm(xf, final_norm, F32).reshape(batch, seq, d)
```

```python
import functools
import math

import numpy as np
import jax
import jax.numpy as jnp
from jax import lax
from jax.experimental import pallas as pl
from jax.experimental.pallas import tpu as pltpu

N_HEADS = 32
HEAD_DIM = 128
N_KV_GROUPS = 4
CMP_BLOCK = 32
CMP_STRIDE = 16
SLC_BLOCK = 64
N_SELECT = 16
WINDOW = 512
N_BUCKETS = 32
MAX_DISTANCE = 128
RMS_EPS = 1e-6
FORCE_SCORE = 1e6

LANES = 128
NEG_BIG = -1e30
VMEM_LIMIT = 56 * 1024 * 1024
LOG2E = math.log2(math.e)

F32 = jnp.float32
BF16 = jnp.bfloat16


def _cparams(sem):
    return pltpu.CompilerParams(dimension_semantics=sem, vmem_limit_bytes=VMEM_LIMIT)


def _dot(a, b):
    return jnp.dot(a, b, preferred_element_type=F32)


def _rmsnorm_kernel(x_ref, g_ref, o_ref):
    x = x_ref[...]
    ms = jnp.mean(x * x, axis=-1, keepdims=True)
    o_ref[...] = (x * lax.rsqrt(ms + RMS_EPS) * g_ref[...]).astype(o_ref.dtype)


def _rmsnorm(x, g, out_dtype, tm=512):
    m, d = x.shape
    tm = min(tm, m)
    return pl.pallas_call(
        _rmsnorm_kernel,
        grid=(m // tm,),
        in_specs=[pl.BlockSpec((tm, d), lambda i: (i, 0)),
                  pl.BlockSpec((1, d), lambda i: (0, 0))],
        out_specs=pl.BlockSpec((tm, d), lambda i: (i, 0)),
        out_shape=jax.ShapeDtypeStruct((m, d), out_dtype),
        compiler_params=_cparams(("parallel",)),
        name="rmsnorm",
    )(x, g.reshape(1, d))


def _matmul_kernel(*refs, has_res, has_scale, cast_w):
    it = iter(refs)
    x_ref, w_ref = next(it), next(it)
    s_ref = next(it) if has_scale else None
    r_ref = next(it) if has_res else None
    o_ref = next(it)
    if cast_w:
        wb_ref = next(it)

        @pl.when(pl.program_id(1) == 0)
        def _():
            w = w_ref[...]
            if has_scale:
                w = w * s_ref[...]
            wb_ref[...] = w.astype(BF16)
    else:
        wb_ref = w_ref
    acc = _dot(x_ref[...], wb_ref[...])
    if has_res:
        acc = r_ref[...] + acc
    o_ref[...] = acc.astype(o_ref.dtype)


def _matmul(x, w, out_dtype, *, col0=0, n=None, res=None, colscale=None, tm=1024, tn=512):
    m, k = x.shape
    n = w.shape[1] - col0 if n is None else n
    tm, tn = min(tm, m), min(tn, n)
    assert m % tm == 0 and n % tn == 0 and col0 % tn == 0
    cast_w = w.dtype != BF16
    assert cast_w or colscale is None
    joff = col0 // tn
    in_specs = [pl.BlockSpec((tm, k), lambda j, i: (i, 0)),
                pl.BlockSpec((k, tn), lambda j, i: (0, j + joff))]
    args = [x, w]
    if colscale is not None:
        in_specs.append(pl.BlockSpec((1, tn), lambda j, i: (0, j)))
        args.append(colscale)
    if res is not None:
        in_specs.append(pl.BlockSpec((tm, tn), lambda j, i: (i, j)))
        args.append(res)
    return pl.pallas_call(
        functools.partial(_matmul_kernel, has_res=res is not None, has_scale=colscale is not None, cast_w=cast_w),
        grid=(n // tn, m // tm),
        in_specs=in_specs,
        out_specs=pl.BlockSpec((tm, tn), lambda j, i: (i, j)),
        out_shape=jax.ShapeDtypeStruct((m, n), out_dtype),
        scratch_shapes=[pltpu.VMEM((k, tn), BF16)] if cast_w else [],
        compiler_params=_cparams(("parallel", "arbitrary")),
        name="matmul_res" if res is not None else "matmul",
    )(*args)


MASKED_BUCKET = N_BUCKETS


def _t5_bucket_np(dist):
    max_exact = N_BUCKETS // 2
    d = np.maximum(dist, 0)
    log_ratio = np.log(np.maximum(d, max_exact).astype(np.float32) / np.float32(max_exact))
    large = max_exact + (log_ratio / np.float32(math.log(MAX_DISTANCE / max_exact))
                         * np.float32(N_BUCKETS - max_exact)).astype(np.int32)
    return np.where(d < max_exact, d, np.minimum(large, N_BUCKETS - 1)).astype(np.int32)


def _bias_kernel(rb_ref, idx_ref, o_ref):
    h = pl.program_id(0)
    idx = idx_ref[...]
    acc = jnp.zeros(idx.shape, F32)
    for b in range(N_BUCKETS + 1):
        acc = jnp.where(idx == b, rb_ref[h, b], acc)
    o_ref[...] = acc


def _bias_table(rel_bias_ext, bucket_idx):
    h = rel_bias_ext.shape[0]
    rows = bucket_idx.shape[0]
    return pl.pallas_call(
        _bias_kernel,
        grid=(h,),
        in_specs=[pl.BlockSpec(memory_space=pltpu.SMEM),
                  pl.BlockSpec((rows, LANES), lambda i: (0, 0))],
        out_specs=pl.BlockSpec((rows, LANES), lambda i: (0, i)),
        out_shape=jax.ShapeDtypeStruct((rows, h * LANES), F32),
        compiler_params=_cparams(("arbitrary",)),
        name="bias_table",
    )(rel_bias_ext, jnp.asarray(bucket_idx))


def _compress_kernel(k2_ref, v2_ref, pos_ref, kw1_ref, kw2_ref, vw1_ref, vw2_ref, kc_ref, vct_ref):
    half = pos_ref.shape[1] // 2
    nch = k2_ref.shape[2]
    row = lax.broadcasted_iota(jnp.int32, (nch, HEAD_DIM), 0)

    def one(x_ref, w1_ref, w2_ref):
        x = x_ref[0, 0]
        lo = (x + pos_ref[:, :half]).astype(BF16)
        hi = pltpu.roll(x + pos_ref[:, half:], nch - 1, 0).astype(BF16)
        h = _dot(lo, w1_ref[:half, :]) + _dot(hi, w1_ref[half:, :])
        h = h * jax.nn.sigmoid(h)
        o = _dot(h.astype(BF16), w2_ref[...])
        return jnp.where(row < nch - 1, o, 0.0)

    kc_ref[0, 0] = one(k2_ref, kw1_ref, kw2_ref).astype(kc_ref.dtype)
    vct_ref[0, 0] = one(v2_ref, vw1_ref, vw2_ref).T.astype(vct_ref.dtype)


def _compress(k2, v2, pos, kw1, kw2, vw1, vw2):
    b, g, nch, wid = k2.shape
    blk = pl.BlockSpec((1, 1, nch, wid), lambda i, j: (i, j, 0, 0))
    oblk = pl.BlockSpec((1, 1, nch, HEAD_DIM), lambda i, j: (i, j, 0, 0))

    def const(a):
        return pl.BlockSpec(a.shape, lambda i, j: (0,) * a.ndim)

    out = jax.ShapeDtypeStruct((b, g, nch, HEAD_DIM), BF16)
    return pl.pallas_call(
        _compress_kernel,
        grid=(b, g),
        in_specs=[blk, blk, const(pos), const(kw1), const(kw2), const(vw1), const(vw2)],
        out_specs=[oblk, oblk],
        out_shape=[out, out],
        compiler_params=_cparams(("parallel", "parallel")),
        name="compress",
    )(k2, v2, pos, kw1, kw2, vw1, vw2)


def _split3(x):
    a = x.astype(BF16)
    r = x - a.astype(F32)
    b = r.astype(BF16)
    c = (r - b.astype(F32)).astype(BF16)
    return a, b, c


def _nsa_kernel(q_ref, kc_ref, vct_ref, ks_ref, vs_ref, kw_ref, vw_ref, bc_ref, bt_ref, gl_ref,
                zc_ref, zs_ref, zw_ref, ovl_ref, o_ref, sel_scr, acc_scr, s_scr, p_scr, vst_scr, vwt_scr,
                *, group, n_slc, n_sel):
    R = group
    QB = LANES
    W = R * QB
    i = pl.program_id(2)
    q0 = i * QB

    @pl.when(i == 0)
    def _():
        for c in range(vs_ref.shape[1] // QB):
            cols = slice(c * QB, (c + 1) * QB)
            vst_scr[:, cols] = vs_ref[0, cols, :].T
            vwt_scr[:, cols] = vw_ref[0, cols, :].T

    q = q_ref[0]
    qt = jnp.concatenate([q[:, r * HEAD_DIM:(r + 1) * HEAD_DIM].T for r in range(R)], axis=1)

    qq = lax.broadcasted_iota(jnp.int32, (QB, W), 1) & (QB - 1)
    row = lax.broadcasted_iota(jnp.int32, (QB, W), 0)
    mask_c = (q0 + qq) - (row * CMP_STRIDE + (CMP_BLOCK - 1)) >= 0
    s = jnp.where(mask_c, _dot(kc_ref[0, 0], qt) + bc_ref[...], NEG_BIG)
    m = jnp.max(s, axis=0, keepdims=True)
    e = jnp.where(mask_c, jnp.exp2(s - m), 0.0)
    p_c = e * (1.0 / jnp.maximum(jnp.sum(e, axis=0, keepdims=True), 1e-30))
    o_c = _dot(vct_ref[0, 0], p_c.astype(BF16))

    psum = p_c[:, 0:QB]
    for r in range(1, R):
        psum = psum + p_c[:, r * QB:(r + 1) * QB]
    ovl = ovl_ref[...]
    imp = sum(_dot(ovl, part) for part in _split3(psum))[:n_slc]
    j = lax.broadcasted_iota(jnp.int32, (n_slc, QB), 0)
    t = q0 + lax.broadcasted_iota(jnp.int32, (n_slc, QB), 1)
    cur = t // SLC_BLOCK
    forced = (j == 0) | (j == cur) | (j == cur - 1)
    score = jnp.where(forced, FORCE_SCORE, jnp.where(j * SLC_BLOCK <= t, imp, -1.0))
    rank = jnp.zeros((n_slc, QB), F32)
    for jp in range(n_slc):
        other = score[jp:jp + 1, :]
        beats = (other > score) | ((other == score) & (j > jp))
        rank = rank + jnp.where(beats, 1.0, 0.0)
    sel_scr[...] = jnp.where(rank < n_sel, 0.0, NEG_BIG)

    def softmax_sweep(k_ref, vt_ref, lo, tile_of, block_mask):
        def raw_scores(kt):
            return _dot(k_ref[0, pl.ds(pl.multiple_of(kt * QB, QB), QB), :], qt)

        def bias(tile):
            return bt_ref[pl.ds(pl.multiple_of(tile * QB, QB), QB), :]

        def accumulate(kt, alpha):
            acc_scr[...] = alpha * acc_scr[...] + _dot(vt_ref[:, pl.ds(pl.multiple_of(kt * QB, QB), QB)], p_scr[...])

        s = raw_scores(i) + bias(0)
        m = jnp.max(s, axis=0, keepdims=True)
        p = jnp.exp2(s - m)
        l = jnp.sum(p, axis=0, keepdims=True)
        acc_scr[...] = jnp.zeros_like(acc_scr)
        p_scr[...] = p.astype(BF16)
        s_scr[...] = raw_scores(lo)

        def body(kt, carry):
            m, l, alpha_prev = carry
            accumulate(jnp.where(kt == lo, i, kt - 1), alpha_prev)
            s = s_scr[...] + bias(tile_of(i - kt))
            if block_mask:
                half = SLC_BLOCK
                blk = kt * (QB // SLC_BLOCK)
                add = jnp.concatenate(
                    [jnp.broadcast_to(sel_scr[pl.ds(blk + c, 1), :], (half, QB)) for c in range(QB // half)], axis=0)
                s = s + jnp.concatenate([add] * R, axis=1)
            m_new = jnp.maximum(m, jnp.max(s, axis=0, keepdims=True))
            alpha = jnp.exp2(m - m_new)
            p = jnp.exp2(s - m_new)
            l = alpha * l + jnp.sum(p, axis=0, keepdims=True)
            p_scr[...] = p.astype(BF16)
            s_scr[...] = raw_scores(jnp.minimum(kt + 1, i))
            return m_new, l, alpha

        _, l, alpha_last = lax.fori_loop(lo, i, body, (m, l, jnp.ones((1, W), F32)))
        accumulate(jnp.where(i > lo, i - 1, i), alpha_last)
        return acc_scr[...] * (1.0 / l)

    o_s = softmax_sweep(ks_ref, vst_scr, 0, lambda d: jnp.minimum(d, 2), True)
    far = WINDOW // QB
    o_w = softmax_sweep(kw_ref, vwt_scr, jnp.maximum(i - far, 0),
                        lambda d: jnp.where(d == far, 3, jnp.minimum(d, 2)), False)

    gates = jax.nn.sigmoid(gl_ref[0]).T
    for r in range(R):
        cols = slice(r * QB, (r + 1) * QB)
        mixed = jnp.zeros((QB, HEAD_DIM), F32)
        for c, (o, z_ref) in enumerate(((o_c, zc_ref), (o_s, zs_ref), (o_w, zw_ref))):
            z = z_ref[0, :, cols]
            gated = o[:, cols] * gates[c * R + r:c * R + r + 1, :]
            mixed = mixed + gated.T * (z * jax.nn.sigmoid(z))
        o_ref[0, :, cols] = mixed.astype(o_ref.dtype)


def _nsa_attention(qkv, kc, vct, bias_c, bias_t, gl, z, ovl, *, batch, seq):
    G, R = N_KV_GROUPS, N_HEADS // N_KV_GROUPS
    QB = LANES
    W = R * QB
    nqb = seq // QB
    n_slc = seq // SLC_BLOCK
    n_sel = min(N_SELECT, n_slc)

    def kv_spec(which):
        return pl.BlockSpec((1, seq, HEAD_DIM), lambda g, b, i: (b, 0, N_HEADS + which * G + g))

    def z_spec(c):
        return pl.BlockSpec((1, QB, R * HEAD_DIM), lambda g, b, i: (b, i, c * G + g))

    cmp_spec = pl.BlockSpec((1, 1, LANES, HEAD_DIM), lambda g, b, i: (b, g, 0, 0))
    in_specs = [
        pl.BlockSpec((1, QB, R * HEAD_DIM), lambda g, b, i: (b, i, g)),
        cmp_spec, cmp_spec,
        kv_spec(2), kv_spec(3), kv_spec(4), kv_spec(5),
        pl.BlockSpec((QB, W), lambda g, b, i: (i, g)),
        pl.BlockSpec((4 * QB, W), lambda g, b, i: (0, g)),
        pl.BlockSpec((1, QB, LANES), lambda g, b, i: (b, i, g)),
        z_spec(0), z_spec(1), z_spec(2),
        pl.BlockSpec(ovl.shape, lambda g, b, i: (0, 0)),
    ]
    return pl.pallas_call(
        functools.partial(_nsa_kernel, group=R, n_slc=n_slc, n_sel=n_sel),
        grid=(G, batch, nqb),
        in_specs=in_specs,
        out_specs=pl.BlockSpec((1, QB, R * HEAD_DIM), lambda g, b, i: (b, i, g)),
        out_shape=jax.ShapeDtypeStruct((batch, seq, N_HEADS * HEAD_DIM), BF16),
        scratch_shapes=[pltpu.VMEM((n_slc, QB), F32), pltpu.VMEM((HEAD_DIM, W), F32),
                        pltpu.VMEM((QB, W), F32), pltpu.VMEM((QB, W), BF16),
                        pltpu.VMEM((HEAD_DIM, seq), BF16), pltpu.VMEM((HEAD_DIM, seq), BF16)],
        compiler_params=_cparams(("parallel", "parallel", "arbitrary")),
        name="nsa_attention",
    )(qkv, kc, vct, qkv, qkv, qkv, qkv, bias_c, bias_t, gl, z, z, z, ovl)


def _sb_kernel(q_ref, k_ref, v_ref, z_ref, tri_ref, o_ref, vt_scr, l_scr, d_scr, w_scr, acc_scr, *, tq, hp):
    i = pl.program_id(2)
    tk = tq
    tri = tri_ref[...]
    n_sub = tk // LANES
    heads = range(hp)

    @pl.when(i == 0)
    def _():
        for g in heads:
            for c in range(v_ref.shape[1] // LANES):
                cols = slice(c * LANES, (c + 1) * LANES)
                vt_scr[g, :, cols] = v_ref[0, cols, g * HEAD_DIM:(g + 1) * HEAD_DIM].T

    qts = [q_ref[0, :, g * HEAD_DIM:(g + 1) * HEAD_DIM].T for g in heads]

    def key0(t):
        return pl.multiple_of(jnp.maximum(i - t, 0) * tk, tk)

    def logits(t):
        k0 = key0(t)
        for g in heads:
            l_scr[g] = _dot(k_ref[0, pl.ds(k0, tk), g * HEAD_DIM:(g + 1) * HEAD_DIM], qts[g])

    def softplus_sums(diagonal):
        totals = []
        for g in heads:
            logit = l_scr[g]
            sp = jnp.maximum(logit, 0.0) + jnp.log(1.0 + jnp.exp2(-jnp.abs(logit))) * LOG2E
            d = logit - sp
            if diagonal:
                ok = (lax.broadcasted_iota(jnp.int32, (tk, tq), 0) < lax.broadcasted_iota(jnp.int32, (tk, tq), 1))
                sp = jnp.where(ok, sp, 0.0)
                d = jnp.where(ok, d, NEG_BIG)
            d_scr[g] = d
            hi = sp.astype(BF16)
            lo = (sp - hi.astype(F32)).astype(BF16)
            tot = []
            for c in range(n_sub):
                rows = slice(c * LANES, (c + 1) * LANES)
                within = _dot(tri, jnp.concatenate([hi[rows], lo[rows]], axis=0))
                w_scr[g, rows, :] = within
                tot.append(within[0:1] + sp[c * LANES:c * LANES + 1])
            totals.append(tuple(tot))
        return tuple(totals)

    def weigh_values(t, laters, totals):
        k0 = key0(t)
        out = []
        for g in heads:
            later = laters[g]
            parts = [None] * n_sub
            for c in reversed(range(n_sub)):
                rows = slice(c * LANES, (c + 1) * LANES)
                parts[c] = jnp.exp2(d_scr[g, rows, :] - w_scr[g, rows, :] - later)
                later = later + totals[g][c]
            a = jnp.concatenate(parts, axis=0).astype(BF16)
            acc_scr[g] += _dot(vt_scr[g, :, pl.ds(k0, tk)], a)
            out.append(later)
        return tuple(out)

    acc_scr[...] = jnp.zeros_like(acc_scr)
    logits(0)
    totals = softplus_sums(True)
    logits(1)

    def body(t, carry):
        laters, totals = carry
        laters = weigh_values(t - 1, laters, totals)
        totals = softplus_sums(False)
        logits(t + 1)
        return laters, totals

    laters, totals = lax.fori_loop(1, i + 1, body, ((jnp.zeros((1, tq), F32),) * hp, totals))
    weigh_values(i, laters, totals)
    for g in heads:
        cols = slice(g * HEAD_DIM, (g + 1) * HEAD_DIM)
        z = z_ref[0, :, cols]
        o_ref[0, :, cols] = (acc_scr[g].T * (z * jax.nn.sigmoid(z))).astype(o_ref.dtype)


def _sb_attention(q, kv, z, tri, *, batch, seq, tq=256, hp=4):
    H = N_HEADS
    tq = min(tq, seq)
    wid = hp * HEAD_DIM
    return pl.pallas_call(
        functools.partial(_sb_kernel, tq=tq, hp=hp),
        grid=(batch, H // hp, seq // tq),
        in_specs=[
            pl.BlockSpec((1, tq, wid), lambda b, h, i: (b, i, h)),
            pl.BlockSpec((1, seq, wid), lambda b, h, i: (b, 0, h)),
            pl.BlockSpec((1, seq, wid), lambda b, h, i: (b, 0, H // hp + h)),
            pl.BlockSpec((1, tq, wid), lambda b, h, i: (b, i, h)),
            pl.BlockSpec(tri.shape, lambda b, h, i: (0, 0)),
        ],
        out_specs=pl.BlockSpec((1, tq, wid), lambda b, h, i: (b, i, h)),
        out_shape=jax.ShapeDtypeStruct((batch, seq, H * HEAD_DIM), BF16),
        scratch_shapes=[pltpu.VMEM((hp, HEAD_DIM, seq), BF16), pltpu.VMEM((hp, tq, tq), F32),
                        pltpu.VMEM((hp, tq, tq), F32), pltpu.VMEM((hp, tq, tq), F32),
                        pltpu.VMEM((hp, HEAD_DIM, tq), F32)],
        compiler_params=_cparams(("parallel", "parallel", "arbitrary")),
        name="sb_attention",
    )(q, kv, kv, z, tri)


def _static_tables(seq):
    QB = LANES
    nch = seq // CMP_STRIDE
    n_slc = seq // SLC_BLOCK
    assert nch == LANES and n_slc <= LANES
    q = np.arange(QB)[None, :]
    rows = np.arange(seq)[:, None]
    idx_c = _t5_bucket_np((rows // QB) * QB + q - ((rows % QB) * CMP_STRIDE + CMP_BLOCK - 1))
    k = np.arange(QB)[:, None]
    far = WINDOW // QB
    tiles = []
    for delta, keep in ((0, k <= q), (1, None), (2, None), (far, k > q)):
        idx = _t5_bucket_np(delta * QB + q - k)
        tiles.append(idx if keep is None else np.where(keep, idx, MASKED_BUCKET))
    idx_t = np.concatenate(tiles, axis=0).astype(np.int32)
    cmp_start = np.arange(LANES) * CMP_STRIDE
    slc_start = np.arange(LANES) * SLC_BLOCK
    ovl = ((cmp_start[None, :] < slc_start[:, None] + SLC_BLOCK)
           & (cmp_start[None, :] + CMP_BLOCK - 1 >= slc_start[:, None]))
    ovl = ovl & (np.arange(LANES)[None, :] < nch - 1) & (np.arange(LANES)[:, None] < n_slc)
    return idx_c, idx_t, ovl.astype(np.float32)


def _nsa_layer(xf, tabs, norm, w_in, cmp_pos, kw1, kw2, vw1, vw2, w_out, *, batch, seq):
    bias_c, bias_t, ovl = tabs
    d = xf.shape[1]
    H, G, Dh = N_HEADS, N_KV_GROUPS, HEAD_DIM
    R = H // G
    HD, GD = H * Dh, G * Dh
    n_qkv = HD + 6 * GD
    hn = _rmsnorm(xf, norm, BF16)
    qscale = jnp.concatenate([jnp.full((1, HD), LOG2E / math.sqrt(Dh), F32), jnp.ones((1, 6 * GD), F32)], axis=1)
    w_gate = w_in[:, n_qkv:n_qkv + 3 * H].reshape(d, 3, G, R).transpose(0, 2, 1, 3).reshape(d, G, 3 * R)
    w_gate = jnp.pad(w_gate, ((0, 0), (0, 0), (0, LANES - 3 * R))).reshape(d, G * LANES).astype(BF16)
    w_z = w_in[:, n_qkv + 3 * H:].astype(BF16)
    qkv = _matmul(hn, w_in, BF16, n=n_qkv, colscale=qscale)
    gl = _matmul(hn, w_gate, F32)
    z = _matmul(hn, w_z, F32)

    nch = seq // CMP_STRIDE

    def chunks(raw):
        t = raw.astype(F32).reshape(batch, nch, CMP_STRIDE, G, Dh).transpose(0, 3, 1, 2, 4)
        return t.reshape(batch, G, nch, CMP_STRIDE * Dh)

    kc, vct = _compress(chunks(qkv[:, HD:HD + GD]), chunks(qkv[:, HD + GD:HD + 2 * GD]),
                        cmp_pos.reshape(1, CMP_BLOCK * Dh),
                        kw1.astype(BF16), kw2.astype(BF16), vw1.astype(BF16), vw2.astype(BF16))
    mixed = _nsa_attention(qkv.reshape(batch, seq, n_qkv), kc, vct, bias_c, bias_t,
                           gl.reshape(batch, seq, G * LANES), z.reshape(batch, seq, 3 * HD),
                           ovl, batch=batch, seq=seq)
    return _matmul(mixed.reshape(batch * seq, HD), w_out, F32, res=xf)


def _sb_layer(xf, kv, tri, norm, w_in, w_out, *, batch, seq):
    H, Dh = N_HEADS, HEAD_DIM
    HD = H * Dh
    hn = _rmsnorm(xf, norm, BF16)
    q = _matmul(hn, w_in, BF16, n=HD, colscale=jnp.full((1, HD), LOG2E / math.sqrt(Dh), F32))
    z = _matmul(hn, w_in, F32, col0=HD, n=HD)
    o = _sb_attention(q.reshape(batch, seq, HD), kv, z.reshape(batch, seq, HD), tri, batch=batch, seq=seq)
    return _matmul(o.reshape(batch * seq, HD), w_out, F32, res=xf)


def kernel(x, rel_bias, a0_norm, a0_w_in, a0_cmp_pos, a0_cmp_k_w1, a0_cmp_k_w2, a0_cmp_v_w1, a0_cmp_v_w2, a0_w_out, a1_norm, a1_w_in, a1_cmp_pos, a1_cmp_k_w1, a1_cmp_k_w2, a1_cmp_v_w1, a1_cmp_v_w2, a1_w_out, kv_norm, w_kv, b2_norm, b2_w_in, b2_w_out, b3_norm, b3_w_in, b3_w_out, final_norm):
    batch, seq, d = x.shape
    HD = N_HEADS * HEAD_DIM
    xf = x.reshape(batch * seq, d)

    idx_c, idx_t, ovl = _static_tables(seq)
    rel_bias_ext = jnp.concatenate([rel_bias.T * LOG2E, jnp.full((N_HEADS, 1), NEG_BIG, F32)], axis=1)
    tabs = (_bias_table(rel_bias_ext, idx_c), _bias_table(rel_bias_ext, idx_t), jnp.asarray(ovl, BF16))

    xf = _nsa_layer(xf, tabs, a0_norm, a0_w_in, a0_cmp_pos, a0_cmp_k_w1, a0_cmp_k_w2, a0_cmp_v_w1, a0_cmp_v_w2,
                    a0_w_out, batch=batch, seq=seq)
    xf = _nsa_layer(xf, tabs, a1_norm, a1_w_in, a1_cmp_pos, a1_cmp_k_w1, a1_cmp_k_w2, a1_cmp_v_w1, a1_cmp_v_w2,
                    a1_w_out, batch=batch, seq=seq)

    kv = _matmul(_rmsnorm(xf, kv_norm, BF16), w_kv, BF16).reshape(batch, seq, 2 * HD)
    m = np.arange(LANES)[None, :] > np.arange(LANES)[:, None]
    tri = jnp.asarray(np.concatenate([m, m], axis=1), BF16)
    xf = _sb_layer(xf, kv, tri, b2_norm, b2_w_in, b2_w_out, batch=batch, seq=seq)
    xf = _sb_layer(xf, kv, tri, b3_norm, b3_w_in, b3_w_out, batch=batch, seq=seq)

    return _rmsnorm(xf, final_norm, F32).reshape(batch, seq, d)
```

```python
import functools
import math

import numpy as np
import jax
import jax.numpy as jnp
from jax import lax
from jax.experimental import pallas as pl
from jax.experimental.pallas import tpu as pltpu

N_HEADS = 32
HEAD_DIM = 128
N_KV_GROUPS = 4
CMP_BLOCK = 32
CMP_STRIDE = 16
SLC_BLOCK = 64
N_SELECT = 16
WINDOW = 512
N_BUCKETS = 32
MAX_DISTANCE = 128
RMS_EPS = 1e-6
FORCE_SCORE = 1e6

LANES = 128
NEG_BIG = -1e30
VMEM_LIMIT = 56 * 1024 * 1024
LOG2E = math.log2(math.e)

F32 = jnp.float32
BF16 = jnp.bfloat16


def _cparams(sem):
    return pltpu.CompilerParams(dimension_semantics=sem, vmem_limit_bytes=VMEM_LIMIT)


def _dot(a, b):
    return jnp.dot(a, b, preferred_element_type=F32)


def _sigmoid(x):
    return 0.5 * jnp.tanh(0.5 * x) + 0.5


def _rmsnorm_kernel(x_ref, g_ref, o_ref):
    x = x_ref[...]
    ms = jnp.mean(x * x, axis=-1, keepdims=True)
    o_ref[...] = (x * lax.rsqrt(ms + RMS_EPS) * g_ref[...]).astype(o_ref.dtype)


def _rmsnorm(x, g, out_dtype, tm=512):
    m, d = x.shape
    tm = min(tm, m)
    return pl.pallas_call(
        _rmsnorm_kernel,
        grid=(m // tm,),
        in_specs=[pl.BlockSpec((tm, d), lambda i: (i, 0)),
                  pl.BlockSpec((1, d), lambda i: (0, 0))],
        out_specs=pl.BlockSpec((tm, d), lambda i: (i, 0)),
        out_shape=jax.ShapeDtypeStruct((m, d), out_dtype),
        compiler_params=_cparams(("parallel",)),
        name="rmsnorm",
    )(x, g.reshape(1, d))


def _matmul_kernel(*refs, has_res, has_scale, cast_w, shift):
    it = iter(refs)
    x_ref, w_ref = next(it), next(it)
    wn_ref = next(it) if shift else None
    s_ref = next(it) if has_scale else None
    r_ref = next(it) if has_res else None
    o_ref = next(it)
    if cast_w:
        wb_ref = next(it)

        @pl.when(pl.program_id(1) == 0)
        def _():
            w = w_ref[...]
            if shift:
                w = jnp.concatenate([w[:, shift:], wn_ref[:, :shift]], axis=1)
            if has_scale:
                w = w * s_ref[...]
            wb_ref[...] = w.astype(BF16)
    else:
        wb_ref = w_ref
    acc = _dot(x_ref[...], wb_ref[...])
    if has_res:
        acc = r_ref[...] + acc
    o_ref[...] = acc.astype(o_ref.dtype)


def _matmul(x, w, out_dtype, *, col0=0, n=None, res=None, colscale=None, tm=1024, tn=512):
    m, k = x.shape
    n = w.shape[1] - col0 if n is None else n
    tm, tn = min(tm, m), min(tn, n)
    cast_w = w.dtype != BF16
    shift = col0 % LANES
    base = col0 - shift
    assert m % tm == 0 and n % tn == 0 and base % tn == 0 and tn % LANES == 0
    assert cast_w or (colscale is None and shift == 0)
    joff = base // tn
    w_mode = dict(pipeline_mode=pl.Buffered(1)) if cast_w else {}
    in_specs = [pl.BlockSpec((tm, k), lambda j, i: (i, 0)),
                pl.BlockSpec((k, tn), lambda j, i: (0, j + joff), **w_mode)]
    args = [x, w]
    if shift:
        per = tn // LANES
        in_specs.append(pl.BlockSpec((k, LANES), lambda j, i: (0, (j + joff + 1) * per), **w_mode))
        args.append(w)
    if colscale is not None:
        in_specs.append(pl.BlockSpec((1, tn), lambda j, i: (0, j)))
        args.append(colscale)
    if res is not None:
        in_specs.append(pl.BlockSpec((tm, tn), lambda j, i: (i, j)))
        args.append(res)
    return pl.pallas_call(
        functools.partial(_matmul_kernel, has_res=res is not None, has_scale=colscale is not None,
                          cast_w=cast_w, shift=shift),
        grid=(n // tn, m // tm),
        in_specs=in_specs,
        out_specs=pl.BlockSpec((tm, tn), lambda j, i: (i, j)),
        out_shape=jax.ShapeDtypeStruct((m, n), out_dtype),
        scratch_shapes=[pltpu.VMEM((k, tn), BF16)] if cast_w else [],
        compiler_params=_cparams(("parallel", "arbitrary")),
        name="matmul_res" if res is not None else "matmul",
    )(*args)


MASKED_BUCKET = N_BUCKETS


def _t5_bucket_np(dist):
    max_exact = N_BUCKETS // 2
    d = np.maximum(dist, 0)
    log_ratio = np.log(np.maximum(d, max_exact).astype(np.float32) / np.float32(max_exact))
    large = max_exact + (log_ratio / np.float32(math.log(MAX_DISTANCE / max_exact))
                         * np.float32(N_BUCKETS - max_exact)).astype(np.int32)
    return np.where(d < max_exact, d, np.minimum(large, N_BUCKETS - 1)).astype(np.int32)


def _bias_kernel(rb_ref, idx_ref, o_ref):
    h = pl.program_id(0)
    idx = idx_ref[...]
    acc = jnp.zeros(idx.shape, F32)
    for b in range(N_BUCKETS + 1):
        acc = jnp.where(idx == b, rb_ref[h, b], acc)
    o_ref[...] = acc


def _bias_table(rel_bias_ext, bucket_idx):
    h = rel_bias_ext.shape[0]
    rows = bucket_idx.shape[0]
    return pl.pallas_call(
        _bias_kernel,
        grid=(h,),
        in_specs=[pl.BlockSpec(memory_space=pltpu.SMEM),
                  pl.BlockSpec((rows, LANES), lambda i: (0, 0))],
        out_specs=pl.BlockSpec((rows, LANES), lambda i: (0, i)),
        out_shape=jax.ShapeDtypeStruct((rows, h * LANES), F32),
        compiler_params=_cparams(("arbitrary",)),
        name="bias_table",
    )(rel_bias_ext, jnp.asarray(bucket_idx))


def _compress_kernel(kv_ref, pos_ref, kw1_ref, kw2_ref, vw1_ref, vw2_ref, kc_ref, vct_ref, *, groups):
    half = pos_ref.shape[1] // 2
    nch = kv_ref.shape[1]
    gd = groups * HEAD_DIM
    g = pl.program_id(1)
    row = lax.broadcasted_iota(jnp.int32, (nch, HEAD_DIM), 0)

    def one(col, w1_ref, w2_ref):
        x = jnp.concatenate(
            [kv_ref[0, :, pl.ds(pl.multiple_of(o * 2 * gd + col + g * HEAD_DIM, HEAD_DIM), HEAD_DIM)]
             for o in range(CMP_STRIDE)], axis=1).astype(F32)
        lo = (x + pos_ref[:, :half]).astype(BF16)
        hi = pltpu.roll(x + pos_ref[:, half:], nch - 1, 0).astype(BF16)
        h = _dot(lo, w1_ref[:half, :]) + _dot(hi, w1_ref[half:, :])
        h = h * _sigmoid(h)
        o = _dot(h.astype(BF16), w2_ref[...])
        return jnp.where(row < nch - 1, o, 0.0)

    kc_ref[0, 0] = one(0, kw1_ref, kw2_ref).astype(kc_ref.dtype)
    vct_ref[0, 0] = one(gd, vw1_ref, vw2_ref).T.astype(vct_ref.dtype)


def _compress(kv, pos, kw1, kw2, vw1, vw2, *, groups):
    b, nch, wid = kv.shape
    oblk = pl.BlockSpec((1, 1, nch, HEAD_DIM), lambda i, j: (i, j, 0, 0))

    def const(a):
        return pl.BlockSpec(a.shape, lambda i, j: (0,) * a.ndim)

    out = jax.ShapeDtypeStruct((b, groups, nch, HEAD_DIM), BF16)
    return pl.pallas_call(
        functools.partial(_compress_kernel, groups=groups),
        grid=(b, groups),
        in_specs=[pl.BlockSpec((1, nch, wid), lambda i, j: (i, 0, 0)),
                  const(pos), const(kw1), const(kw2), const(vw1), const(vw2)],
        out_specs=[oblk, oblk],
        out_shape=[out, out],
        compiler_params=_cparams(("parallel", "arbitrary")),
        name="compress",
    )(kv, pos, kw1, kw2, vw1, vw2)


def _split3(x):
    a = x.astype(BF16)
    r = x - a.astype(F32)
    b = r.astype(BF16)
    c = (r - b.astype(F32)).astype(BF16)
    return a, b, c


def _nsa_kernel(q_ref, kc_ref, vct_ref, ks_ref, vs_ref, kw_ref, vw_ref, bc_ref, bt_ref, gl_ref,
                zc_ref, zs_ref, zw_ref, ovl_ref, o_ref, sel_scr, acc_scr, s_scr, p_scr, vst_scr, vwt_scr,
                *, group, n_slc, n_sel):
    R = group
    QB = LANES
    W = R * QB
    i = pl.program_id(2)
    q0 = i * QB

    @pl.when(i == 0)
    def _():
        for c in range(vs_ref.shape[1] // QB):
            cols = slice(c * QB, (c + 1) * QB)
            vst_scr[:, cols] = vs_ref[0, cols, :].T
            vwt_scr[:, cols] = vw_ref[0, cols, :].T

    q = q_ref[0]
    qt = jnp.concatenate([q[:, r * HEAD_DIM:(r + 1) * HEAD_DIM].T for r in range(R)], axis=1)

    qq = lax.broadcasted_iota(jnp.int32, (QB, W), 1) & (QB - 1)
    row = lax.broadcasted_iota(jnp.int32, (QB, W), 0)
    mask_c = (q0 + qq) - (row * CMP_STRIDE + (CMP_BLOCK - 1)) >= 0
    s = jnp.where(mask_c, _dot(kc_ref[0, 0], qt) + bc_ref[...], NEG_BIG)
    m = jnp.max(s, axis=0, keepdims=True)
    e = jnp.where(mask_c, jnp.exp2(s - m), 0.0)
    p_c = e * (1.0 / jnp.maximum(jnp.sum(e, axis=0, keepdims=True), 1e-30))
    o_c = _dot(vct_ref[0, 0], p_c.astype(BF16))

    psum = p_c[:, 0:QB]
    for r in range(1, R):
        psum = psum + p_c[:, r * QB:(r + 1) * QB]
    ovl = ovl_ref[...]
    imp = sum(_dot(ovl, part) for part in _split3(psum))[:n_slc]
    j = lax.broadcasted_iota(jnp.int32, (n_slc, QB), 0)
    t = q0 + lax.broadcasted_iota(jnp.int32, (n_slc, QB), 1)
    cur = t // SLC_BLOCK
    forced = (j == 0) | (j == cur) | (j == cur - 1)
    score = jnp.where(forced, FORCE_SCORE, jnp.where(j * SLC_BLOCK <= t, imp, -1.0))
    rank = jnp.zeros((n_slc, QB), F32)
    for jp in range(n_slc):
        other = score[jp:jp + 1, :]
        beats = (other > score) | ((other == score) & (j > jp))
        rank = rank + jnp.where(beats, 1.0, 0.0)
    sel_scr[...] = jnp.where(rank < n_sel, 0.0, NEG_BIG)

    def softmax_sweep(k_ref, vt_ref, lo, tile_of, block_mask):
        def raw_scores(kt):
            return _dot(k_ref[0, pl.ds(pl.multiple_of(kt * QB, QB), QB), :], qt)

        def bias(tile):
            return bt_ref[pl.ds(pl.multiple_of(tile * QB, QB), QB), :]

        def accumulate(kt, alpha):
            acc_scr[...] = alpha * acc_scr[...] + _dot(vt_ref[:, pl.ds(pl.multiple_of(kt * QB, QB), QB)], p_scr[...])

        s = raw_scores(i) + bias(0)
        m = jnp.max(s, axis=0, keepdims=True)
        p = jnp.exp2(s - m)
        l = jnp.sum(p, axis=0, keepdims=True)
        acc_scr[...] = jnp.zeros_like(acc_scr)
        p_scr[...] = p.astype(BF16)
        s_scr[...] = raw_scores(lo)

        def body(kt, carry):
            m, l, alpha_prev = carry
            accumulate(jnp.where(kt == lo, i, kt - 1), alpha_prev)
            s = s_scr[...] + bias(tile_of(i - kt))
            if block_mask:
                half = SLC_BLOCK
                blk = kt * (QB // SLC_BLOCK)
                add = jnp.concatenate(
                    [jnp.broadcast_to(sel_scr[pl.ds(blk + c, 1), :], (half, QB)) for c in range(QB // half)], axis=0)
                s = s + jnp.concatenate([add] * R, axis=1)
            m_new = jnp.maximum(m, jnp.max(s, axis=0, keepdims=True))
            alpha = jnp.exp2(m - m_new)
            p = jnp.exp2(s - m_new)
            l = alpha * l + jnp.sum(p, axis=0, keepdims=True)
            p_scr[...] = p.astype(BF16)
            s_scr[...] = raw_scores(jnp.minimum(kt + 1, i))
            return m_new, l, alpha

        _, l, alpha_last = lax.fori_loop(lo, i, body, (m, l, jnp.ones((1, W), F32)))
        accumulate(jnp.where(i > lo, i - 1, i), alpha_last)
        return acc_scr[...] * (1.0 / l)

    o_s = softmax_sweep(ks_ref, vst_scr, 0, lambda d: jnp.minimum(d, 2), True)
    far = WINDOW // QB
    o_w = softmax_sweep(kw_ref, vwt_scr, jnp.maximum(i - far, 0),
                        lambda d: jnp.where(d == far, 3, jnp.minimum(d, 2)), False)

    gates = _sigmoid(gl_ref[0]).T
    for r in range(R):
        cols = slice(r * QB, (r + 1) * QB)
        mixed = jnp.zeros((QB, HEAD_DIM), F32)
        for c, (o, z_ref) in enumerate(((o_c, zc_ref), (o_s, zs_ref), (o_w, zw_ref))):
            z = z_ref[0, :, cols]
            gated = o[:, cols] * gates[c * R + r:c * R + r + 1, :]
            mixed = mixed + gated.T * (z * _sigmoid(z))
        o_ref[0, :, cols] = mixed.astype(o_ref.dtype)


def _nsa_attention(qkv, kc, vct, bias_c, bias_t, gl, z, ovl, *, batch, seq):
    G, R = N_KV_GROUPS, N_HEADS // N_KV_GROUPS
    QB = LANES
    W = R * QB
    nqb = seq // QB
    n_slc = seq // SLC_BLOCK
    n_sel = min(N_SELECT, n_slc)

    def kv_spec(which):
        return pl.BlockSpec((1, seq, HEAD_DIM), lambda g, b, i: (b, 0, N_HEADS + which * G + g))

    def z_spec(c):
        return pl.BlockSpec((1, QB, R * HEAD_DIM), lambda g, b, i: (b, i, c * G + g))

    cmp_spec = pl.BlockSpec((1, 1, LANES, HEAD_DIM), lambda g, b, i: (b, g, 0, 0))
    in_specs = [
        pl.BlockSpec((1, QB, R * HEAD_DIM), lambda g, b, i: (b, i, g)),
        cmp_spec, cmp_spec,
        kv_spec(2), kv_spec(3), kv_spec(4), kv_spec(5),
        pl.BlockSpec((QB, W), lambda g, b, i: (i, g)),
        pl.BlockSpec((4 * QB, W), lambda g, b, i: (0, g)),
        pl.BlockSpec((1, QB, LANES), lambda g, b, i: (b, i, g)),
        z_spec(0), z_spec(1), z_spec(2),
        pl.BlockSpec(ovl.shape, lambda g, b, i: (0, 0)),
    ]
    return pl.pallas_call(
        functools.partial(_nsa_kernel, group=R, n_slc=n_slc, n_sel=n_sel),
        grid=(G, batch, nqb),
        in_specs=in_specs,
        out_specs=pl.BlockSpec((1, QB, R * HEAD_DIM), lambda g, b, i: (b, i, g)),
        out_shape=jax.ShapeDtypeStruct((batch, seq, N_HEADS * HEAD_DIM), BF16),
        scratch_shapes=[pltpu.VMEM((n_slc, QB), F32), pltpu.VMEM((HEAD_DIM, W), F32),
                        pltpu.VMEM((QB, W), F32), pltpu.VMEM((QB, W), BF16),
                        pltpu.VMEM((HEAD_DIM, seq), BF16), pltpu.VMEM((HEAD_DIM, seq), BF16)],
        compiler_params=_cparams(("parallel", "parallel", "arbitrary")),
        name="nsa_attention",
    )(qkv, kc, vct, qkv, qkv, qkv, qkv, bias_c, bias_t, gl, z, z, z, ovl)


def _sb_kernel(q_ref, k_ref, v_ref, z_ref, tri_ref, o_ref, vt_scr, l_scr, d_scr, w_scr, acc_scr, *, tq, hp):
    i = pl.program_id(2)
    tk = tq
    tri = tri_ref[...]
    n_sub = tk // LANES
    heads = range(hp)

    @pl.when(i == 0)
    def _():
        for g in heads:
            for c in range(v_ref.shape[1] // LANES):
                cols = slice(c * LANES, (c + 1) * LANES)
                vt_scr[g, :, cols] = v_ref[0, cols, g * HEAD_DIM:(g + 1) * HEAD_DIM].T

    qts = [q_ref[0, :, g * HEAD_DIM:(g + 1) * HEAD_DIM].T for g in heads]

    def key0(t):
        return pl.multiple_of(jnp.maximum(i - t, 0) * tk, tk)

    def logits(t):
        k0 = key0(t)
        for g in heads:
            l_scr[g] = _dot(k_ref[0, pl.ds(k0, tk), g * HEAD_DIM:(g + 1) * HEAD_DIM], qts[g])

    def softplus_sums(diagonal):
        totals = []
        for g in heads:
            logit = l_scr[g]
            sp = jnp.maximum(logit, 0.0) + jnp.log(1.0 + jnp.exp2(-jnp.abs(logit))) * LOG2E
            d = logit - sp
            if diagonal:
                ok = (lax.broadcasted_iota(jnp.int32, (tk, tq), 0) < lax.broadcasted_iota(jnp.int32, (tk, tq), 1))
                sp = jnp.where(ok, sp, 0.0)
                d = jnp.where(ok, d, NEG_BIG)
            d_scr[g] = d
            hi = sp.astype(BF16)
            lo = (sp - hi.astype(F32)).astype(BF16)
            tot = []
            for c in range(n_sub):
                rows = slice(c * LANES, (c + 1) * LANES)
                within = _dot(tri, jnp.concatenate([hi[rows], lo[rows]], axis=0))
                w_scr[g, rows, :] = within
                tot.append(within[0:1] + sp[c * LANES:c * LANES + 1])
            totals.append(tuple(tot))
        return tuple(totals)

    def weigh_values(t, laters, totals):
        k0 = key0(t)
        out = []
        for g in heads:
            later = laters[g]
            parts = [None] * n_sub
            for c in reversed(range(n_sub)):
                rows = slice(c * LANES, (c + 1) * LANES)
                parts[c] = jnp.exp2(d_scr[g, rows, :] - w_scr[g, rows, :] - later)
                later = later + totals[g][c]
            a = jnp.concatenate(parts, axis=0).astype(BF16)
            acc_scr[g] += _dot(vt_scr[g, :, pl.ds(k0, tk)], a)
            out.append(later)
        return tuple(out)

    acc_scr[...] = jnp.zeros_like(acc_scr)
    logits(0)
    totals = softplus_sums(True)
    logits(1)

    def body(t, carry):
        laters, totals = carry
        laters = weigh_values(t - 1, laters, totals)
        totals = softplus_sums(False)
        logits(t + 1)
        return laters, totals

    laters, totals = lax.fori_loop(1, i + 1, body, ((jnp.zeros((1, tq), F32),) * hp, totals))
    weigh_values(i, laters, totals)
    for g in heads:
        cols = slice(g * HEAD_DIM, (g + 1) * HEAD_DIM)
        z = z_ref[0, :, cols]
        o_ref[0, :, cols] = (acc_scr[g].T * (z * _sigmoid(z))).astype(o_ref.dtype)


def _sb_attention(q, kv, z, tri, *, batch, seq, tq=512, hp=4):
    H = N_HEADS
    tq = min(tq, seq)
    wid = hp * HEAD_DIM
    return pl.pallas_call(
        functools.partial(_sb_kernel, tq=tq, hp=hp),
        grid=(batch, H // hp, seq // tq),
        in_specs=[
            pl.BlockSpec((1, tq, wid), lambda b, h, i: (b, i, h)),
            pl.BlockSpec((1, seq, wid), lambda b, h, i: (b, 0, h)),
            pl.BlockSpec((1, seq, wid), lambda b, h, i: (b, 0, H // hp + h)),
            pl.BlockSpec((1, tq, wid), lambda b, h, i: (b, i, h)),
            pl.BlockSpec(tri.shape, lambda b, h, i: (0, 0)),
        ],
        out_specs=pl.BlockSpec((1, tq, wid), lambda b, h, i: (b, i, h)),
        out_shape=jax.ShapeDtypeStruct((batch, seq, H * HEAD_DIM), BF16),
        scratch_shapes=[pltpu.VMEM((hp, HEAD_DIM, seq), BF16), pltpu.VMEM((hp, tq, tq), F32),
                        pltpu.VMEM((hp, tq, tq), F32), pltpu.VMEM((hp, tq, tq), F32),
                        pltpu.VMEM((hp, HEAD_DIM, tq), F32)],
        compiler_params=_cparams(("parallel", "parallel", "arbitrary")),
        name="sb_attention",
    )(q, kv, kv, z, tri)


def _static_tables(seq):
    QB = LANES
    nch = seq // CMP_STRIDE
    n_slc = seq // SLC_BLOCK
    assert nch == LANES and n_slc <= LANES
    q = np.arange(QB)[None, :]
    rows = np.arange(seq)[:, None]
    idx_c = _t5_bucket_np((rows // QB) * QB + q - ((rows % QB) * CMP_STRIDE + CMP_BLOCK - 1))
    k = np.arange(QB)[:, None]
    far = WINDOW // QB
    tiles = []
    for delta, keep in ((0, k <= q), (1, None), (2, None), (far, k > q)):
        idx = _t5_bucket_np(delta * QB + q - k)
        tiles.append(idx if keep is None else np.where(keep, idx, MASKED_BUCKET))
    idx_t = np.concatenate(tiles, axis=0).astype(np.int32)
    cmp_start = np.arange(LANES) * CMP_STRIDE
    slc_start = np.arange(LANES) * SLC_BLOCK
    ovl = ((cmp_start[None, :] < slc_start[:, None] + SLC_BLOCK)
           & (cmp_start[None, :] + CMP_BLOCK - 1 >= slc_start[:, None]))
    ovl = ovl & (np.arange(LANES)[None, :] < nch - 1) & (np.arange(LANES)[:, None] < n_slc)
    return idx_c, idx_t, ovl.astype(np.float32)


def _nsa_layer(xf, tabs, norm, w_in, cmp_pos, kw1, kw2, vw1, vw2, w_out, *, batch, seq):
    bias_c, bias_t, ovl = tabs
    d = xf.shape[1]
    H, G, Dh = N_HEADS, N_KV_GROUPS, HEAD_DIM
    R = H // G
    HD, GD = H * Dh, G * Dh
    n_qkv = HD + 6 * GD
    hn = _rmsnorm(xf, norm, BF16)
    qscale = jnp.concatenate([jnp.full((1, HD), LOG2E / math.sqrt(Dh), F32), jnp.ones((1, 6 * GD), F32)], axis=1)
    w_gate = w_in[:, n_qkv:n_qkv + 3 * H].reshape(d, 3, G, R).transpose(0, 2, 1, 3).reshape(d, G, 3 * R)
    w_gate = jnp.pad(w_gate, ((0, 0), (0, 0), (0, LANES - 3 * R))).reshape(d, G * LANES).astype(BF16)
    qkv = _matmul(hn, w_in, BF16, n=n_qkv, colscale=qscale)
    gl = _matmul(hn, w_gate, F32)
    z = _matmul(hn, w_in, F32, col0=n_qkv + 3 * H, n=3 * HD)

    nch = seq // CMP_STRIDE
    cmp_in = qkv[:, HD:HD + 2 * GD].reshape(batch, nch, CMP_STRIDE * 2 * GD)
    kc, vct = _compress(cmp_in, cmp_pos.reshape(1, CMP_BLOCK * Dh),
                        kw1.astype(BF16), kw2.astype(BF16), vw1.astype(BF16), vw2.astype(BF16), groups=G)
    mixed = _nsa_attention(qkv.reshape(batch, seq, n_qkv), kc, vct, bias_c, bias_t,
                           gl.reshape(batch, seq, G * LANES), z.reshape(batch, seq, 3 * HD),
                           ovl, batch=batch, seq=seq)
    return _matmul(mixed.reshape(batch * seq, HD), w_out, F32, res=xf)


def _sb_layer(xf, kv, tri, norm, w_in, w_out, *, batch, seq):
    H, Dh = N_HEADS, HEAD_DIM
    HD = H * Dh
    hn = _rmsnorm(xf, norm, BF16)
    q = _matmul(hn, w_in, BF16, n=HD, colscale=jnp.full((1, HD), LOG2E / math.sqrt(Dh), F32))
    z = _matmul(hn, w_in, F32, col0=HD, n=HD)
    o = _sb_attention(q.reshape(batch, seq, HD), kv, z.reshape(batch, seq, HD), tri, batch=batch, seq=seq)
    return _matmul(o.reshape(batch * seq, HD), w_out, F32, res=xf)


def kernel(x, rel_bias, a0_norm, a0_w_in, a0_cmp_pos, a0_cmp_k_w1, a0_cmp_k_w2, a0_cmp_v_w1, a0_cmp_v_w2, a0_w_out, a1_norm, a1_w_in, a1_cmp_pos, a1_cmp_k_w1, a1_cmp_k_w2, a1_cmp_v_w1, a1_cmp_v_w2, a1_w_out, kv_norm, w_kv, b2_norm, b2_w_in, b2_w_out, b3_norm, b3_w_in, b3_w_out, final_norm):
    batch, seq, d = x.shape
    HD = N_HEADS * HEAD_DIM
    xf = x.reshape(batch * seq, d)

    idx_c, idx_t, ovl = _static_tables(seq)
    rel_bias_ext = jnp.concatenate([rel_bias.T * LOG2E, jnp.full((N_HEADS, 1), NEG_BIG, F32)], axis=1)
    tabs = (_bias_table(rel_bias_ext, idx_c), _bias_table(rel_bias_ext, idx_t), jnp.asarray(ovl, BF16))

    xf = _nsa_layer(xf, tabs, a0_norm, a0_w_in, a0_cmp_pos, a0_cmp_k_w1, a0_cmp_k_w2, a0_cmp_v_w1, a0_cmp_v_w2,
                    a0_w_out, batch=batch, seq=seq)
    xf = _nsa_layer(xf, tabs, a1_norm, a1_w_in, a1_cmp_pos, a1_cmp_k_w1, a1_cmp_k_w2, a1_cmp_v_w1, a1_cmp_v_w2,
                    a1_w_out, batch=batch, seq=seq)

    kv = _matmul(_rmsnorm(xf, kv_norm, BF16), w_kv, BF16).reshape(batch, seq, 2 * HD)
    m = np.arange(LANES)[None, :] > np.arange(LANES)[:, None]
    tri = jnp.asarray(np.concatenate([m, m], axis=1), BF16)
    xf = _sb_layer(xf, kv, tri, b2_norm, b2_w_in, b2_w_out, batch=batch, seq=seq)
    xf = _sb_layer(xf, kv, tri, b3_norm, b3_w_in, b3_w_out, batch=batch, seq=seq)

    return _rmsnorm(xf, final_norm, F32).reshape(batch, seq, d)
```

```python
import functools
import math

import numpy as np
import jax
import jax.numpy as jnp
from jax import lax
from jax.experimental import pallas as pl
from jax.experimental.pallas import tpu as pltpu

N_HEADS = 32
HEAD_DIM = 128
N_KV_GROUPS = 4
CMP_BLOCK = 32
CMP_STRIDE = 16
SLC_BLOCK = 64
N_SELECT = 16
WINDOW = 512
N_BUCKETS = 32
MAX_DISTANCE = 128
RMS_EPS = 1e-6
FORCE_SCORE = 1e6

LANES = 128
NEG_BIG = -1e30
VMEM_LIMIT = 56 * 1024 * 1024
LOG2E = math.log2(math.e)

F32 = jnp.float32
BF16 = jnp.bfloat16


def _cparams(sem):
    return pltpu.CompilerParams(dimension_semantics=sem, vmem_limit_bytes=VMEM_LIMIT)


def _dot(a, b):
    return jnp.dot(a, b, preferred_element_type=F32)


def _sigmoid(x):
    return 0.5 * jnp.tanh(0.5 * x) + 0.5


def _rmsnorm_kernel(x_ref, g_ref, o_ref):
    x = x_ref[...]
    ms = jnp.mean(x * x, axis=-1, keepdims=True)
    o_ref[...] = (x * lax.rsqrt(ms + RMS_EPS) * g_ref[...]).astype(o_ref.dtype)


def _rmsnorm(x, g, out_dtype, tm=512):
    m, d = x.shape
    tm = min(tm, m)
    return pl.pallas_call(
        _rmsnorm_kernel,
        grid=(m // tm,),
        in_specs=[pl.BlockSpec((tm, d), lambda i: (i, 0)),
                  pl.BlockSpec((1, d), lambda i: (0, 0))],
        out_specs=pl.BlockSpec((tm, d), lambda i: (i, 0)),
        out_shape=jax.ShapeDtypeStruct((m, d), out_dtype),
        compiler_params=_cparams(("parallel",)),
        name="rmsnorm",
    )(x, g.reshape(1, d))


def _matmul_kernel(*refs, has_res, has_scale, cast_w, shift):
    it = iter(refs)
    x_ref, w_ref = next(it), next(it)
    wn_ref = next(it) if shift else None
    s_ref = next(it) if has_scale else None
    r_ref = next(it) if has_res else None
    o_ref = next(it)
    if cast_w:
        wb_ref = next(it)

        @pl.when(pl.program_id(1) == 0)
        def _():
            w = w_ref[...]
            if shift:
                w = jnp.concatenate([w[:, shift:], wn_ref[:, :shift]], axis=1)
            if has_scale:
                w = w * s_ref[...]
            wb_ref[...] = w.astype(BF16)
    else:
        wb_ref = w_ref
    acc = _dot(x_ref[...], wb_ref[...])
    if has_res:
        acc = r_ref[...] + acc
    o_ref[...] = acc.astype(o_ref.dtype)


def _matmul(x, w, out_dtype, *, col0=0, n=None, res=None, colscale=None, tm=1024, tn=512):
    m, k = x.shape
    n = w.shape[1] - col0 if n is None else n
    tm, tn = min(tm, m), min(tn, n)
    cast_w = w.dtype != BF16
    shift = col0 % LANES
    base = col0 - shift
    assert m % tm == 0 and n % tn == 0 and base % tn == 0 and tn % LANES == 0
    assert cast_w or (colscale is None and shift == 0)
    joff = base // tn
    w_mode = dict(pipeline_mode=pl.Buffered(1)) if cast_w else {}
    in_specs = [pl.BlockSpec((tm, k), lambda j, i: (i, 0)),
                pl.BlockSpec((k, tn), lambda j, i: (0, j + joff), **w_mode)]
    args = [x, w]
    if shift:
        per = tn // LANES
        in_specs.append(pl.BlockSpec((k, LANES), lambda j, i: (0, (j + joff + 1) * per), **w_mode))
        args.append(w)
    if colscale is not None:
        in_specs.append(pl.BlockSpec((1, tn), lambda j, i: (0, j)))
        args.append(colscale)
    if res is not None:
        in_specs.append(pl.BlockSpec((tm, tn), lambda j, i: (i, j)))
        args.append(res)
    return pl.pallas_call(
        functools.partial(_matmul_kernel, has_res=res is not None, has_scale=colscale is not None,
                          cast_w=cast_w, shift=shift),
        grid=(n // tn, m // tm),
        in_specs=in_specs,
        out_specs=pl.BlockSpec((tm, tn), lambda j, i: (i, j)),
        out_shape=jax.ShapeDtypeStruct((m, n), out_dtype),
        scratch_shapes=[pltpu.VMEM((k, tn), BF16)] if cast_w else [],
        compiler_params=_cparams(("parallel", "arbitrary")),
        name="matmul_res" if res is not None else "matmul",
    )(*args)


MASKED_BUCKET = N_BUCKETS


def _t5_bucket_np(dist):
    max_exact = N_BUCKETS // 2
    d = np.maximum(dist, 0)
    log_ratio = np.log(np.maximum(d, max_exact).astype(np.float32) / np.float32(max_exact))
    large = max_exact + (log_ratio / np.float32(math.log(MAX_DISTANCE / max_exact))
                         * np.float32(N_BUCKETS - max_exact)).astype(np.int32)
    return np.where(d < max_exact, d, np.minimum(large, N_BUCKETS - 1)).astype(np.int32)


def _bias_kernel(rb_ref, idx_ref, o_ref):
    h = pl.program_id(0)
    idx = idx_ref[...]
    acc = jnp.zeros(idx.shape, F32)
    for b in range(N_BUCKETS + 1):
        acc = jnp.where(idx == b, rb_ref[h, b], acc)
    o_ref[...] = acc


def _bias_table(rel_bias_ext, bucket_idx):
    h = rel_bias_ext.shape[0]
    rows = bucket_idx.shape[0]
    return pl.pallas_call(
        _bias_kernel,
        grid=(h,),
        in_specs=[pl.BlockSpec(memory_space=pltpu.SMEM),
                  pl.BlockSpec((rows, LANES), lambda i: (0, 0))],
        out_specs=pl.BlockSpec((rows, LANES), lambda i: (0, i)),
        out_shape=jax.ShapeDtypeStruct((rows, h * LANES), F32),
        compiler_params=_cparams(("arbitrary",)),
        name="bias_table",
    )(rel_bias_ext, jnp.asarray(bucket_idx))


def _compress_kernel(kv_ref, pos_ref, kw1_ref, kw2_ref, vw1_ref, vw2_ref, kc_ref, vct_ref, *, groups):
    half = pos_ref.shape[1] // 2
    nch = kv_ref.shape[1]
    gd = groups * HEAD_DIM
    g = pl.program_id(1)
    row = lax.broadcasted_iota(jnp.int32, (nch, HEAD_DIM), 0)

    def one(col, w1_ref, w2_ref):
        x = jnp.concatenate(
            [kv_ref[0, :, pl.ds(pl.multiple_of(o * 2 * gd + col + g * HEAD_DIM, HEAD_DIM), HEAD_DIM)]
             for o in range(CMP_STRIDE)], axis=1).astype(F32)
        lo = (x + pos_ref[:, :half]).astype(BF16)
        hi = pltpu.roll(x + pos_ref[:, half:], nch - 1, 0).astype(BF16)
        h = _dot(lo, w1_ref[:half, :]) + _dot(hi, w1_ref[half:, :])
        h = h * _sigmoid(h)
        o = _dot(h.astype(BF16), w2_ref[...])
        return jnp.where(row < nch - 1, o, 0.0)

    kc_ref[0, 0] = one(0, kw1_ref, kw2_ref).astype(kc_ref.dtype)
    vct_ref[0, 0] = one(gd, vw1_ref, vw2_ref).T.astype(vct_ref.dtype)


def _compress(kv, pos, kw1, kw2, vw1, vw2, *, groups):
    b, nch, wid = kv.shape
    oblk = pl.BlockSpec((1, 1, nch, HEAD_DIM), lambda i, j: (i, j, 0, 0))

    def const(a):
        return pl.BlockSpec(a.shape, lambda i, j: (0,) * a.ndim)

    out = jax.ShapeDtypeStruct((b, groups, nch, HEAD_DIM), BF16)
    return pl.pallas_call(
        functools.partial(_compress_kernel, groups=groups),
        grid=(b, groups),
        in_specs=[pl.BlockSpec((1, nch, wid), lambda i, j: (i, 0, 0)),
                  const(pos), const(kw1), const(kw2), const(vw1), const(vw2)],
        out_specs=[oblk, oblk],
        out_shape=[out, out],
        compiler_params=_cparams(("parallel", "arbitrary")),
        name="compress",
    )(kv, pos, kw1, kw2, vw1, vw2)


def _split3(x):
    a = x.astype(BF16)
    r = x - a.astype(F32)
    b = r.astype(BF16)
    c = (r - b.astype(F32)).astype(BF16)
    return a, b, c


def _nsa_kernel(q_ref, kc_ref, vct_ref, ks_ref, vs_ref, kw_ref, vw_ref, bc_ref, bt_ref, gl_ref,
                zc_ref, zs_ref, zw_ref, ovl_ref, o_ref, sel_scr, acc_scr, s_scr, p_scr, acc2_scr, s2_scr, p2_scr,
                vst_scr, vwt_scr, gate_scr,
                *, group, n_slc, n_sel):
    R = group
    QB = LANES
    W = R * QB
    i = pl.program_id(2)
    q0 = i * QB

    @pl.when(i == 0)
    def _():
        for c in range(vs_ref.shape[1] // QB):
            cols = slice(c * QB, (c + 1) * QB)
            vst_scr[:, cols] = vs_ref[0, cols, :].T
            vwt_scr[:, cols] = vw_ref[0, cols, :].T

    q = q_ref[0]
    qt = jnp.concatenate([q[:, r * HEAD_DIM:(r + 1) * HEAD_DIM].T for r in range(R)], axis=1)

    qq = lax.broadcasted_iota(jnp.int32, (QB, W), 1) & (QB - 1)
    row = lax.broadcasted_iota(jnp.int32, (QB, W), 0)
    mask_c = (q0 + qq) - (row * CMP_STRIDE + (CMP_BLOCK - 1)) >= 0
    s = jnp.where(mask_c, _dot(kc_ref[0, 0], qt) + bc_ref[...], NEG_BIG)
    m = jnp.max(s, axis=0, keepdims=True)
    e = jnp.where(mask_c, jnp.exp2(s - m), 0.0)
    p_c = e * (1.0 / jnp.maximum(jnp.sum(e, axis=0, keepdims=True), 1e-30))
    o_c = _dot(vct_ref[0, 0], p_c.astype(BF16))

    psum = p_c[:, 0:QB]
    for r in range(1, R):
        psum = psum + p_c[:, r * QB:(r + 1) * QB]
    ovl = ovl_ref[...]
    imp = sum(_dot(ovl, part) for part in _split3(psum))[:n_slc]
    j = lax.broadcasted_iota(jnp.int32, (n_slc, QB), 0)
    t = q0 + lax.broadcasted_iota(jnp.int32, (n_slc, QB), 1)
    cur = t // SLC_BLOCK
    forced = (j == 0) | (j == cur) | (j == cur - 1)
    score = jnp.where(forced, FORCE_SCORE, jnp.where(j * SLC_BLOCK <= t, imp, -1.0))
    rank = jnp.zeros((n_slc, QB), F32)
    for jp in range(n_slc):
        other = score[jp:jp + 1, :]
        beats = (other > score) | ((other == score) & (j > jp))
        rank = rank + jnp.where(beats, 1.0, 0.0)
    sel_scr[...] = jnp.where(rank < n_sel, 0.0, NEG_BIG)

    far = WINDOW // QB
    lo_w = jnp.maximum(i - far, 0)

    def tile_slice(kt):
        return pl.ds(pl.multiple_of(kt * QB, QB), QB)

    def bias(tile):
        return bt_ref[tile_slice(tile), :]

    class Chain:
        def __init__(self, k_ref, vt_ref, s_ref, p_ref, a_ref, block_mask):
            self.k_ref, self.vt_ref, self.s_ref, self.p_ref, self.a_ref = k_ref, vt_ref, s_ref, p_ref, a_ref
            self.block_mask = block_mask

        def raw_scores(self, kt):
            return _dot(self.k_ref[0, tile_slice(kt), :], qt)

        def start(self, first):
            s = self.raw_scores(i) + bias(0)
            m = jnp.max(s, axis=0, keepdims=True)
            p = jnp.exp2(s - m)
            l = jnp.sum(p, axis=0, keepdims=True)
            self.a_ref[...] = jnp.zeros_like(self.a_ref)
            self.p_ref[...] = p.astype(BF16)
            self.s_ref[...] = self.raw_scores(first)
            return m, l, jnp.ones((1, W), F32)

        def accumulate(self, kt, alpha):
            self.a_ref[...] = alpha * self.a_ref[...] + _dot(self.vt_ref[:, tile_slice(kt)], self.p_ref[...])

        def step(self, carry, prev, cur, nxt, tile):
            m, l, alpha_prev = carry
            self.accumulate(prev, alpha_prev)
            s = self.s_ref[...] + bias(tile)
            if self.block_mask:
                per = QB // SLC_BLOCK
                add = jnp.concatenate(
                    [jnp.broadcast_to(sel_scr[pl.ds(cur * per + c, 1), :], (SLC_BLOCK, QB)) for c in range(per)],
                    axis=0)
                s = s + jnp.concatenate([add] * R, axis=1)
            m_new = jnp.maximum(m, jnp.max(s, axis=0, keepdims=True))
            alpha = jnp.exp2(m - m_new)
            p = jnp.exp2(s - m_new)
            l = alpha * l + jnp.sum(p, axis=0, keepdims=True)
            self.p_ref[...] = p.astype(BF16)
            self.s_ref[...] = self.raw_scores(nxt)
            return m_new, l, alpha

        def finish(self, carry, last):
            _, l, alpha = carry
            self.accumulate(last, alpha)
            return self.a_ref[...] * (1.0 / l)

    sel = Chain(ks_ref, vst_scr, s_scr, p_scr, acc_scr, True)
    win = Chain(kw_ref, vwt_scr, s2_scr, p2_scr, acc2_scr, False)
    carry_s = sel.start(lo_w)
    carry_w = win.start(lo_w)

    def both(kt, carry):
        carry_s, carry_w = carry
        prev = jnp.where(kt == lo_w, i, kt - 1)
        d = i - kt
        carry_s = sel.step(carry_s, prev, kt, jnp.where(kt == i - 1, 0, kt + 1), jnp.minimum(d, 2))
        carry_w = win.step(carry_w, prev, kt, jnp.minimum(kt + 1, i), jnp.where(d == far, 3, jnp.minimum(d, 2)))
        return carry_s, carry_w

    carry_s, carry_w = lax.fori_loop(lo_w, i, both, (carry_s, carry_w))
    o_w = win.finish(carry_w, jnp.where(i > 0, i - 1, i))

    def sel_only(kt, carry):
        return sel.step(carry, jnp.where(kt == 0, i - 1, kt - 1), kt, jnp.minimum(kt + 1, i), 2)

    carry_s = lax.fori_loop(0, lo_w, sel_only, carry_s)
    o_s = sel.finish(carry_s, jnp.where(lo_w > 0, lo_w - 1, jnp.where(i > 0, i - 1, i)))

    n_heads = R * pl.num_programs(0)
    head0 = pl.program_id(0) * R
    gate_scr[...] = _sigmoid(gl_ref[0]).T
    for r in range(R):
        cols = slice(r * QB, (r + 1) * QB)
        mixed = jnp.zeros((QB, HEAD_DIM), F32)
        for c, (o, z_ref) in enumerate(((o_c, zc_ref), (o_s, zs_ref), (o_w, zw_ref))):
            z = z_ref[0, :, cols]
            gated = o[:, cols] * gate_scr[pl.ds(c * n_heads + head0 + r, 1), :]
            mixed = mixed + gated.T * (z * _sigmoid(z))
        o_ref[0, :, cols] = mixed.astype(o_ref.dtype)


def _nsa_attention(qkv, kc, vct, bias_c, bias_t, gl, z, ovl, *, batch, seq):
    G, R = N_KV_GROUPS, N_HEADS // N_KV_GROUPS
    QB = LANES
    W = R * QB
    nqb = seq // QB
    n_slc = seq // SLC_BLOCK
    n_sel = min(N_SELECT, n_slc)

    def kv_spec(which):
        return pl.BlockSpec((1, seq, HEAD_DIM), lambda g, b, i: (b, 0, N_HEADS + which * G + g))

    def z_spec(c):
        return pl.BlockSpec((1, QB, R * HEAD_DIM), lambda g, b, i: (b, i, c * G + g))

    cmp_spec = pl.BlockSpec((1, 1, LANES, HEAD_DIM), lambda g, b, i: (b, g, 0, 0))
    in_specs = [
        pl.BlockSpec((1, QB, R * HEAD_DIM), lambda g, b, i: (b, i, g)),
        cmp_spec, cmp_spec,
        kv_spec(2), kv_spec(3), kv_spec(4), kv_spec(5),
        pl.BlockSpec((QB, W), lambda g, b, i: (i, g)),
        pl.BlockSpec((4 * QB, W), lambda g, b, i: (0, g)),
        pl.BlockSpec((1, QB, LANES), lambda g, b, i: (b, i, 0)),
        z_spec(0), z_spec(1), z_spec(2),
        pl.BlockSpec(ovl.shape, lambda g, b, i: (0, 0)),
    ]
    return pl.pallas_call(
        functools.partial(_nsa_kernel, group=R, n_slc=n_slc, n_sel=n_sel),
        grid=(G, batch, nqb),
        in_specs=in_specs,
        out_specs=pl.BlockSpec((1, QB, R * HEAD_DIM), lambda g, b, i: (b, i, g)),
        out_shape=jax.ShapeDtypeStruct((batch, seq, N_HEADS * HEAD_DIM), BF16),
        scratch_shapes=[pltpu.VMEM((n_slc, QB), F32), pltpu.VMEM((HEAD_DIM, W), F32),
                        pltpu.VMEM((QB, W), F32), pltpu.VMEM((QB, W), BF16),
                        pltpu.VMEM((HEAD_DIM, W), F32), pltpu.VMEM((QB, W), F32), pltpu.VMEM((QB, W), BF16),
                        pltpu.VMEM((HEAD_DIM, seq), BF16), pltpu.VMEM((HEAD_DIM, seq), BF16),
                        pltpu.VMEM((LANES, QB), F32)],
        compiler_params=_cparams(("parallel", "parallel", "arbitrary")),
        name="nsa_attention",
    )(qkv, kc, vct, qkv, qkv, qkv, qkv, bias_c, bias_t, gl, z, z, z, ovl)


def _sb_kernel(q_ref, k_ref, v_ref, z_ref, tri_ref, o_ref, vt_scr, l_scr, d_scr, w_scr, acc_scr, *, tq, hp):
    i = pl.program_id(2)
    tk = tq
    tri = tri_ref[...]
    n_sub = tk // LANES
    heads = range(hp)

    @pl.when(i == 0)
    def _():
        for g in heads:
            for c in range(v_ref.shape[1] // LANES):
                cols = slice(c * LANES, (c + 1) * LANES)
                vt_scr[g, :, cols] = v_ref[0, cols, g * HEAD_DIM:(g + 1) * HEAD_DIM].T

    qts = [q_ref[0, :, g * HEAD_DIM:(g + 1) * HEAD_DIM].T for g in heads]

    def key0(t):
        return pl.multiple_of(jnp.maximum(i - t, 0) * tk, tk)

    def logits(t):
        k0 = key0(t)
        for g in heads:
            l_scr[g] = _dot(k_ref[0, pl.ds(k0, tk), g * HEAD_DIM:(g + 1) * HEAD_DIM], qts[g])

    def softplus_sums(diagonal):
        totals = []
        for g in heads:
            logit = l_scr[g]
            sp = jnp.maximum(logit, 0.0) + jnp.log(1.0 + jnp.exp2(-jnp.abs(logit))) * LOG2E
            d = logit - sp
            if diagonal:
                ok = (lax.broadcasted_iota(jnp.int32, (tk, tq), 0) < lax.broadcasted_iota(jnp.int32, (tk, tq), 1))
                sp = jnp.where(ok, sp, 0.0)
                d = jnp.where(ok, d, NEG_BIG)
            d_scr[g] = d
            hi = sp.astype(BF16)
            lo = (sp - hi.astype(F32)).astype(BF16)
            tot = []
            for c in range(n_sub):
                rows = slice(c * LANES, (c + 1) * LANES)
                within = _dot(tri, jnp.concatenate([hi[rows], lo[rows]], axis=0))
                w_scr[g, rows, :] = within
                tot.append(within[0:1] + sp[c * LANES:c * LANES + 1])
            totals.append(tuple(tot))
        return tuple(totals)

    def weigh_values(t, laters, totals):
        k0 = key0(t)
        out = []
        for g in heads:
            later = laters[g]
            parts = [None] * n_sub
            for c in reversed(range(n_sub)):
                rows = slice(c * LANES, (c + 1) * LANES)
                parts[c] = jnp.exp2(d_scr[g, rows, :] - w_scr[g, rows, :] - later)
                later = later + totals[g][c]
            a = jnp.concatenate(parts, axis=0).astype(BF16)
            acc_scr[g] += _dot(vt_scr[g, :, pl.ds(k0, tk)], a)
            out.append(later)
        return tuple(out)

    acc_scr[...] = jnp.zeros_like(acc_scr)
    logits(0)
    totals = softplus_sums(True)
    logits(1)

    def body(t, carry):
        laters, totals = carry
        laters = weigh_values(t - 1, laters, totals)
        totals = softplus_sums(False)
        logits(t + 1)
        return laters, totals

    laters, totals = lax.fori_loop(1, i + 1, body, ((jnp.zeros((1, tq), F32),) * hp, totals))
    weigh_values(i, laters, totals)
    for g in heads:
        cols = slice(g * HEAD_DIM, (g + 1) * HEAD_DIM)
        z = z_ref[0, :, cols]
        o_ref[0, :, cols] = (acc_scr[g].T * (z * _sigmoid(z))).astype(o_ref.dtype)


def _sb_attention(q, kv, z, tri, *, batch, seq, tq=512, hp=4):
    H = N_HEADS
    tq = min(tq, seq)
    wid = hp * HEAD_DIM
    return pl.pallas_call(
        functools.partial(_sb_kernel, tq=tq, hp=hp),
        grid=(batch, H // hp, seq // tq),
        in_specs=[
            pl.BlockSpec((1, tq, wid), lambda b, h, i: (b, i, h)),
            pl.BlockSpec((1, seq, wid), lambda b, h, i: (b, 0, h)),
            pl.BlockSpec((1, seq, wid), lambda b, h, i: (b, 0, H // hp + h)),
            pl.BlockSpec((1, tq, wid), lambda b, h, i: (b, i, h)),
            pl.BlockSpec(tri.shape, lambda b, h, i: (0, 0)),
        ],
        out_specs=pl.BlockSpec((1, tq, wid), lambda b, h, i: (b, i, h)),
        out_shape=jax.ShapeDtypeStruct((batch, seq, H * HEAD_DIM), BF16),
        scratch_shapes=[pltpu.VMEM((hp, HEAD_DIM, seq), BF16), pltpu.VMEM((hp, tq, tq), F32),
                        pltpu.VMEM((hp, tq, tq), F32), pltpu.VMEM((hp, tq, tq), F32),
                        pltpu.VMEM((hp, HEAD_DIM, tq), F32)],
        compiler_params=_cparams(("parallel", "parallel", "arbitrary")),
        name="sb_attention",
    )(q, kv, kv, z, tri)


def _static_tables(seq):
    QB = LANES
    nch = seq // CMP_STRIDE
    n_slc = seq // SLC_BLOCK
    assert nch == LANES and n_slc <= LANES
    q = np.arange(QB)[None, :]
    rows = np.arange(seq)[:, None]
    idx_c = _t5_bucket_np((rows // QB) * QB + q - ((rows % QB) * CMP_STRIDE + CMP_BLOCK - 1))
    k = np.arange(QB)[:, None]
    far = WINDOW // QB
    tiles = []
    for delta, keep in ((0, k <= q), (1, None), (2, None), (far, k > q)):
        idx = _t5_bucket_np(delta * QB + q - k)
        tiles.append(idx if keep is None else np.where(keep, idx, MASKED_BUCKET))
    idx_t = np.concatenate(tiles, axis=0).astype(np.int32)
    cmp_start = np.arange(LANES) * CMP_STRIDE
    slc_start = np.arange(LANES) * SLC_BLOCK
    ovl = ((cmp_start[None, :] < slc_start[:, None] + SLC_BLOCK)
           & (cmp_start[None, :] + CMP_BLOCK - 1 >= slc_start[:, None]))
    ovl = ovl & (np.arange(LANES)[None, :] < nch - 1) & (np.arange(LANES)[:, None] < n_slc)
    return idx_c, idx_t, ovl.astype(np.float32)


def _nsa_layer(xf, tabs, norm, w_in, cmp_pos, kw1, kw2, vw1, vw2, w_out, *, batch, seq):
    bias_c, bias_t, ovl = tabs
    d = xf.shape[1]
    H, G, Dh = N_HEADS, N_KV_GROUPS, HEAD_DIM
    R = H // G
    HD, GD = H * Dh, G * Dh
    n_qkv = HD + 6 * GD
    hn = _rmsnorm(xf, norm, BF16)
    qscale = jnp.concatenate([jnp.full((1, HD), LOG2E / math.sqrt(Dh), F32), jnp.ones((1, 6 * GD), F32)], axis=1)
    qkv = _matmul(hn, w_in, BF16, n=n_qkv, colscale=qscale)
    assert 3 * H <= LANES
    gl = _matmul(hn, w_in, F32, col0=n_qkv, n=LANES)
    z = _matmul(hn, w_in, F32, col0=n_qkv + 3 * H, n=3 * HD)

    nch = seq // CMP_STRIDE
    cmp_in = qkv[:, HD:HD + 2 * GD].reshape(batch, nch, CMP_STRIDE * 2 * GD)
    kc, vct = _compress(cmp_in, cmp_pos.reshape(1, CMP_BLOCK * Dh),
                        kw1.astype(BF16), kw2.astype(BF16), vw1.astype(BF16), vw2.astype(BF16), groups=G)
    mixed = _nsa_attention(qkv.reshape(batch, seq, n_qkv), kc, vct, bias_c, bias_t,
                           gl.reshape(batch, seq, LANES), z.reshape(batch, seq, 3 * HD),
                           ovl, batch=batch, seq=seq)
    return _matmul(mixed.reshape(batch * seq, HD), w_out, F32, res=xf)


def _sb_layer(xf, kv, tri, norm, w_in, w_out, *, batch, seq):
    H, Dh = N_HEADS, HEAD_DIM
    HD = H * Dh
    hn = _rmsnorm(xf, norm, BF16)
    q = _matmul(hn, w_in, BF16, n=HD, colscale=jnp.full((1, HD), LOG2E / math.sqrt(Dh), F32))
    z = _matmul(hn, w_in, F32, col0=HD, n=HD)
    o = _sb_attention(q.reshape(batch, seq, HD), kv, z.reshape(batch, seq, HD), tri, batch=batch, seq=seq)
    return _matmul(o.reshape(batch * seq, HD), w_out, F32, res=xf)


def kernel(x, rel_bias, a0_norm, a0_w_in, a0_cmp_pos, a0_cmp_k_w1, a0_cmp_k_w2, a0_cmp_v_w1, a0_cmp_v_w2, a0_w_out, a1_norm, a1_w_in, a1_cmp_pos, a1_cmp_k_w1, a1_cmp_k_w2, a1_cmp_v_w1, a1_cmp_v_w2, a1_w_out, kv_norm, w_kv, b2_norm, b2_w_in, b2_w_out, b3_norm, b3_w_in, b3_w_out, final_norm):
    batch, seq, d = x.shape
    HD = N_HEADS * HEAD_DIM
    xf = x.reshape(batch * seq, d)

    idx_c, idx_t, ovl = _static_tables(seq)
    rel_bias_ext = jnp.concatenate([rel_bias.T * LOG2E, jnp.full((N_HEADS, 1), NEG_BIG, F32)], axis=1)
    tabs = (_bias_table(rel_bias_ext, idx_c), _bias_table(rel_bias_ext, idx_t), jnp.asarray(ovl, BF16))

    xf = _nsa_layer(xf, tabs, a0_norm, a0_w_in, a0_cmp_pos, a0_cmp_k_w1, a0_cmp_k_w2, a0_cmp_v_w1, a0_cmp_v_w2,
                    a0_w_out, batch=batch, seq=seq)
    xf = _nsa_layer(xf, tabs, a1_norm, a1_w_in, a1_cmp_pos, a1_cmp_k_w1, a1_cmp_k_w2, a1_cmp_v_w1, a1_cmp_v_w2,
                    a1_w_out, batch=batch, seq=seq)

    kv = _matmul(_rmsnorm(xf, kv_norm, BF16), w_kv, BF16).reshape(batch, seq, 2 * HD)
    m = np.arange(LANES)[None, :] > np.arange(LANES)[:, None]
    tri = jnp.asarray(np.concatenate([m, m], axis=1), BF16)
    xf = _sb_layer(xf, kv, tri, b2_norm, b2_w_in, b2_w_out, batch=batch, seq=seq)
    xf = _sb_layer(xf, kv, tri, b3_norm, b3_w_in, b3_w_out, batch=batch, seq=seq)

    return _rmsnorm(xf, final_norm, F32).reshape(batch, seq, d)
```

```python
import functools
import math

import numpy as np
import jax
import jax.numpy as jnp
from jax import lax
from jax.experimental import pallas as pl
from jax.experimental.pallas import tpu as pltpu

N_HEADS = 32
HEAD_DIM = 128
N_KV_GROUPS = 4
CMP_BLOCK = 32
CMP_STRIDE = 16
SLC_BLOCK = 64
N_SELECT = 16
WINDOW = 512
N_BUCKETS = 32
MAX_DISTANCE = 128
RMS_EPS = 1e-6
FORCE_SCORE = 1e6

LANES = 128
NEG_BIG = -1e30
VMEM_LIMIT = 56 * 1024 * 1024
LOG2E = math.log2(math.e)

F32 = jnp.float32
BF16 = jnp.bfloat16


def _cparams(sem):
    return pltpu.CompilerParams(dimension_semantics=sem, vmem_limit_bytes=VMEM_LIMIT)


def _dot(a, b):
    return jnp.dot(a, b, preferred_element_type=F32)


def _sigmoid(x):
    return 0.5 * jnp.tanh(0.5 * x) + 0.5


def _rmsnorm_kernel(x_ref, *refs):
    n_out = len(refs) // 2
    x = x_ref[...]
    ms = jnp.mean(x * x, axis=-1, keepdims=True)
    y = x * lax.rsqrt(ms + RMS_EPS)
    for g_ref, o_ref in zip(refs[:n_out], refs[n_out:]):
        o_ref[...] = (y * g_ref[...]).astype(o_ref.dtype)


def _rmsnorm(x, gains, out_dtype, tm=512):
    m, d = x.shape
    tm = min(tm, m)
    row = pl.BlockSpec((tm, d), lambda i: (i, 0))
    outs = pl.pallas_call(
        _rmsnorm_kernel,
        grid=(m // tm,),
        in_specs=[row] + [pl.BlockSpec((1, d), lambda i: (0, 0))] * len(gains),
        out_specs=[row] * len(gains),
        out_shape=[jax.ShapeDtypeStruct((m, d), out_dtype)] * len(gains),
        compiler_params=_cparams(("parallel",)),
        name="rmsnorm",
    )(x, *[g.reshape(1, d) for g in gains])
    return outs


def _matmul_kernel(*refs, has_res, has_scale, cast_w, shift, w_rows):
    it = iter(refs)
    x_ref, w_ref = next(it), next(it)
    wn_ref = next(it) if shift else None
    s_ref = next(it) if has_scale else None
    r_ref = next(it) if has_res else None
    o_ref = next(it)
    if cast_w:
        wb_ref = next(it)

        @pl.when(pl.program_id(1) == 0)
        def _():
            w = w_ref[...]
            if shift:
                ax = 0 if w_rows else 1
                w = jnp.concatenate([lax.slice_in_dim(w, shift, w.shape[ax], axis=ax),
                                     lax.slice_in_dim(wn_ref[...], 0, shift, axis=ax)], axis=ax)
            if w_rows:
                w = w.T
            if has_scale:
                w = w * s_ref[...]
            wb_ref[...] = w.astype(BF16)
    else:
        wb_ref = w_ref
    acc = _dot(x_ref[...], wb_ref[...])
    if has_res:
        acc = r_ref[...] + acc
    o_ref[...] = acc.astype(o_ref.dtype)


def _matmul(x, w, out_dtype, *, col0=0, n=None, res=None, colscale=None, w_rows=False, tm=1024, tn=512):
    m, k = x.shape
    n_total = w.shape[0] if w_rows else w.shape[1]
    n = n_total - col0 if n is None else n
    tm, tn = min(tm, m), min(tn, n)
    cast_w = w.dtype != BF16
    shift = col0 % LANES
    base = col0 - shift
    assert m % tm == 0 and n % tn == 0 and base % tn == 0 and tn % LANES == 0
    assert cast_w or (colscale is None and shift == 0 and not w_rows)
    joff = base // tn
    per = tn // LANES
    w_mode = dict(pipeline_mode=pl.Buffered(1)) if cast_w else {}
    if w_rows:
        assert shift % 8 == 0
        w_spec = pl.BlockSpec((tn, k), lambda j, i: (j + joff, 0), **w_mode)
        wn_spec = pl.BlockSpec((LANES, k), lambda j, i: ((j + joff + 1) * per, 0), **w_mode)
    else:
        w_spec = pl.BlockSpec((k, tn), lambda j, i: (0, j + joff), **w_mode)
        wn_spec = pl.BlockSpec((k, LANES), lambda j, i: (0, (j + joff + 1) * per), **w_mode)
    in_specs = [pl.BlockSpec((tm, k), lambda j, i: (i, 0)), w_spec]
    args = [x, w]
    if shift:
        in_specs.append(wn_spec)
        args.append(w)
    if colscale is not None:
        in_specs.append(pl.BlockSpec((1, tn), lambda j, i: (0, j)))
        args.append(colscale)
    if res is not None:
        in_specs.append(pl.BlockSpec((tm, tn), lambda j, i: (i, j)))
        args.append(res)
    return pl.pallas_call(
        functools.partial(_matmul_kernel, has_res=res is not None, has_scale=colscale is not None,
                          cast_w=cast_w, shift=shift, w_rows=w_rows),
        grid=(n // tn, m // tm),
        in_specs=in_specs,
        out_specs=pl.BlockSpec((tm, tn), lambda j, i: (i, j)),
        out_shape=jax.ShapeDtypeStruct((m, n), out_dtype),
        scratch_shapes=[pltpu.VMEM((k, tn), BF16)] if cast_w else [],
        compiler_params=_cparams(("parallel", "arbitrary")),
        name="matmul_res" if res is not None else "matmul",
    )(*args)


MASKED_BUCKET = N_BUCKETS


def _t5_bucket_np(dist):
    max_exact = N_BUCKETS // 2
    d = np.maximum(dist, 0)
    log_ratio = np.log(np.maximum(d, max_exact).astype(np.float32) / np.float32(max_exact))
    large = max_exact + (log_ratio / np.float32(math.log(MAX_DISTANCE / max_exact))
                         * np.float32(N_BUCKETS - max_exact)).astype(np.int32)
    return np.where(d < max_exact, d, np.minimum(large, N_BUCKETS - 1)).astype(np.int32)


def _bias_kernel(rb_ref, idx_ref, o_ref):
    h = pl.program_id(0)
    idx = idx_ref[...]
    acc = jnp.zeros(idx.shape, F32)
    for b in range(N_BUCKETS + 1):
        acc = jnp.where(idx == b, rb_ref[h, b], acc)
    o_ref[...] = acc


def _bias_table(rel_bias_ext, bucket_idx):
    h = rel_bias_ext.shape[0]
    rows = bucket_idx.shape[0]
    return pl.pallas_call(
        _bias_kernel,
        grid=(h,),
        in_specs=[pl.BlockSpec(memory_space=pltpu.SMEM),
                  pl.BlockSpec((rows, LANES), lambda i: (0, 0))],
        out_specs=pl.BlockSpec((rows, LANES), lambda i: (0, i)),
        out_shape=jax.ShapeDtypeStruct((rows, h * LANES), F32),
        compiler_params=_cparams(("arbitrary",)),
        name="bias_table",
    )(rel_bias_ext, jnp.asarray(bucket_idx))


def _compress_kernel(kv_ref, pos_ref, kw1_ref, kw2_ref, vw1_ref, vw2_ref, kc_ref, vct_ref, *, groups):
    half = pos_ref.shape[1] // 2
    nch = kv_ref.shape[1]
    gd = groups * HEAD_DIM
    g = pl.program_id(1)
    row = lax.broadcasted_iota(jnp.int32, (nch, HEAD_DIM), 0)

    def one(col, w1_ref, w2_ref):
        x = jnp.concatenate(
            [kv_ref[0, :, pl.ds(pl.multiple_of(o * 2 * gd + col + g * HEAD_DIM, HEAD_DIM), HEAD_DIM)]
             for o in range(CMP_STRIDE)], axis=1).astype(F32)
        lo = (x + pos_ref[:, :half]).astype(BF16)
        hi = pltpu.roll(x + pos_ref[:, half:], nch - 1, 0).astype(BF16)
        h = _dot(lo, w1_ref[:half, :]) + _dot(hi, w1_ref[half:, :])
        h = h * _sigmoid(h)
        o = _dot(h.astype(BF16), w2_ref[...])
        return jnp.where(row < nch - 1, o, 0.0)

    kc_ref[0, 0] = one(0, kw1_ref, kw2_ref).astype(kc_ref.dtype)
    vct_ref[0, 0] = one(gd, vw1_ref, vw2_ref).T.astype(vct_ref.dtype)


def _compress(kv, pos, kw1, kw2, vw1, vw2, *, groups):
    b, nch, wid = kv.shape
    oblk = pl.BlockSpec((1, 1, nch, HEAD_DIM), lambda i, j: (i, j, 0, 0))

    def const(a):
        return pl.BlockSpec(a.shape, lambda i, j: (0,) * a.ndim)

    out = jax.ShapeDtypeStruct((b, groups, nch, HEAD_DIM), BF16)
    return pl.pallas_call(
        functools.partial(_compress_kernel, groups=groups),
        grid=(b, groups),
        in_specs=[pl.BlockSpec((1, nch, wid), lambda i, j: (i, 0, 0)),
                  const(pos), const(kw1), const(kw2), const(vw1), const(vw2)],
        out_specs=[oblk, oblk],
        out_shape=[out, out],
        compiler_params=_cparams(("parallel", "arbitrary")),
        name="compress",
    )(kv, pos, kw1, kw2, vw1, vw2)


def _split3(x):
    a = x.astype(BF16)
    r = x - a.astype(F32)
    b = r.astype(BF16)
    c = (r - b.astype(F32)).astype(BF16)
    return a, b, c


def _nsa_kernel(q_ref, kc_ref, vct_ref, ks_ref, vs_ref, kw_ref, vw_ref, bc_ref, bt_ref, gl_ref,
                zc_ref, zs_ref, zw_ref, ovl_ref, o_ref, sel_scr, acc_scr, s_scr, p_scr, acc2_scr, s2_scr, p2_scr,
                vst_scr, vwt_scr, gate_scr,
                *, group, n_slc, n_sel):
    R = group
    QB = LANES
    W = R * QB
    i = pl.program_id(2)
    q0 = i * QB

    @pl.when(i == 0)
    def _():
        for c in range(vs_ref.shape[1] // QB):
            cols = slice(c * QB, (c + 1) * QB)
            vst_scr[:, cols] = vs_ref[0, cols, :].T
            vwt_scr[:, cols] = vw_ref[0, cols, :].T

    q = q_ref[0]
    qt = jnp.concatenate([q[:, r * HEAD_DIM:(r + 1) * HEAD_DIM].T for r in range(R)], axis=1)

    qq = lax.broadcasted_iota(jnp.int32, (QB, W), 1) & (QB - 1)
    row = lax.broadcasted_iota(jnp.int32, (QB, W), 0)
    mask_c = (q0 + qq) - (row * CMP_STRIDE + (CMP_BLOCK - 1)) >= 0
    s = jnp.where(mask_c, _dot(kc_ref[0, 0], qt) + bc_ref[...], NEG_BIG)
    m = jnp.max(s, axis=0, keepdims=True)
    e = jnp.where(mask_c, jnp.exp2(s - m), 0.0)
    p_c = e * (1.0 / jnp.maximum(jnp.sum(e, axis=0, keepdims=True), 1e-30))
    o_c = _dot(vct_ref[0, 0], p_c.astype(BF16))

    psum = p_c[:, 0:QB]
    for r in range(1, R):
        psum = psum + p_c[:, r * QB:(r + 1) * QB]
    ovl = ovl_ref[...]
    imp = sum(_dot(ovl, part) for part in _split3(psum))[:n_slc]
    j = lax.broadcasted_iota(jnp.int32, (n_slc, QB), 0)
    t = q0 + lax.broadcasted_iota(jnp.int32, (n_slc, QB), 1)
    cur = t // SLC_BLOCK
    forced = (j == 0) | (j == cur) | (j == cur - 1)
    score = jnp.where(forced, FORCE_SCORE, jnp.where(j * SLC_BLOCK <= t, imp, -1.0))
    rank = jnp.zeros((n_slc, QB), F32)
    for jp in range(n_slc):
        other = score[jp:jp + 1, :]
        beats = (other > score) | ((other == score) & (j > jp))
        rank = rank + jnp.where(beats, 1.0, 0.0)
    sel_scr[...] = jnp.where(rank < n_sel, 0.0, NEG_BIG)

    far = WINDOW // QB
    lo_w = jnp.maximum(i - far, 0)

    def tile_slice(kt):
        return pl.ds(pl.multiple_of(kt * QB, QB), QB)

    def bias(tile):
        return bt_ref[tile_slice(tile), :]

    class Chain:
        def __init__(self, k_ref, vt_ref, s_ref, p_ref, a_ref, block_mask):
            self.k_ref, self.vt_ref, self.s_ref, self.p_ref, self.a_ref = k_ref, vt_ref, s_ref, p_ref, a_ref
            self.block_mask = block_mask

        def raw_scores(self, kt):
            return _dot(self.k_ref[0, tile_slice(kt), :], qt)

        def start(self, first):
            s = self.raw_scores(i) + bias(0)
            m = jnp.max(s, axis=0, keepdims=True)
            p = jnp.exp2(s - m)
            l = jnp.sum(p, axis=0, keepdims=True)
            self.a_ref[...] = jnp.zeros_like(self.a_ref)
            self.p_ref[...] = p.astype(BF16)
            self.s_ref[...] = self.raw_scores(first)
            return m, l, jnp.ones((1, W), F32)

        def accumulate(self, kt, alpha):
            self.a_ref[...] = alpha * self.a_ref[...] + _dot(self.vt_ref[:, tile_slice(kt)], self.p_ref[...])

        def step(self, carry, prev, cur, nxt, tile):
            m, l, alpha_prev = carry
            self.accumulate(prev, alpha_prev)
            s = self.s_ref[...] + bias(tile)
            if self.block_mask:
                per = QB // SLC_BLOCK
                add = jnp.concatenate(
                    [jnp.broadcast_to(sel_scr[pl.ds(cur * per + c, 1), :], (SLC_BLOCK, QB)) for c in range(per)],
                    axis=0)
                s = s + jnp.concatenate([add] * R, axis=1)
            m_new = jnp.maximum(m, jnp.max(s, axis=0, keepdims=True))
            alpha = jnp.exp2(m - m_new)
            p = jnp.exp2(s - m_new)
            l = alpha * l + jnp.sum(p, axis=0, keepdims=True)
            self.p_ref[...] = p.astype(BF16)
            self.s_ref[...] = self.raw_scores(nxt)
            return m_new, l, alpha

        def finish(self, carry, last):
            _, l, alpha = carry
            self.accumulate(last, alpha)
            return self.a_ref[...] * (1.0 / l)

    sel = Chain(ks_ref, vst_scr, s_scr, p_scr, acc_scr, True)
    win = Chain(kw_ref, vwt_scr, s2_scr, p2_scr, acc2_scr, False)
    carry_s = sel.start(lo_w)
    carry_w = win.start(lo_w)

    def both(kt, carry):
        carry_s, carry_w = carry
        prev = jnp.where(kt == lo_w, i, kt - 1)
        d = i - kt
        carry_s = sel.step(carry_s, prev, kt, jnp.where(kt == i - 1, 0, kt + 1), jnp.minimum(d, 2))
        carry_w = win.step(carry_w, prev, kt, jnp.minimum(kt + 1, i), jnp.where(d == far, 3, jnp.minimum(d, 2)))
        return carry_s, carry_w

    carry_s, carry_w = lax.fori_loop(lo_w, i, both, (carry_s, carry_w))
    o_w = win.finish(carry_w, jnp.where(i > 0, i - 1, i))

    def sel_only(kt, carry):
        return sel.step(carry, jnp.where(kt == 0, i - 1, kt - 1), kt, jnp.minimum(kt + 1, i), 2)

    carry_s = lax.fori_loop(0, lo_w, sel_only, carry_s)
    o_s = sel.finish(carry_s, jnp.where(lo_w > 0, lo_w - 1, jnp.where(i > 0, i - 1, i)))

    n_heads = R * pl.num_programs(0)
    head0 = pl.program_id(0) * R
    gate_scr[...] = _sigmoid(gl_ref[0]).T
    for r in range(R):
        cols = slice(r * QB, (r + 1) * QB)
        mixed = jnp.zeros((QB, HEAD_DIM), F32)
        for c, (o, z_ref) in enumerate(((o_c, zc_ref), (o_s, zs_ref), (o_w, zw_ref))):
            z = z_ref[0, :, cols]
            gated = o[:, cols] * gate_scr[pl.ds(c * n_heads + head0 + r, 1), :]
            mixed = mixed + gated.T * (z * _sigmoid(z))
        o_ref[0, :, cols] = mixed.astype(o_ref.dtype)


def _nsa_attention(qkv, kc, vct, bias_c, bias_t, gl, z, ovl, *, batch, seq):
    G, R = N_KV_GROUPS, N_HEADS // N_KV_GROUPS
    QB = LANES
    W = R * QB
    nqb = seq // QB
    n_slc = seq // SLC_BLOCK
    n_sel = min(N_SELECT, n_slc)

    def kv_spec(which):
        return pl.BlockSpec((1, seq, HEAD_DIM), lambda g, b, i: (b, 0, N_HEADS + which * G + g))

    def z_spec(c):
        return pl.BlockSpec((1, QB, R * HEAD_DIM), lambda g, b, i: (b, i, c * G + g))

    cmp_spec = pl.BlockSpec((1, 1, LANES, HEAD_DIM), lambda g, b, i: (b, g, 0, 0))
    in_specs = [
        pl.BlockSpec((1, QB, R * HEAD_DIM), lambda g, b, i: (b, i, g)),
        cmp_spec, cmp_spec,
        kv_spec(2), kv_spec(3), kv_spec(4), kv_spec(5),
        pl.BlockSpec((QB, W), lambda g, b, i: (i, g)),
        pl.BlockSpec((4 * QB, W), lambda g, b, i: (0, g)),
        pl.BlockSpec((1, QB, LANES), lambda g, b, i: (b, i, 0)),
        z_spec(0), z_spec(1), z_spec(2),
        pl.BlockSpec(ovl.shape, lambda g, b, i: (0, 0)),
    ]
    return pl.pallas_call(
        functools.partial(_nsa_kernel, group=R, n_slc=n_slc, n_sel=n_sel),
        grid=(G, batch, nqb),
        in_specs=in_specs,
        out_specs=pl.BlockSpec((1, QB, R * HEAD_DIM), lambda g, b, i: (b, i, g)),
        out_shape=jax.ShapeDtypeStruct((batch, seq, N_HEADS * HEAD_DIM), BF16),
        scratch_shapes=[pltpu.VMEM((n_slc, QB), F32), pltpu.VMEM((HEAD_DIM, W), F32),
                        pltpu.VMEM((QB, W), F32), pltpu.VMEM((QB, W), BF16),
                        pltpu.VMEM((HEAD_DIM, W), F32), pltpu.VMEM((QB, W), F32), pltpu.VMEM((QB, W), BF16),
                        pltpu.VMEM((HEAD_DIM, seq), BF16), pltpu.VMEM((HEAD_DIM, seq), BF16),
                        pltpu.VMEM((LANES, QB), F32)],
        compiler_params=_cparams(("parallel", "parallel", "arbitrary")),
        name="nsa_attention",
    )(qkv, kc, vct, qkv, qkv, qkv, qkv, bias_c, bias_t, gl, z, z, z, ovl)


def _sb_kernel(q_ref, k_ref, v_ref, z_ref, tri_ref, o_ref, vt_scr, l_scr, d_scr, w_scr, acc_scr, *, tq, hp):
    i = pl.program_id(2)
    tk = tq
    tri = tri_ref[...]
    n_sub = tk // LANES
    heads = range(hp)

    @pl.when(i == 0)
    def _():
        for g in heads:
            for c in range(v_ref.shape[1] // LANES):
                cols = slice(c * LANES, (c + 1) * LANES)
                vt_scr[g, :, cols] = v_ref[0, cols, g * HEAD_DIM:(g + 1) * HEAD_DIM].T

    qts = [q_ref[0, :, g * HEAD_DIM:(g + 1) * HEAD_DIM].T for g in heads]

    def key0(t):
        return pl.multiple_of(jnp.maximum(i - t, 0) * tk, tk)

    def logits(t):
        k0 = key0(t)
        for g in heads:
            l_scr[g] = _dot(k_ref[0, pl.ds(k0, tk), g * HEAD_DIM:(g + 1) * HEAD_DIM], qts[g])

    def softplus_sums(diagonal):
        totals = []
        for g in heads:
            logit = l_scr[g]
            sp = jnp.maximum(logit, 0.0) + jnp.log(1.0 + jnp.exp2(-jnp.abs(logit))) * LOG2E
            d = logit - sp
            if diagonal:
                ok = (lax.broadcasted_iota(jnp.int32, (tk, tq), 0) < lax.broadcasted_iota(jnp.int32, (tk, tq), 1))
                sp = jnp.where(ok, sp, 0.0)
                d = jnp.where(ok, d, NEG_BIG)
            d_scr[g] = d
            hi = sp.astype(BF16)
            lo = (sp - hi.astype(F32)).astype(BF16)
            tot = []
            for c in range(n_sub):
                rows = slice(c * LANES, (c + 1) * LANES)
                within = _dot(tri, jnp.concatenate([hi[rows], lo[rows]], axis=0))
                w_scr[g, rows, :] = within
                tot.append(within[0:1] + sp[c * LANES:c * LANES + 1])
            totals.append(tuple(tot))
        return tuple(totals)

    def weigh_values(t, laters, totals):
        k0 = key0(t)
        out = []
        for g in heads:
            later = laters[g]
            parts = [None] * n_sub
            for c in reversed(range(n_sub)):
                rows = slice(c * LANES, (c + 1) * LANES)
                parts[c] = jnp.exp2(d_scr[g, rows, :] - w_scr[g, rows, :] - later)
                later = later + totals[g][c]
            a = jnp.concatenate(parts, axis=0).astype(BF16)
            acc_scr[g] += _dot(vt_scr[g, :, pl.ds(k0, tk)], a)
            out.append(later)
        return tuple(out)

    acc_scr[...] = jnp.zeros_like(acc_scr)
    logits(0)
    totals = softplus_sums(True)
    logits(1)

    def body(t, carry):
        laters, totals = carry
        laters = weigh_values(t - 1, laters, totals)
        totals = softplus_sums(False)
        logits(t + 1)
        return laters, totals

    laters, totals = lax.fori_loop(1, i + 1, body, ((jnp.zeros((1, tq), F32),) * hp, totals))
    weigh_values(i, laters, totals)
    for g in heads:
        cols = slice(g * HEAD_DIM, (g + 1) * HEAD_DIM)
        z = z_ref[0, :, cols]
        o_ref[0, :, cols] = (acc_scr[g].T * (z * _sigmoid(z))).astype(o_ref.dtype)


def _sb_attention(q, kv, z, tri, *, batch, seq, tq=512, hp=4):
    H = N_HEADS
    tq = min(tq, seq)
    wid = hp * HEAD_DIM
    return pl.pallas_call(
        functools.partial(_sb_kernel, tq=tq, hp=hp),
        grid=(batch, H // hp, seq // tq),
        in_specs=[
            pl.BlockSpec((1, tq, wid), lambda b, h, i: (b, i, h)),
            pl.BlockSpec((1, seq, wid), lambda b, h, i: (b, 0, h)),
            pl.BlockSpec((1, seq, wid), lambda b, h, i: (b, 0, H // hp + h)),
            pl.BlockSpec((1, tq, wid), lambda b, h, i: (b, i, h)),
            pl.BlockSpec(tri.shape, lambda b, h, i: (0, 0)),
        ],
        out_specs=pl.BlockSpec((1, tq, wid), lambda b, h, i: (b, i, h)),
        out_shape=jax.ShapeDtypeStruct((batch, seq, H * HEAD_DIM), BF16),
        scratch_shapes=[pltpu.VMEM((hp, HEAD_DIM, seq), BF16), pltpu.VMEM((hp, tq, tq), F32),
                        pltpu.VMEM((hp, tq, tq), F32), pltpu.VMEM((hp, tq, tq), F32),
                        pltpu.VMEM((hp, HEAD_DIM, tq), F32)],
        compiler_params=_cparams(("parallel", "parallel", "arbitrary")),
        name="sb_attention",
    )(q, kv, kv, z, tri)


def _static_tables(seq):
    QB = LANES
    nch = seq // CMP_STRIDE
    n_slc = seq // SLC_BLOCK
    assert nch == LANES and n_slc <= LANES
    q = np.arange(QB)[None, :]
    rows = np.arange(seq)[:, None]
    idx_c = _t5_bucket_np((rows // QB) * QB + q - ((rows % QB) * CMP_STRIDE + CMP_BLOCK - 1))
    k = np.arange(QB)[:, None]
    far = WINDOW // QB
    tiles = []
    for delta, keep in ((0, k <= q), (1, None), (2, None), (far, k > q)):
        idx = _t5_bucket_np(delta * QB + q - k)
        tiles.append(idx if keep is None else np.where(keep, idx, MASKED_BUCKET))
    idx_t = np.concatenate(tiles, axis=0).astype(np.int32)
    cmp_start = np.arange(LANES) * CMP_STRIDE
    slc_start = np.arange(LANES) * SLC_BLOCK
    ovl = ((cmp_start[None, :] < slc_start[:, None] + SLC_BLOCK)
           & (cmp_start[None, :] + CMP_BLOCK - 1 >= slc_start[:, None]))
    ovl = ovl & (np.arange(LANES)[None, :] < nch - 1) & (np.arange(LANES)[:, None] < n_slc)
    return idx_c, idx_t, ovl.astype(np.float32)


def _nsa_layer(xf, tabs, norm, w_in, cmp_pos, kw1, kw2, vw1, vw2, w_out, *, batch, seq):
    bias_c, bias_t, ovl = tabs
    H, G, Dh = N_HEADS, N_KV_GROUPS, HEAD_DIM
    HD, GD = H * Dh, G * Dh
    n_qkv = HD + 6 * GD
    hn, = _rmsnorm(xf, [norm], BF16)
    qscale = jnp.concatenate([jnp.full((1, HD), LOG2E / math.sqrt(Dh), F32), jnp.ones((1, 6 * GD), F32)], axis=1)
    w_in_t = w_in.T
    qkv = _matmul(hn, w_in_t, BF16, n=n_qkv, colscale=qscale, w_rows=True)
    assert 3 * H <= LANES
    gl = _matmul(hn, w_in_t, F32, col0=n_qkv, n=LANES, w_rows=True)
    z = _matmul(hn, w_in_t, F32, col0=n_qkv + 3 * H, n=3 * HD, w_rows=True)

    nch = seq // CMP_STRIDE
    cmp_in = qkv[:, HD:HD + 2 * GD].reshape(batch, nch, CMP_STRIDE * 2 * GD)
    kc, vct = _compress(cmp_in, cmp_pos.reshape(1, CMP_BLOCK * Dh),
                        kw1.astype(BF16), kw2.astype(BF16), vw1.astype(BF16), vw2.astype(BF16), groups=G)
    mixed = _nsa_attention(qkv.reshape(batch, seq, n_qkv), kc, vct, bias_c, bias_t,
                           gl.reshape(batch, seq, LANES), z.reshape(batch, seq, 3 * HD),
                           ovl, batch=batch, seq=seq)
    return _matmul(mixed.reshape(batch * seq, HD), w_out, F32, res=xf)


def _sb_layer(xf, hn, kv, tri, w_in, w_out, *, batch, seq):
    H, Dh = N_HEADS, HEAD_DIM
    HD = H * Dh
    q = _matmul(hn, w_in, BF16, n=HD, colscale=jnp.full((1, HD), LOG2E / math.sqrt(Dh), F32))
    z = _matmul(hn, w_in, F32, col0=HD, n=HD)
    o = _sb_attention(q.reshape(batch, seq, HD), kv, z.reshape(batch, seq, HD), tri, batch=batch, seq=seq)
    return _matmul(o.reshape(batch * seq, HD), w_out, F32, res=xf)


def kernel(x, rel_bias, a0_norm, a0_w_in, a0_cmp_pos, a0_cmp_k_w1, a0_cmp_k_w2, a0_cmp_v_w1, a0_cmp_v_w2, a0_w_out, a1_norm, a1_w_in, a1_cmp_pos, a1_cmp_k_w1, a1_cmp_k_w2, a1_cmp_v_w1, a1_cmp_v_w2, a1_w_out, kv_norm, w_kv, b2_norm, b2_w_in, b2_w_out, b3_norm, b3_w_in, b3_w_out, final_norm):
    batch, seq, d = x.shape
    HD = N_HEADS * HEAD_DIM
    xf = x.reshape(batch * seq, d)

    idx_c, idx_t, ovl = _static_tables(seq)
    rel_bias_ext = jnp.concatenate([rel_bias.T * LOG2E, jnp.full((N_HEADS, 1), NEG_BIG, F32)], axis=1)
    tabs = (_bias_table(rel_bias_ext, idx_c), _bias_table(rel_bias_ext, idx_t), jnp.asarray(ovl, BF16))

    xf = _nsa_layer(xf, tabs, a0_norm, a0_w_in, a0_cmp_pos, a0_cmp_k_w1, a0_cmp_k_w2, a0_cmp_v_w1, a0_cmp_v_w2,
                    a0_w_out, batch=batch, seq=seq)
    xf = _nsa_layer(xf, tabs, a1_norm, a1_w_in, a1_cmp_pos, a1_cmp_k_w1, a1_cmp_k_w2, a1_cmp_v_w1, a1_cmp_v_w2,
                    a1_w_out, batch=batch, seq=seq)

    hn_kv, hn_b2 = _rmsnorm(xf, [kv_norm, b2_norm], BF16)
    kv = _matmul(hn_kv, w_kv, BF16).reshape(batch, seq, 2 * HD)
    m = np.arange(LANES)[None, :] > np.arange(LANES)[:, None]
    tri = jnp.asarray(np.concatenate([m, m], axis=1), BF16)
    xf = _sb_layer(xf, hn_b2, kv, tri, b2_w_in, b2_w_out, batch=batch, seq=seq)
    hn_b3, = _rmsnorm(xf, [b3_norm], BF16)
    xf = _sb_layer(xf, hn_b3, kv, tri, b3_w_in, b3_w_out, batch=batch, seq=seq)

    out, = _rmsnorm(xf, [final_norm], F32)
    return out.reshape(batch, seq, d)
```

```python
import functools
import math

import numpy as np
import jax
import jax.numpy as jnp
from jax import lax
from jax.experimental import pallas as pl
from jax.experimental.pallas import tpu as pltpu

N_HEADS = 32
HEAD_DIM = 128
N_KV_GROUPS = 4
CMP_BLOCK = 32
CMP_STRIDE = 16
SLC_BLOCK = 64
N_SELECT = 16
WINDOW = 512
N_BUCKETS = 32
MAX_DISTANCE = 128
RMS_EPS = 1e-6
FORCE_SCORE = 1e6

LANES = 128
NEG_BIG = -1e30
VMEM_LIMIT = 56 * 1024 * 1024
LOG2E = math.log2(math.e)

F32 = jnp.float32
BF16 = jnp.bfloat16


def _cparams(sem):
    return pltpu.CompilerParams(dimension_semantics=sem, vmem_limit_bytes=VMEM_LIMIT)


def _dot(a, b):
    return jnp.dot(a, b, preferred_element_type=F32)


def _sigmoid(x):
    return 0.5 * jnp.tanh(0.5 * x) + 0.5


def _rmsnorm_kernel(x_ref, *refs):
    n_out = len(refs) // 2
    x = x_ref[...]
    ms = jnp.mean(x * x, axis=-1, keepdims=True)
    y = x * lax.rsqrt(ms + RMS_EPS)
    for g_ref, o_ref in zip(refs[:n_out], refs[n_out:]):
        o_ref[...] = (y * g_ref[...]).astype(o_ref.dtype)


def _rmsnorm(x, gains, out_dtype, tm=512):
    m, d = x.shape
    tm = min(tm, m)
    row = pl.BlockSpec((tm, d), lambda i: (i, 0))
    outs = pl.pallas_call(
        _rmsnorm_kernel,
        grid=(m // tm,),
        in_specs=[row] + [pl.BlockSpec((1, d), lambda i: (0, 0))] * len(gains),
        out_specs=[row] * len(gains),
        out_shape=[jax.ShapeDtypeStruct((m, d), out_dtype)] * len(gains),
        compiler_params=_cparams(("parallel",)),
        name="rmsnorm",
    )(x, *[g.reshape(1, d) for g in gains])
    return outs


def _matmul_kernel(*refs, has_res, has_scale, cast_w, shift, w_rows):
    it = iter(refs)
    x_ref, w_ref = next(it), next(it)
    wn_ref = next(it) if shift else None
    s_ref = next(it) if has_scale else None
    r_ref = next(it) if has_res else None
    o_ref = next(it)
    if cast_w:
        wb_ref = next(it)

        @pl.when(pl.program_id(1) == 0)
        def _():
            w = w_ref[...]
            if shift:
                ax = 0 if w_rows else 1
                w = jnp.concatenate([lax.slice_in_dim(w, shift, w.shape[ax], axis=ax),
                                     lax.slice_in_dim(wn_ref[...], 0, shift, axis=ax)], axis=ax)
            if w_rows:
                w = w.T
            if has_scale:
                w = w * s_ref[...]
            wb_ref[...] = w.astype(BF16)
    else:
        wb_ref = w_ref
    acc = _dot(x_ref[...], wb_ref[...])
    if has_res:
        acc = r_ref[...] + acc
    o_ref[...] = acc.astype(o_ref.dtype)


def _matmul(x, w, out_dtype, *, col0=0, n=None, res=None, colscale=None, w_rows=False, tm=1024, tn=512):
    m, k = x.shape
    n_total = w.shape[0] if w_rows else w.shape[1]
    n = n_total - col0 if n is None else n
    tm, tn = min(tm, m), min(tn, n)
    cast_w = w.dtype != BF16
    shift = col0 % LANES
    base = col0 - shift
    assert m % tm == 0 and n % tn == 0 and base % tn == 0 and tn % LANES == 0
    assert cast_w or (colscale is None and shift == 0 and not w_rows)
    joff = base // tn
    per = tn // LANES
    if w_rows:
        assert shift % 8 == 0
        w_spec = pl.BlockSpec((tn, k), lambda j, i: (j + joff, 0))
        wn_spec = pl.BlockSpec((LANES, k), lambda j, i: ((j + joff + 1) * per, 0))
    else:
        w_spec = pl.BlockSpec((k, tn), lambda j, i: (0, j + joff))
        wn_spec = pl.BlockSpec((k, LANES), lambda j, i: (0, (j + joff + 1) * per))
    in_specs = [pl.BlockSpec((tm, k), lambda j, i: (i, 0)), w_spec]
    args = [x, w]
    if shift:
        in_specs.append(wn_spec)
        args.append(w)
    if colscale is not None:
        in_specs.append(pl.BlockSpec((1, tn), lambda j, i: (0, j)))
        args.append(colscale)
    if res is not None:
        in_specs.append(pl.BlockSpec((tm, tn), lambda j, i: (i, j)))
        args.append(res)
    return pl.pallas_call(
        functools.partial(_matmul_kernel, has_res=res is not None, has_scale=colscale is not None,
                          cast_w=cast_w, shift=shift, w_rows=w_rows),
        grid=(n // tn, m // tm),
        in_specs=in_specs,
        out_specs=pl.BlockSpec((tm, tn), lambda j, i: (i, j)),
        out_shape=jax.ShapeDtypeStruct((m, n), out_dtype),
        scratch_shapes=[pltpu.VMEM((k, tn), BF16)] if cast_w else [],
        compiler_params=_cparams(("parallel", "arbitrary")),
        name="matmul_res" if res is not None else "matmul",
    )(*args)


MASKED_BUCKET = N_BUCKETS


def _t5_bucket_np(dist):
    max_exact = N_BUCKETS // 2
    d = np.maximum(dist, 0)
    log_ratio = np.log(np.maximum(d, max_exact).astype(np.float32) / np.float32(max_exact))
    large = max_exact + (log_ratio / np.float32(math.log(MAX_DISTANCE / max_exact))
                         * np.float32(N_BUCKETS - max_exact)).astype(np.int32)
    return np.where(d < max_exact, d, np.minimum(large, N_BUCKETS - 1)).astype(np.int32)


def _bias_kernel(rb_ref, idx_ref, o_ref):
    h = pl.program_id(0)
    idx = idx_ref[...]
    acc = jnp.zeros(idx.shape, F32)
    for b in range(N_BUCKETS + 1):
        acc = jnp.where(idx == b, rb_ref[h, b], acc)
    o_ref[...] = acc


def _bias_table(rel_bias_ext, bucket_idx):
    h = rel_bias_ext.shape[0]
    rows = bucket_idx.shape[0]
    return pl.pallas_call(
        _bias_kernel,
        grid=(h,),
        in_specs=[pl.BlockSpec(memory_space=pltpu.SMEM),
                  pl.BlockSpec((rows, LANES), lambda i: (0, 0))],
        out_specs=pl.BlockSpec((rows, LANES), lambda i: (0, i)),
        out_shape=jax.ShapeDtypeStruct((rows, h * LANES), F32),
        compiler_params=_cparams(("arbitrary",)),
        name="bias_table",
    )(rel_bias_ext, jnp.asarray(bucket_idx))


def _compress_kernel(kv_ref, pos_ref, kw1_ref, kw2_ref, vw1_ref, vw2_ref, kc_ref, vct_ref, *, groups):
    half = pos_ref.shape[1] // 2
    nch = kv_ref.shape[1]
    gd = groups * HEAD_DIM
    g = pl.program_id(1)
    row = lax.broadcasted_iota(jnp.int32, (nch, HEAD_DIM), 0)

    def one(col, w1_ref, w2_ref):
        x = jnp.concatenate(
            [kv_ref[0, :, pl.ds(pl.multiple_of(o * 2 * gd + col + g * HEAD_DIM, HEAD_DIM), HEAD_DIM)]
             for o in range(CMP_STRIDE)], axis=1).astype(F32)
        lo = (x + pos_ref[:, :half]).astype(BF16)
        hi = pltpu.roll(x + pos_ref[:, half:], nch - 1, 0).astype(BF16)
        h = _dot(lo, w1_ref[:half, :]) + _dot(hi, w1_ref[half:, :])
        h = h * _sigmoid(h)
        o = _dot(h.astype(BF16), w2_ref[...])
        return jnp.where(row < nch - 1, o, 0.0)

    kc_ref[0, 0] = one(0, kw1_ref, kw2_ref).astype(kc_ref.dtype)
    vct_ref[0, 0] = one(gd, vw1_ref, vw2_ref).T.astype(vct_ref.dtype)


def _compress(kv, pos, kw1, kw2, vw1, vw2, *, groups):
    b, nch, wid = kv.shape
    oblk = pl.BlockSpec((1, 1, nch, HEAD_DIM), lambda i, j: (i, j, 0, 0))

    def const(a):
        return pl.BlockSpec(a.shape, lambda i, j: (0,) * a.ndim)

    out = jax.ShapeDtypeStruct((b, groups, nch, HEAD_DIM), BF16)
    return pl.pallas_call(
        functools.partial(_compress_kernel, groups=groups),
        grid=(b, groups),
        in_specs=[pl.BlockSpec((1, nch, wid), lambda i, j: (i, 0, 0)),
                  const(pos), const(kw1), const(kw2), const(vw1), const(vw2)],
        out_specs=[oblk, oblk],
        out_shape=[out, out],
        compiler_params=_cparams(("parallel", "arbitrary")),
        name="compress",
    )(kv, pos, kw1, kw2, vw1, vw2)


def _split3(x):
    a = x.astype(BF16)
    r = x - a.astype(F32)
    b = r.astype(BF16)
    c = (r - b.astype(F32)).astype(BF16)
    return a, b, c


def _nsa_kernel(q_ref, kc_ref, vct_ref, ks_ref, vs_ref, kw_ref, vw_ref, bc_ref, bt_ref, gl_ref,
                zc_ref, zs_ref, zw_ref, ovl_ref, o_ref, sel_scr, acc_scr, s_scr, p_scr, acc2_scr, s2_scr, p2_scr,
                vst_scr, vwt_scr, gate_scr,
                *, group, n_slc, n_sel):
    R = group
    QB = LANES
    W = R * QB
    i = pl.program_id(2)
    q0 = i * QB

    @pl.when(i == 0)
    def _():
        for c in range(vs_ref.shape[1] // QB):
            cols = slice(c * QB, (c + 1) * QB)
            vst_scr[:, cols] = vs_ref[0, cols, :].T
            vwt_scr[:, cols] = vw_ref[0, cols, :].T

    q = q_ref[0]
    qt = jnp.concatenate([q[:, r * HEAD_DIM:(r + 1) * HEAD_DIM].T for r in range(R)], axis=1)

    qq = lax.broadcasted_iota(jnp.int32, (QB, W), 1) & (QB - 1)
    row = lax.broadcasted_iota(jnp.int32, (QB, W), 0)
    mask_c = (q0 + qq) - (row * CMP_STRIDE + (CMP_BLOCK - 1)) >= 0
    s = jnp.where(mask_c, _dot(kc_ref[0, 0], qt) + bc_ref[...], NEG_BIG)
    m = jnp.max(s, axis=0, keepdims=True)
    e = jnp.where(mask_c, jnp.exp2(s - m), 0.0)
    p_c = e * (1.0 / jnp.maximum(jnp.sum(e, axis=0, keepdims=True), 1e-30))
    o_c = _dot(vct_ref[0, 0], p_c.astype(BF16))

    psum = p_c[:, 0:QB]
    for r in range(1, R):
        psum = psum + p_c[:, r * QB:(r + 1) * QB]
    ovl = ovl_ref[...]
    imp = sum(_dot(ovl, part) for part in _split3(psum))[:n_slc]
    j = lax.broadcasted_iota(jnp.int32, (n_slc, QB), 0)
    t = q0 + lax.broadcasted_iota(jnp.int32, (n_slc, QB), 1)
    cur = t // SLC_BLOCK
    forced = (j == 0) | (j == cur) | (j == cur - 1)
    score = jnp.where(forced, FORCE_SCORE, jnp.where(j * SLC_BLOCK <= t, imp, -1.0))
    rank = jnp.zeros((n_slc, QB), F32)
    for jp in range(n_slc):
        other = score[jp:jp + 1, :]
        beats = (other > score) | ((other == score) & (j > jp))
        rank = rank + jnp.where(beats, 1.0, 0.0)
    sel_scr[...] = jnp.where(rank < n_sel, 0.0, NEG_BIG)

    far = WINDOW // QB
    lo_w = jnp.maximum(i - far, 0)

    def tile_slice(kt):
        return pl.ds(pl.multiple_of(kt * QB, QB), QB)

    def bias(tile):
        return bt_ref[tile_slice(tile), :]

    class Chain:
        def __init__(self, k_ref, vt_ref, s_ref, p_ref, a_ref, block_mask):
            self.k_ref, self.vt_ref, self.s_ref, self.p_ref, self.a_ref = k_ref, vt_ref, s_ref, p_ref, a_ref
            self.block_mask = block_mask

        def raw_scores(self, kt):
            return _dot(self.k_ref[0, tile_slice(kt), :], qt)

        def start(self, first):
            s = self.raw_scores(i) + bias(0)
            m = jnp.max(s, axis=0, keepdims=True)
            p = jnp.exp2(s - m)
            l = jnp.sum(p, axis=0, keepdims=True)
            self.a_ref[...] = jnp.zeros_like(self.a_ref)
            self.p_ref[...] = p.astype(BF16)
            self.s_ref[...] = self.raw_scores(first)
            return m, l, jnp.ones((1, W), F32)

        def accumulate(self, kt, alpha):
            self.a_ref[...] = alpha * self.a_ref[...] + _dot(self.vt_ref[:, tile_slice(kt)], self.p_ref[...])

        def step(self, carry, prev, cur, nxt, tile):
            m, l, alpha_prev = carry
            self.accumulate(prev, alpha_prev)
            s = self.s_ref[...] + bias(tile)
            if self.block_mask:
                per = QB // SLC_BLOCK
                add = jnp.concatenate(
                    [jnp.broadcast_to(sel_scr[pl.ds(cur * per + c, 1), :], (SLC_BLOCK, QB)) for c in range(per)],
                    axis=0)
                s = s + jnp.concatenate([add] * R, axis=1)
            m_new = jnp.maximum(m, jnp.max(s, axis=0, keepdims=True))
            alpha = jnp.exp2(m - m_new)
            p = jnp.exp2(s - m_new)
            l = alpha * l + jnp.sum(p, axis=0, keepdims=True)
            self.p_ref[...] = p.astype(BF16)
            self.s_ref[...] = self.raw_scores(nxt)
            return m_new, l, alpha

        def finish(self, carry, last):
            _, l, alpha = carry
            self.accumulate(last, alpha)
            return self.a_ref[...] * (1.0 / l)

    sel = Chain(ks_ref, vst_scr, s_scr, p_scr, acc_scr, True)
    win = Chain(kw_ref, vwt_scr, s2_scr, p2_scr, acc2_scr, False)
    carry_s = sel.start(lo_w)
    carry_w = win.start(lo_w)

    def both(kt, carry):
        carry_s, carry_w = carry
        prev = jnp.where(kt == lo_w, i, kt - 1)
        d = i - kt
        carry_s = sel.step(carry_s, prev, kt, jnp.where(kt == i - 1, 0, kt + 1), jnp.minimum(d, 2))
        carry_w = win.step(carry_w, prev, kt, jnp.minimum(kt + 1, i), jnp.where(d == far, 3, jnp.minimum(d, 2)))
        return carry_s, carry_w

    carry_s, carry_w = lax.fori_loop(lo_w, i, both, (carry_s, carry_w))
    o_w = win.finish(carry_w, jnp.where(i > 0, i - 1, i))

    def sel_only(kt, carry):
        return sel.step(carry, jnp.where(kt == 0, i - 1, kt - 1), kt, jnp.minimum(kt + 1, i), 2)

    carry_s = lax.fori_loop(0, lo_w, sel_only, carry_s)
    o_s = sel.finish(carry_s, jnp.where(lo_w > 0, lo_w - 1, jnp.where(i > 0, i - 1, i)))

    n_heads = R * pl.num_programs(0)
    head0 = pl.program_id(0) * R
    gate_scr[...] = _sigmoid(gl_ref[0]).T
    for r in range(R):
        cols = slice(r * QB, (r + 1) * QB)
        mixed = jnp.zeros((QB, HEAD_DIM), F32)
        for c, (o, z_ref) in enumerate(((o_c, zc_ref), (o_s, zs_ref), (o_w, zw_ref))):
            z = z_ref[0, :, cols]
            gated = o[:, cols] * gate_scr[pl.ds(c * n_heads + head0 + r, 1), :]
            mixed = mixed + gated.T * (z * _sigmoid(z))
        o_ref[0, :, cols] = mixed.astype(o_ref.dtype)


def _nsa_attention(qkv, kc, vct, bias_c, bias_t, gl, z, ovl, *, batch, seq):
    G, R = N_KV_GROUPS, N_HEADS // N_KV_GROUPS
    QB = LANES
    W = R * QB
    nqb = seq // QB
    n_slc = seq // SLC_BLOCK
    n_sel = min(N_SELECT, n_slc)

    def kv_spec(which):
        return pl.BlockSpec((1, seq, HEAD_DIM), lambda g, b, i: (b, 0, N_HEADS + which * G + g))

    def z_spec(c):
        return pl.BlockSpec((1, QB, R * HEAD_DIM), lambda g, b, i: (b, i, c * G + g))

    cmp_spec = pl.BlockSpec((1, 1, LANES, HEAD_DIM), lambda g, b, i: (b, g, 0, 0))
    in_specs = [
        pl.BlockSpec((1, QB, R * HEAD_DIM), lambda g, b, i: (b, i, g)),
        cmp_spec, cmp_spec,
        kv_spec(2), kv_spec(3), kv_spec(4), kv_spec(5),
        pl.BlockSpec((QB, W), lambda g, b, i: (i, g)),
        pl.BlockSpec((4 * QB, W), lambda g, b, i: (0, g)),
        pl.BlockSpec((1, QB, LANES), lambda g, b, i: (b, i, 0)),
        z_spec(0), z_spec(1), z_spec(2),
        pl.BlockSpec(ovl.shape, lambda g, b, i: (0, 0)),
    ]
    return pl.pallas_call(
        functools.partial(_nsa_kernel, group=R, n_slc=n_slc, n_sel=n_sel),
        grid=(G, batch, nqb),
        in_specs=in_specs,
        out_specs=pl.BlockSpec((1, QB, R * HEAD_DIM), lambda g, b, i: (b, i, g)),
        out_shape=jax.ShapeDtypeStruct((batch, seq, N_HEADS * HEAD_DIM), BF16),
        scratch_shapes=[pltpu.VMEM((n_slc, QB), F32), pltpu.VMEM((HEAD_DIM, W), F32),
                        pltpu.VMEM((QB, W), F32), pltpu.VMEM((QB, W), BF16),
                        pltpu.VMEM((HEAD_DIM, W), F32), pltpu.VMEM((QB, W), F32), pltpu.VMEM((QB, W), BF16),
                        pltpu.VMEM((HEAD_DIM, seq), BF16), pltpu.VMEM((HEAD_DIM, seq), BF16),
                        pltpu.VMEM((LANES, QB), F32)],
        compiler_params=_cparams(("parallel", "parallel", "arbitrary")),
        name="nsa_attention",
    )(qkv, kc, vct, qkv, qkv, qkv, qkv, bias_c, bias_t, gl, z, z, z, ovl)


def _sb_kernel(q_ref, k_ref, v_ref, z_ref, tri_ref, o_ref, vt_scr, l_scr, d_scr, w_scr, acc_scr, *, tq, hp):
    i = pl.program_id(2)
    tk = tq
    tri = tri_ref[...]
    n_sub = tk // LANES
    heads = range(hp)

    @pl.when(i == 0)
    def _():
        for g in heads:
            for c in range(v_ref.shape[1] // LANES):
                cols = slice(c * LANES, (c + 1) * LANES)
                vt_scr[g, :, cols] = v_ref[0, cols, g * HEAD_DIM:(g + 1) * HEAD_DIM].T

    qts = [q_ref[0, :, g * HEAD_DIM:(g + 1) * HEAD_DIM].T for g in heads]

    def key0(t):
        return pl.multiple_of(jnp.maximum(i - t, 0) * tk, tk)

    def logits(t):
        k0 = key0(t)
        for g in heads:
            l_scr[g] = _dot(k_ref[0, pl.ds(k0, tk), g * HEAD_DIM:(g + 1) * HEAD_DIM], qts[g])

    def softplus_sums(diagonal):
        totals = []
        for g in heads:
            logit = l_scr[g]
            neg_abs = pltpu.bitcast(pltpu.bitcast(logit, jnp.uint32) | jnp.uint32(0x80000000), F32)
            sp = jnp.maximum(logit, 0.0) + jnp.log(1.0 + jnp.exp2(neg_abs)) * LOG2E
            d = logit - sp
            if diagonal:
                ok = (lax.broadcasted_iota(jnp.int32, (tk, tq), 0) < lax.broadcasted_iota(jnp.int32, (tk, tq), 1))
                sp = jnp.where(ok, sp, 0.0)
                d = jnp.where(ok, d, NEG_BIG)
            d_scr[g] = d
            hi = sp.astype(BF16)
            lo = (sp - hi.astype(F32)).astype(BF16)
            tot = []
            for c in range(n_sub):
                rows = slice(c * LANES, (c + 1) * LANES)
                within = _dot(tri, jnp.concatenate([hi[rows], lo[rows]], axis=0))
                w_scr[g, rows, :] = within
                tot.append(within[0:1] + sp[c * LANES:c * LANES + 1])
            totals.append(tuple(tot))
        return tuple(totals)

    def weigh_values(t, laters, totals):
        k0 = key0(t)
        out = []
        for g in heads:
            later = laters[g]
            parts = [None] * n_sub
            for c in reversed(range(n_sub)):
                rows = slice(c * LANES, (c + 1) * LANES)
                parts[c] = jnp.exp2(d_scr[g, rows, :] - w_scr[g, rows, :] - later)
                later = later + totals[g][c]
            a = jnp.concatenate(parts, axis=0).astype(BF16)
            acc_scr[g] += _dot(vt_scr[g, :, pl.ds(k0, tk)], a)
            out.append(later)
        return tuple(out)

    acc_scr[...] = jnp.zeros_like(acc_scr)
    logits(0)
    totals = softplus_sums(True)
    logits(1)

    def body(t, carry):
        laters, totals = carry
        laters = weigh_values(t - 1, laters, totals)
        totals = softplus_sums(False)
        logits(t + 1)
        return laters, totals

    laters, totals = lax.fori_loop(1, i + 1, body, ((jnp.zeros((1, tq), F32),) * hp, totals))
    weigh_values(i, laters, totals)
    for g in heads:
        cols = slice(g * HEAD_DIM, (g + 1) * HEAD_DIM)
        z = z_ref[0, :, cols]
        o_ref[0, :, cols] = (acc_scr[g].T * (z * _sigmoid(z))).astype(o_ref.dtype)


def _sb_attention(q, kv, z, tri, *, batch, seq, tq=512, hp=4):
    H = N_HEADS
    tq = min(tq, seq)
    wid = hp * HEAD_DIM
    return pl.pallas_call(
        functools.partial(_sb_kernel, tq=tq, hp=hp),
        grid=(batch, H // hp, seq // tq),
        in_specs=[
            pl.BlockSpec((1, tq, wid), lambda b, h, i: (b, i, h)),
            pl.BlockSpec((1, seq, wid), lambda b, h, i: (b, 0, h)),
            pl.BlockSpec((1, seq, wid), lambda b, h, i: (b, 0, H // hp + h)),
            pl.BlockSpec((1, tq, wid), lambda b, h, i: (b, i, h)),
            pl.BlockSpec(tri.shape, lambda b, h, i: (0, 0)),
        ],
        out_specs=pl.BlockSpec((1, tq, wid), lambda b, h, i: (b, i, h)),
        out_shape=jax.ShapeDtypeStruct((batch, seq, H * HEAD_DIM), BF16),
        scratch_shapes=[pltpu.VMEM((hp, HEAD_DIM, seq), BF16), pltpu.VMEM((hp, tq, tq), F32),
                        pltpu.VMEM((hp, tq, tq), F32), pltpu.VMEM((hp, tq, tq), F32),
                        pltpu.VMEM((hp, HEAD_DIM, tq), F32)],
        compiler_params=_cparams(("parallel", "parallel", "arbitrary")),
        name="sb_attention",
    )(q, kv, kv, z, tri)


def _static_tables(seq):
    QB = LANES
    nch = seq // CMP_STRIDE
    n_slc = seq // SLC_BLOCK
    assert nch == LANES and n_slc <= LANES
    q = np.arange(QB)[None, :]
    rows = np.arange(seq)[:, None]
    idx_c = _t5_bucket_np((rows // QB) * QB + q - ((rows % QB) * CMP_STRIDE + CMP_BLOCK - 1))
    k = np.arange(QB)[:, None]
    far = WINDOW // QB
    tiles = []
    for delta, keep in ((0, k <= q), (1, None), (2, None), (far, k > q)):
        idx = _t5_bucket_np(delta * QB + q - k)
        tiles.append(idx if keep is None else np.where(keep, idx, MASKED_BUCKET))
    idx_t = np.concatenate(tiles, axis=0).astype(np.int32)
    cmp_start = np.arange(LANES) * CMP_STRIDE
    slc_start = np.arange(LANES) * SLC_BLOCK
    ovl = ((cmp_start[None, :] < slc_start[:, None] + SLC_BLOCK)
           & (cmp_start[None, :] + CMP_BLOCK - 1 >= slc_start[:, None]))
    ovl = ovl & (np.arange(LANES)[None, :] < nch - 1) & (np.arange(LANES)[:, None] < n_slc)
    return idx_c, idx_t, ovl.astype(np.float32)


def _nsa_layer(xf, tabs, norm, w_in, cmp_pos, kw1, kw2, vw1, vw2, w_out, *, batch, seq):
    bias_c, bias_t, ovl = tabs
    H, G, Dh = N_HEADS, N_KV_GROUPS, HEAD_DIM
    HD, GD = H * Dh, G * Dh
    n_qkv = HD + 6 * GD
    hn, = _rmsnorm(xf, [norm], BF16)
    qscale = jnp.concatenate([jnp.full((1, HD), LOG2E / math.sqrt(Dh), F32), jnp.ones((1, 6 * GD), F32)], axis=1)
    w_in_t = w_in.T
    qkv = _matmul(hn, w_in_t, BF16, n=n_qkv, colscale=qscale, w_rows=True)
    assert 3 * H <= LANES
    gl = _matmul(hn, w_in_t, F32, col0=n_qkv, n=LANES, w_rows=True)
    z = _matmul(hn, w_in_t, F32, col0=n_qkv + 3 * H, n=3 * HD, w_rows=True)

    nch = seq // CMP_STRIDE
    cmp_in = qkv[:, HD:HD + 2 * GD].reshape(batch, nch, CMP_STRIDE * 2 * GD)
    kc, vct = _compress(cmp_in, cmp_pos.reshape(1, CMP_BLOCK * Dh),
                        kw1.astype(BF16), kw2.astype(BF16), vw1.astype(BF16), vw2.astype(BF16), groups=G)
    mixed = _nsa_attention(qkv.reshape(batch, seq, n_qkv), kc, vct, bias_c, bias_t,
                           gl.reshape(batch, seq, LANES), z.reshape(batch, seq, 3 * HD),
                           ovl, batch=batch, seq=seq)
    return _matmul(mixed.reshape(batch * seq, HD), w_out, F32, res=xf)


def _sb_layer(xf, hn, kv, tri, w_in, w_out, *, batch, seq):
    H, Dh = N_HEADS, HEAD_DIM
    HD = H * Dh
    q = _matmul(hn, w_in, BF16, n=HD, colscale=jnp.full((1, HD), LOG2E / math.sqrt(Dh), F32))
    z = _matmul(hn, w_in, F32, col0=HD, n=HD)
    o = _sb_attention(q.reshape(batch, seq, HD), kv, z.reshape(batch, seq, HD), tri, batch=batch, seq=seq)
    return _matmul(o.reshape(batch * seq, HD), w_out, F32, res=xf)


def kernel(x, rel_bias, a0_norm, a0_w_in, a0_cmp_pos, a0_cmp_k_w1, a0_cmp_k_w2, a0_cmp_v_w1, a0_cmp_v_w2, a0_w_out, a1_norm, a1_w_in, a1_cmp_pos, a1_cmp_k_w1, a1_cmp_k_w2, a1_cmp_v_w1, a1_cmp_v_w2, a1_w_out, kv_norm, w_kv, b2_norm, b2_w_in, b2_w_out, b3_norm, b3_w_in, b3_w_out, final_norm):
    batch, seq, d = x.shape
    HD = N_HEADS * HEAD_DIM
    xf = x.reshape(batch * seq, d)

    idx_c, idx_t, ovl = _static_tables(seq)
    rel_bias_ext = jnp.concatenate([rel_bias.T * LOG2E, jnp.full((N_HEADS, 1), NEG_BIG, F32)], axis=1)
    tabs = (_bias_table(rel_bias_ext, idx_c), _bias_table(rel_bias_ext, idx_t), jnp.asarray(ovl, BF16))

    xf = _nsa_layer(xf, tabs, a0_norm, a0_w_in, a0_cmp_pos, a0_cmp_k_w1, a0_cmp_k_w2, a0_cmp_v_w1, a0_cmp_v_w2,
                    a0_w_out, batch=batch, seq=seq)
    xf = _nsa_layer(xf, tabs, a1_norm, a1_w_in, a1_cmp_pos, a1_cmp_k_w1, a1_cmp_k_w2, a1_cmp_v_w1, a1_cmp_v_w2,
                    a1_w_out, batch=batch, seq=seq)

    hn_kv, hn_b2 = _rmsnorm(xf, [kv_norm, b2_norm], BF16)
    kv = _matmul(hn_kv, w_kv, BF16).reshape(batch, seq, 2 * HD)
    m = np.arange(LANES)[None, :] > np.arange(LANES)[:, None]
    tri = jnp.asarray(np.concatenate([m, m], axis=1), BF16)
    xf = _sb_layer(xf, hn_b2, kv, tri, b2_w_in, b2_w_out, batch=batch, seq=seq)
    hn_b3, = _rmsnorm(xf, [b3_norm], BF16)
    xf = _sb_layer(xf, hn_b3, kv, tri, b3_w_in, b3_w_out, batch=batch, seq=seq)

    out, = _rmsnorm(xf, [final_norm], F32)
    return out.reshape(batch, seq, d)
```

```python
import functools
import math

import numpy as np
import jax
import jax.numpy as jnp
from jax import lax
from jax.experimental import pallas as pl
from jax.experimental.pallas import tpu as pltpu

N_HEADS = 32
HEAD_DIM = 128
N_KV_GROUPS = 4
CMP_BLOCK = 32
CMP_STRIDE = 16
SLC_BLOCK = 64
N_SELECT = 16
WINDOW = 512
N_BUCKETS = 32
MAX_DISTANCE = 128
RMS_EPS = 1e-6
FORCE_SCORE = 1e6

LANES = 128
NEG_BIG = -1e30
VMEM_LIMIT = 56 * 1024 * 1024
LOG2E = math.log2(math.e)

F32 = jnp.float32
BF16 = jnp.bfloat16


def _cparams(sem):
    return pltpu.CompilerParams(dimension_semantics=sem, vmem_limit_bytes=VMEM_LIMIT)


def _dot(a, b):
    return jnp.dot(a, b, preferred_element_type=F32)


def _sigmoid(x):
    return 0.5 * jnp.tanh(0.5 * x) + 0.5


def _rmsnorm_kernel(x_ref, *refs):
    n_out = len(refs) // 2
    x = x_ref[...]
    ms = jnp.mean(x * x, axis=-1, keepdims=True)
    y = x * lax.rsqrt(ms + RMS_EPS)
    for g_ref, o_ref in zip(refs[:n_out], refs[n_out:]):
        o_ref[...] = (y * g_ref[...]).astype(o_ref.dtype)


def _rmsnorm(x, gains, out_dtype, tm=512):
    m, d = x.shape
    tm = min(tm, m)
    row = pl.BlockSpec((tm, d), lambda i: (i, 0))
    outs = pl.pallas_call(
        _rmsnorm_kernel,
        grid=(m // tm,),
        in_specs=[row] + [pl.BlockSpec((1, d), lambda i: (0, 0))] * len(gains),
        out_specs=[row] * len(gains),
        out_shape=[jax.ShapeDtypeStruct((m, d), out_dtype)] * len(gains),
        compiler_params=_cparams(("parallel",)),
        name="rmsnorm",
    )(x, *[g.reshape(1, d) for g in gains])
    return outs


W_SLABS = 4


def _matmul_kernel(*refs, has_res, has_scale, cast_w, shift, w_rows):
    it = iter(refs)
    x_ref = next(it)
    w_refs = [next(it) for _ in range(W_SLABS if cast_w else 1)]
    wn_ref = next(it) if shift else None
    s_ref = next(it) if has_scale else None
    r_ref = next(it) if has_res else None
    o_ref = next(it)
    if cast_w:
        wb_ref = next(it)

        @pl.when(pl.program_id(1) == 0)
        def _():
            ks = wb_ref.shape[0] // W_SLABS
            for s, w_ref in enumerate(w_refs):
                w = w_ref[...]
                if shift:
                    ax = 0 if w_rows else 1
                    nxt = wn_ref[:, s * ks:(s + 1) * ks] if w_rows else wn_ref[s * ks:(s + 1) * ks, :]
                    w = jnp.concatenate([lax.slice_in_dim(w, shift, w.shape[ax], axis=ax),
                                         lax.slice_in_dim(nxt, 0, shift, axis=ax)], axis=ax)
                if w_rows:
                    w = w.T
                if has_scale:
                    w = w * s_ref[...]
                wb_ref[s * ks:(s + 1) * ks, :] = w.astype(BF16)
    else:
        wb_ref = w_refs[0]
    acc = _dot(x_ref[...], wb_ref[...])
    if has_res:
        acc = r_ref[...] + acc
    o_ref[...] = acc.astype(o_ref.dtype)


def _matmul(x, w, out_dtype, *, col0=0, n=None, res=None, colscale=None, w_rows=False, tm=1024, tn=512):
    m, k = x.shape
    n_total = w.shape[0] if w_rows else w.shape[1]
    n = n_total - col0 if n is None else n
    tm, tn = min(tm, m), min(tn, n)
    cast_w = w.dtype != BF16
    shift = col0 % LANES
    base = col0 - shift
    assert m % tm == 0 and n % tn == 0 and base % tn == 0 and tn % LANES == 0
    assert cast_w or (colscale is None and shift == 0 and not w_rows)
    joff = base // tn
    per = tn // LANES
    n_i = m // tm
    in_specs = [pl.BlockSpec((tm, k), lambda j, i: (i, 0))]
    args = [x]
    if cast_w:
        assert k % (W_SLABS * LANES) == 0
        ks = k // W_SLABS
        stagger = n_i > W_SLABS
        for s in range(W_SLABS):
            def tile(j, i, s=s):
                ahead = (i > s).astype(jnp.int32) if stagger else 0
                return jnp.minimum(j + ahead, n // tn - 1) + joff
            if w_rows:
                in_specs.append(pl.BlockSpec((tn, ks), lambda j, i, s=s, tile=tile: (tile(j, i), s)))
            else:
                in_specs.append(pl.BlockSpec((ks, tn), lambda j, i, s=s, tile=tile: (s, tile(j, i))))
            args.append(w)
    else:
        in_specs.append(pl.BlockSpec((k, tn), lambda j, i: (0, j + joff)))
        args.append(w)
    if shift:
        if w_rows:
            assert shift % 8 == 0
            in_specs.append(pl.BlockSpec((LANES, k), lambda j, i: ((j + joff + 1) * per, 0)))
        else:
            in_specs.append(pl.BlockSpec((k, LANES), lambda j, i: (0, (j + joff + 1) * per)))
        args.append(w)
    if colscale is not None:
        in_specs.append(pl.BlockSpec((1, tn), lambda j, i: (0, j)))
        args.append(colscale)
    if res is not None:
        in_specs.append(pl.BlockSpec((tm, tn), lambda j, i: (i, j)))
        args.append(res)
    return pl.pallas_call(
        functools.partial(_matmul_kernel, has_res=res is not None, has_scale=colscale is not None,
                          cast_w=cast_w, shift=shift, w_rows=w_rows),
        grid=(n // tn, n_i),
        in_specs=in_specs,
        out_specs=pl.BlockSpec((tm, tn), lambda j, i: (i, j)),
        out_shape=jax.ShapeDtypeStruct((m, n), out_dtype),
        scratch_shapes=[pltpu.VMEM((k, tn), BF16)] if cast_w else [],
        compiler_params=_cparams(("arbitrary", "arbitrary")),
        name="matmul_res" if res is not None else "matmul",
    )(*args)


MASKED_BUCKET = N_BUCKETS


def _t5_bucket_np(dist):
    max_exact = N_BUCKETS // 2
    d = np.maximum(dist, 0)
    log_ratio = np.log(np.maximum(d, max_exact).astype(np.float32) / np.float32(max_exact))
    large = max_exact + (log_ratio / np.float32(math.log(MAX_DISTANCE / max_exact))
                         * np.float32(N_BUCKETS - max_exact)).astype(np.int32)
    return np.where(d < max_exact, d, np.minimum(large, N_BUCKETS - 1)).astype(np.int32)


def _bias_kernel(rb_ref, idx_ref, o_ref):
    h = pl.program_id(0)
    idx = idx_ref[...]
    acc = jnp.zeros(idx.shape, F32)
    for b in range(N_BUCKETS + 1):
        acc = jnp.where(idx == b, rb_ref[h, b], acc)
    o_ref[...] = acc


def _bias_table(rel_bias_ext, bucket_idx):
    h = rel_bias_ext.shape[0]
    rows = bucket_idx.shape[0]
    return pl.pallas_call(
        _bias_kernel,
        grid=(h,),
        in_specs=[pl.BlockSpec(memory_space=pltpu.SMEM),
                  pl.BlockSpec((rows, LANES), lambda i: (0, 0))],
        out_specs=pl.BlockSpec((rows, LANES), lambda i: (0, i)),
        out_shape=jax.ShapeDtypeStruct((rows, h * LANES), F32),
        compiler_params=_cparams(("arbitrary",)),
        name="bias_table",
    )(rel_bias_ext, jnp.asarray(bucket_idx))


def _compress_kernel(kv_ref, pos_ref, kw1_ref, kw2_ref, vw1_ref, vw2_ref, kc_ref, vct_ref, *, groups):
    half = pos_ref.shape[1] // 2
    nch = kv_ref.shape[1]
    gd = groups * HEAD_DIM
    g = pl.program_id(1)
    row = lax.broadcasted_iota(jnp.int32, (nch, HEAD_DIM), 0)

    def one(col, w1_ref, w2_ref):
        x = jnp.concatenate(
            [kv_ref[0, :, pl.ds(pl.multiple_of(o * 2 * gd + col + g * HEAD_DIM, HEAD_DIM), HEAD_DIM)]
             for o in range(CMP_STRIDE)], axis=1).astype(F32)
        lo = (x + pos_ref[:, :half]).astype(BF16)
        hi = pltpu.roll(x + pos_ref[:, half:], nch - 1, 0).astype(BF16)
        h = _dot(lo, w1_ref[:half, :]) + _dot(hi, w1_ref[half:, :])
        h = h * _sigmoid(h)
        o = _dot(h.astype(BF16), w2_ref[...])
        return jnp.where(row < nch - 1, o, 0.0)

    kc_ref[0, 0] = one(0, kw1_ref, kw2_ref).astype(kc_ref.dtype)
    vct_ref[0, 0] = one(gd, vw1_ref, vw2_ref).T.astype(vct_ref.dtype)


def _compress(kv, pos, kw1, kw2, vw1, vw2, *, groups):
    b, nch, wid = kv.shape
    oblk = pl.BlockSpec((1, 1, nch, HEAD_DIM), lambda i, j: (i, j, 0, 0))

    def const(a):
        return pl.BlockSpec(a.shape, lambda i, j: (0,) * a.ndim)

    out = jax.ShapeDtypeStruct((b, groups, nch, HEAD_DIM), BF16)
    return pl.pallas_call(
        functools.partial(_compress_kernel, groups=groups),
        grid=(b, groups),
        in_specs=[pl.BlockSpec((1, nch, wid), lambda i, j: (i, 0, 0)),
                  const(pos), const(kw1), const(kw2), const(vw1), const(vw2)],
        out_specs=[oblk, oblk],
        out_shape=[out, out],
        compiler_params=_cparams(("parallel", "arbitrary")),
        name="compress",
    )(kv, pos, kw1, kw2, vw1, vw2)


def _split3(x):
    a = x.astype(BF16)
    r = x - a.astype(F32)
    b = r.astype(BF16)
    c = (r - b.astype(F32)).astype(BF16)
    return a, b, c


def _nsa_kernel(q_ref, kc_ref, vct_ref, ks_ref, vs_ref, kw_ref, vw_ref, bc_ref, bt_ref, gl_ref,
                zc_ref, zs_ref, zw_ref, ovl_ref, o_ref, sel_scr, acc_scr, s_scr, p_scr, acc2_scr, s2_scr, p2_scr,
                vst_scr, vwt_scr, gate_scr,
                *, group, n_slc, n_sel):
    R = group
    QB = LANES
    W = R * QB
    i = pl.program_id(2)
    q0 = i * QB

    @pl.when(i == 0)
    def _():
        for c in range(vs_ref.shape[1] // QB):
            cols = slice(c * QB, (c + 1) * QB)
            vst_scr[:, cols] = vs_ref[0, cols, :].T
            vwt_scr[:, cols] = vw_ref[0, cols, :].T

    q = q_ref[0]
    qt = jnp.concatenate([q[:, r * HEAD_DIM:(r + 1) * HEAD_DIM].T for r in range(R)], axis=1)

    qq = lax.broadcasted_iota(jnp.int32, (QB, W), 1) & (QB - 1)
    row = lax.broadcasted_iota(jnp.int32, (QB, W), 0)
    mask_c = (q0 + qq) - (row * CMP_STRIDE + (CMP_BLOCK - 1)) >= 0
    s = jnp.where(mask_c, _dot(kc_ref[0, 0], qt) + bc_ref[...], NEG_BIG)
    m = jnp.max(s, axis=0, keepdims=True)
    e = jnp.where(mask_c, jnp.exp2(s - m), 0.0)
    p_c = e * (1.0 / jnp.maximum(jnp.sum(e, axis=0, keepdims=True), 1e-30))
    o_c = _dot(vct_ref[0, 0], p_c.astype(BF16))

    psum = p_c[:, 0:QB]
    for r in range(1, R):
        psum = psum + p_c[:, r * QB:(r + 1) * QB]
    ovl = ovl_ref[...]
    imp = sum(_dot(ovl, part) for part in _split3(psum))[:n_slc]
    j = lax.broadcasted_iota(jnp.int32, (n_slc, QB), 0)
    t = q0 + lax.broadcasted_iota(jnp.int32, (n_slc, QB), 1)
    cur = t // SLC_BLOCK
    forced = (j == 0) | (j == cur) | (j == cur - 1)
    score = jnp.where(forced, FORCE_SCORE, jnp.where(j * SLC_BLOCK <= t, imp, -1.0))
    rank = jnp.zeros((n_slc, QB), F32)
    for jp in range(n_slc):
        other = score[jp:jp + 1, :]
        beats = (other > score) | ((other == score) & (j > jp))
        rank = rank + jnp.where(beats, 1.0, 0.0)
    sel_scr[...] = jnp.where(rank < n_sel, 0.0, NEG_BIG)

    far = WINDOW // QB
    lo_w = jnp.maximum(i - far, 0)

    def tile_slice(kt):
        return pl.ds(pl.multiple_of(kt * QB, QB), QB)

    def bias(tile):
        return bt_ref[tile_slice(tile), :]

    class Chain:
        def __init__(self, k_ref, vt_ref, s_ref, p_ref, a_ref, block_mask):
            self.k_ref, self.vt_ref, self.s_ref, self.p_ref, self.a_ref = k_ref, vt_ref, s_ref, p_ref, a_ref
            self.block_mask = block_mask

        def raw_scores(self, kt):
            return _dot(self.k_ref[0, tile_slice(kt), :], qt)

        def start(self, first):
            s = self.raw_scores(i) + bias(0)
            m = jnp.max(s, axis=0, keepdims=True)
            p = jnp.exp2(s - m)
            l = jnp.sum(p, axis=0, keepdims=True)
            self.a_ref[...] = jnp.zeros_like(self.a_ref)
            self.p_ref[...] = p.astype(BF16)
            self.s_ref[...] = self.raw_scores(first)
            return m, l, jnp.ones((1, W), F32)

        def accumulate(self, kt, alpha):
            self.a_ref[...] = alpha * self.a_ref[...] + _dot(self.vt_ref[:, tile_slice(kt)], self.p_ref[...])

        def step(self, carry, prev, cur, nxt, tile):
            m, l, alpha_prev = carry
            self.accumulate(prev, alpha_prev)
            s = self.s_ref[...] + bias(tile)
            if self.block_mask:
                per = QB // SLC_BLOCK
                add = jnp.concatenate(
                    [jnp.broadcast_to(sel_scr[pl.ds(cur * per + c, 1), :], (SLC_BLOCK, QB)) for c in range(per)],
                    axis=0)
                s = s + jnp.concatenate([add] * R, axis=1)
            m_new = jnp.maximum(m, jnp.max(s, axis=0, keepdims=True))
            alpha = jnp.exp2(m - m_new)
            p = jnp.exp2(s - m_new)
            l = alpha * l + jnp.sum(p, axis=0, keepdims=True)
            self.p_ref[...] = p.astype(BF16)
            self.s_ref[...] = self.raw_scores(nxt)
            return m_new, l, alpha

        def finish(self, carry, last):
            _, l, alpha = carry
            self.accumulate(last, alpha)
            return self.a_ref[...] * (1.0 / l)

    sel = Chain(ks_ref, vst_scr, s_scr, p_scr, acc_scr, True)
    win = Chain(kw_ref, vwt_scr, s2_scr, p2_scr, acc2_scr, False)
    carry_s = sel.start(lo_w)
    carry_w = win.start(lo_w)

    def both(kt, carry):
        carry_s, carry_w = carry
        prev = jnp.where(kt == lo_w, i, kt - 1)
        d = i - kt
        carry_s = sel.step(carry_s, prev, kt, jnp.where(kt == i - 1, 0, kt + 1), jnp.minimum(d, 2))
        carry_w = win.step(carry_w, prev, kt, jnp.minimum(kt + 1, i), jnp.where(d == far, 3, jnp.minimum(d, 2)))
        return carry_s, carry_w

    carry_s, carry_w = lax.fori_loop(lo_w, i, both, (carry_s, carry_w))
    o_w = win.finish(carry_w, jnp.where(i > 0, i - 1, i))

    def sel_only(kt, carry):
        return sel.step(carry, jnp.where(kt == 0, i - 1, kt - 1), kt, jnp.minimum(kt + 1, i), 2)

    carry_s = lax.fori_loop(0, lo_w, sel_only, carry_s)
    o_s = sel.finish(carry_s, jnp.where(lo_w > 0, lo_w - 1, jnp.where(i > 0, i - 1, i)))

    n_heads = R * pl.num_programs(0)
    head0 = pl.program_id(0) * R
    gate_scr[...] = _sigmoid(gl_ref[0]).T
    for r in range(R):
        cols = slice(r * QB, (r + 1) * QB)
        mixed = jnp.zeros((QB, HEAD_DIM), F32)
        for c, (o, z_ref) in enumerate(((o_c, zc_ref), (o_s, zs_ref), (o_w, zw_ref))):
            z = z_ref[0, :, cols]
            gated = o[:, cols] * gate_scr[pl.ds(c * n_heads + head0 + r, 1), :]
            mixed = mixed + gated.T * (z * _sigmoid(z))
        o_ref[0, :, cols] = mixed.astype(o_ref.dtype)


def _nsa_attention(qkv, kc, vct, bias_c, bias_t, gl, z, ovl, *, batch, seq):
    G, R = N_KV_GROUPS, N_HEADS // N_KV_GROUPS
    QB = LANES
    W = R * QB
    nqb = seq // QB
    n_slc = seq // SLC_BLOCK
    n_sel = min(N_SELECT, n_slc)

    def kv_spec(which):
        return pl.BlockSpec((1, seq, HEAD_DIM), lambda g, b, i: (b, 0, N_HEADS + which * G + g))

    def z_spec(c):
        return pl.BlockSpec((1, QB, R * HEAD_DIM), lambda g, b, i: (b, i, c * G + g))

    cmp_spec = pl.BlockSpec((1, 1, LANES, HEAD_DIM), lambda g, b, i: (b, g, 0, 0))
    in_specs = [
        pl.BlockSpec((1, QB, R * HEAD_DIM), lambda g, b, i: (b, i, g)),
        cmp_spec, cmp_spec,
        kv_spec(2), kv_spec(3), kv_spec(4), kv_spec(5),
        pl.BlockSpec((QB, W), lambda g, b, i: (i, g)),
        pl.BlockSpec((4 * QB, W), lambda g, b, i: (0, g)),
        pl.BlockSpec((1, QB, LANES), lambda g, b, i: (b, i, 0)),
        z_spec(0), z_spec(1), z_spec(2),
        pl.BlockSpec(ovl.shape, lambda g, b, i: (0, 0)),
    ]
    return pl.pallas_call(
        functools.partial(_nsa_kernel, group=R, n_slc=n_slc, n_sel=n_sel),
        grid=(G, batch, nqb),
        in_specs=in_specs,
        out_specs=pl.BlockSpec((1, QB, R * HEAD_DIM), lambda g, b, i: (b, i, g)),
        out_shape=jax.ShapeDtypeStruct((batch, seq, N_HEADS * HEAD_DIM), BF16),
        scratch_shapes=[pltpu.VMEM((n_slc, QB), F32), pltpu.VMEM((HEAD_DIM, W), F32),
                        pltpu.VMEM((QB, W), F32), pltpu.VMEM((QB, W), BF16),
                        pltpu.VMEM((HEAD_DIM, W), F32), pltpu.VMEM((QB, W), F32), pltpu.VMEM((QB, W), BF16),
                        pltpu.VMEM((HEAD_DIM, seq), BF16), pltpu.VMEM((HEAD_DIM, seq), BF16),
                        pltpu.VMEM((LANES, QB), F32)],
        compiler_params=_cparams(("parallel", "parallel", "arbitrary")),
        name="nsa_attention",
    )(qkv, kc, vct, qkv, qkv, qkv, qkv, bias_c, bias_t, gl, z, z, z, ovl)


def _sb_kernel(q_ref, k_ref, v_ref, z_ref, tri_ref, o_ref, vt_scr, l_scr, d_scr, w_scr, acc_scr, *, tq, hp):
    i = pl.program_id(2)
    tk = tq
    tri = tri_ref[...]
    n_sub = tk // LANES
    heads = range(hp)

    @pl.when(i == 0)
    def _():
        for g in heads:
            for c in range(v_ref.shape[1] // LANES):
                cols = slice(c * LANES, (c + 1) * LANES)
                vt_scr[g, :, cols] = v_ref[0, cols, g * HEAD_DIM:(g + 1) * HEAD_DIM].T

    qts = [q_ref[0, :, g * HEAD_DIM:(g + 1) * HEAD_DIM].T for g in heads]

    def key0(t):
        return pl.multiple_of(jnp.maximum(i - t, 0) * tk, tk)

    def logits(t):
        k0 = key0(t)
        for g in heads:
            l_scr[g] = _dot(k_ref[0, pl.ds(k0, tk), g * HEAD_DIM:(g + 1) * HEAD_DIM], qts[g])

    def softplus_sums(diagonal):
        totals = []
        for g in heads:
            logit = l_scr[g]
            neg_abs = pltpu.bitcast(pltpu.bitcast(logit, jnp.uint32) | jnp.uint32(0x80000000), F32)
            sp = jnp.maximum(logit, 0.0) + jnp.log(1.0 + jnp.exp2(neg_abs)) * LOG2E
            d = logit - sp
            if diagonal:
                ok = (lax.broadcasted_iota(jnp.int32, (tk, tq), 0) < lax.broadcasted_iota(jnp.int32, (tk, tq), 1))
                sp = jnp.where(ok, sp, 0.0)
                d = jnp.where(ok, d, NEG_BIG)
            d_scr[g] = d
            hi = sp.astype(BF16)
            lo = (sp - hi.astype(F32)).astype(BF16)
            tot = []
            for c in range(n_sub):
                rows = slice(c * LANES, (c + 1) * LANES)
                within = _dot(tri, jnp.concatenate([hi[rows], lo[rows]], axis=0))
                w_scr[g, rows, :] = within
                tot.append(within[0:1] + sp[c * LANES:c * LANES + 1])
            totals.append(tuple(tot))
        return tuple(totals)

    def weigh_values(t, laters, totals):
        k0 = key0(t)
        out = []
        for g in heads:
            later = laters[g]
            parts = [None] * n_sub
            for c in reversed(range(n_sub)):
                rows = slice(c * LANES, (c + 1) * LANES)
                parts[c] = jnp.exp2(d_scr[g, rows, :] - w_scr[g, rows, :] - later)
                later = later + totals[g][c]
            a = jnp.concatenate(parts, axis=0).astype(BF16)
            acc_scr[g] += _dot(vt_scr[g, :, pl.ds(k0, tk)], a)
            out.append(later)
        return tuple(out)

    acc_scr[...] = jnp.zeros_like(acc_scr)
    logits(0)
    totals = softplus_sums(True)
    logits(1)

    def body(t, carry):
        laters, totals = carry
        laters = weigh_values(t - 1, laters, totals)
        totals = softplus_sums(False)
        logits(t + 1)
        return laters, totals

    laters, totals = lax.fori_loop(1, i + 1, body, ((jnp.zeros((1, tq), F32),) * hp, totals))
    weigh_values(i, laters, totals)
    for g in heads:
        cols = slice(g * HEAD_DIM, (g + 1) * HEAD_DIM)
        z = z_ref[0, :, cols]
        o_ref[0, :, cols] = (acc_scr[g].T * (z * _sigmoid(z))).astype(o_ref.dtype)


def _sb_attention(q, kv, z, tri, *, batch, seq, tq=512, hp=4):
    H = N_HEADS
    tq = min(tq, seq)
    wid = hp * HEAD_DIM
    return pl.pallas_call(
        functools.partial(_sb_kernel, tq=tq, hp=hp),
        grid=(batch, H // hp, seq // tq),
        in_specs=[
            pl.BlockSpec((1, tq, wid), lambda b, h, i: (b, i, h)),
            pl.BlockSpec((1, seq, wid), lambda b, h, i: (b, 0, h)),
            pl.BlockSpec((1, seq, wid), lambda b, h, i: (b, 0, H // hp + h)),
            pl.BlockSpec((1, tq, wid), lambda b, h, i: (b, i, h)),
            pl.BlockSpec(tri.shape, lambda b, h, i: (0, 0)),
        ],
        out_specs=pl.BlockSpec((1, tq, wid), lambda b, h, i: (b, i, h)),
        out_shape=jax.ShapeDtypeStruct((batch, seq, H * HEAD_DIM), BF16),
        scratch_shapes=[pltpu.VMEM((hp, HEAD_DIM, seq), BF16), pltpu.VMEM((hp, tq, tq), F32),
                        pltpu.VMEM((hp, tq, tq), F32), pltpu.VMEM((hp, tq, tq), F32),
                        pltpu.VMEM((hp, HEAD_DIM, tq), F32)],
        compiler_params=_cparams(("parallel", "parallel", "arbitrary")),
        name="sb_attention",
    )(q, kv, kv, z, tri)


def _static_tables(seq):
    QB = LANES
    nch = seq // CMP_STRIDE
    n_slc = seq // SLC_BLOCK
    assert nch == LANES and n_slc <= LANES
    q = np.arange(QB)[None, :]
    rows = np.arange(seq)[:, None]
    idx_c = _t5_bucket_np((rows // QB) * QB + q - ((rows % QB) * CMP_STRIDE + CMP_BLOCK - 1))
    k = np.arange(QB)[:, None]
    far = WINDOW // QB
    tiles = []
    for delta, keep in ((0, k <= q), (1, None), (2, None), (far, k > q)):
        idx = _t5_bucket_np(delta * QB + q - k)
        tiles.append(idx if keep is None else np.where(keep, idx, MASKED_BUCKET))
    idx_t = np.concatenate(tiles, axis=0).astype(np.int32)
    cmp_start = np.arange(LANES) * CMP_STRIDE
    slc_start = np.arange(LANES) * SLC_BLOCK
    ovl = ((cmp_start[None, :] < slc_start[:, None] + SLC_BLOCK)
           & (cmp_start[None, :] + CMP_BLOCK - 1 >= slc_start[:, None]))
    ovl = ovl & (np.arange(LANES)[None, :] < nch - 1) & (np.arange(LANES)[:, None] < n_slc)
    return idx_c, idx_t, ovl.astype(np.float32)


def _nsa_layer(xf, tabs, norm, w_in, cmp_pos, kw1, kw2, vw1, vw2, w_out, *, batch, seq):
    bias_c, bias_t, ovl = tabs
    H, G, Dh = N_HEADS, N_KV_GROUPS, HEAD_DIM
    HD, GD = H * Dh, G * Dh
    n_qkv = HD + 6 * GD
    hn, = _rmsnorm(xf, [norm], BF16)
    qscale = jnp.concatenate([jnp.full((1, HD), LOG2E / math.sqrt(Dh), F32), jnp.ones((1, 6 * GD), F32)], axis=1)
    w_in_t = w_in.T
    qkv = _matmul(hn, w_in_t, BF16, n=n_qkv, colscale=qscale, w_rows=True)
    assert 3 * H <= LANES
    gl = _matmul(hn, w_in_t, F32, col0=n_qkv, n=LANES, w_rows=True)
    z = _matmul(hn, w_in_t, F32, col0=n_qkv + 3 * H, n=3 * HD, w_rows=True)

    nch = seq // CMP_STRIDE
    cmp_in = qkv[:, HD:HD + 2 * GD].reshape(batch, nch, CMP_STRIDE * 2 * GD)
    kc, vct = _compress(cmp_in, cmp_pos.reshape(1, CMP_BLOCK * Dh),
                        kw1.astype(BF16), kw2.astype(BF16), vw1.astype(BF16), vw2.astype(BF16), groups=G)
    mixed = _nsa_attention(qkv.reshape(batch, seq, n_qkv), kc, vct, bias_c, bias_t,
                           gl.reshape(batch, seq, LANES), z.reshape(batch, seq, 3 * HD),
                           ovl, batch=batch, seq=seq)
    return _matmul(mixed.reshape(batch * seq, HD), w_out, F32, res=xf)


def _sb_layer(xf, hn, kv, tri, w_in, w_out, *, batch, seq):
    H, Dh = N_HEADS, HEAD_DIM
    HD = H * Dh
    q = _matmul(hn, w_in, BF16, n=HD, colscale=jnp.full((1, HD), LOG2E / math.sqrt(Dh), F32))
    z = _matmul(hn, w_in, F32, col0=HD, n=HD)
    o = _sb_attention(q.reshape(batch, seq, HD), kv, z.reshape(batch, seq, HD), tri, batch=batch, seq=seq)
    return _matmul(o.reshape(batch * seq, HD), w_out, F32, res=xf)


def kernel(x, rel_bias, a0_norm, a0_w_in, a0_cmp_pos, a0_cmp_k_w1, a0_cmp_k_w2, a0_cmp_v_w1, a0_cmp_v_w2, a0_w_out, a1_norm, a1_w_in, a1_cmp_pos, a1_cmp_k_w1, a1_cmp_k_w2, a1_cmp_v_w1, a1_cmp_v_w2, a1_w_out, kv_norm, w_kv, b2_norm, b2_w_in, b2_w_out, b3_norm, b3_w_in, b3_w_out, final_norm):
    batch, seq, d = x.shape
    HD = N_HEADS * HEAD_DIM
    xf = x.reshape(batch * seq, d)

    idx_c, idx_t, ovl = _static_tables(seq)
    rel_bias_ext = jnp.concatenate([rel_bias.T * LOG2E, jnp.full((N_HEADS, 1), NEG_BIG, F32)], axis=1)
    tabs = (_bias_table(rel_bias_ext, idx_c), _bias_table(rel_bias_ext, idx_t), jnp.asarray(ovl, BF16))

    xf = _nsa_layer(xf, tabs, a0_norm, a0_w_in, a0_cmp_pos, a0_cmp_k_w1, a0_cmp_k_w2, a0_cmp_v_w1, a0_cmp_v_w2,
                    a0_w_out, batch=batch, seq=seq)
    xf = _nsa_layer(xf, tabs, a1_norm, a1_w_in, a1_cmp_pos, a1_cmp_k_w1, a1_cmp_k_w2, a1_cmp_v_w1, a1_cmp_v_w2,
                    a1_w_out, batch=batch, seq=seq)

    hn_kv, hn_b2 = _rmsnorm(xf, [kv_norm, b2_norm], BF16)
    kv = _matmul(hn_kv, w_kv, BF16).reshape(batch, seq, 2 * HD)
    m = np.arange(LANES)[None, :] > np.arange(LANES)[:, None]
    tri = jnp.asarray(np.concatenate([m, m], axis=1), BF16)
    xf = _sb_layer(xf, hn_b2, kv, tri, b2_w_in, b2_w_out, batch=batch, seq=seq)
    hn_b3, = _rmsnorm(xf, [b3_norm], BF16)
    xf = _sb_layer(xf, hn_b3, kv, tri, b3_w_in, b3_w_out, batch=batch, seq=seq)

    out, = _rmsnorm(xf, [final_norm], F32)
    return out.reshape(batch, seq, d)
```

```python
import functools
import math

import numpy as np
import jax
import jax.numpy as jnp
from jax import lax
from jax.experimental import pallas as pl
from jax.experimental.pallas import tpu as pltpu

N_HEADS = 32
HEAD_DIM = 128
N_KV_GROUPS = 4
CMP_BLOCK = 32
CMP_STRIDE = 16
SLC_BLOCK = 64
N_SELECT = 16
WINDOW = 512
N_BUCKETS = 32
MAX_DISTANCE = 128
RMS_EPS = 1e-6
FORCE_SCORE = 1e6

LANES = 128
NEG_BIG = -1e30
VMEM_LIMIT = 56 * 1024 * 1024
LOG2E = math.log2(math.e)

F32 = jnp.float32
BF16 = jnp.bfloat16


def _cparams(sem):
    return pltpu.CompilerParams(dimension_semantics=sem, vmem_limit_bytes=VMEM_LIMIT)


def _dot(a, b):
    return jnp.dot(a, b, preferred_element_type=F32)


def _sigmoid(x):
    return 0.5 * jnp.tanh(0.5 * x) + 0.5


def _rmsnorm_kernel(x_ref, *refs):
    n_out = len(refs) // 2
    x = x_ref[...]
    ms = jnp.mean(x * x, axis=-1, keepdims=True)
    y = x * lax.rsqrt(ms + RMS_EPS)
    for g_ref, o_ref in zip(refs[:n_out], refs[n_out:]):
        o_ref[...] = (y * g_ref[...]).astype(o_ref.dtype)


def _rmsnorm(x, gains, out_dtype, tm=512):
    m, d = x.shape
    tm = min(tm, m)
    row = pl.BlockSpec((tm, d), lambda i: (i, 0))
    outs = pl.pallas_call(
        _rmsnorm_kernel,
        grid=(m // tm,),
        in_specs=[row] + [pl.BlockSpec((1, d), lambda i: (0, 0))] * len(gains),
        out_specs=[row] * len(gains),
        out_shape=[jax.ShapeDtypeStruct((m, d), out_dtype)] * len(gains),
        compiler_params=_cparams(("parallel",)),
        name="rmsnorm",
    )(x, *[g.reshape(1, d) for g in gains])
    return outs


W_SLABS = 4


def _matmul_kernel(*refs, has_res, has_scale, cast_w, shift, w_rows):
    it = iter(refs)
    x_ref = next(it)
    w_refs = [next(it) for _ in range(W_SLABS if cast_w else 1)]
    wn_ref = next(it) if shift else None
    s_ref = next(it) if has_scale else None
    r_ref = next(it) if has_res else None
    o_ref = next(it)
    if cast_w:
        wb_ref = next(it)

        @pl.when(pl.program_id(1) == 0)
        def _():
            ks = wb_ref.shape[0] // W_SLABS
            for s, w_ref in enumerate(w_refs):
                w = w_ref[...]
                if shift:
                    ax = 0 if w_rows else 1
                    nxt = wn_ref[:, s * ks:(s + 1) * ks] if w_rows else wn_ref[s * ks:(s + 1) * ks, :]
                    w = jnp.concatenate([lax.slice_in_dim(w, shift, w.shape[ax], axis=ax),
                                         lax.slice_in_dim(nxt, 0, shift, axis=ax)], axis=ax)
                if w_rows:
                    w = w.T
                if has_scale:
                    w = w * s_ref[...]
                wb_ref[s * ks:(s + 1) * ks, :] = w.astype(BF16)
    else:
        wb_ref = w_refs[0]
    acc = _dot(x_ref[...], wb_ref[...])
    if has_res:
        acc = r_ref[...] + acc
    o_ref[...] = acc.astype(o_ref.dtype)


def _matmul(x, w, out_dtype, *, col0=0, n=None, res=None, colscale=None, w_rows=False, tm=1024, tn=512):
    m, k = x.shape
    n_total = w.shape[0] if w_rows else w.shape[1]
    n = n_total - col0 if n is None else n
    tm, tn = min(tm, m), min(tn, n)
    cast_w = w.dtype != BF16
    shift = col0 % LANES
    base = col0 - shift
    assert m % tm == 0 and n % tn == 0 and base % tn == 0 and tn % LANES == 0
    assert cast_w or (colscale is None and shift == 0 and not w_rows)
    joff = base // tn
    per = tn // LANES
    n_i = m // tm
    in_specs = [pl.BlockSpec((tm, k), lambda j, i: (i, 0))]
    args = [x]
    if cast_w:
        assert k % (W_SLABS * LANES) == 0
        ks = k // W_SLABS
        stagger = n_i > W_SLABS
        for s in range(W_SLABS):
            def tile(j, i, s=s):
                ahead = (i > s).astype(jnp.int32) if stagger else 0
                return jnp.minimum(j + ahead, n // tn - 1) + joff
            if w_rows:
                in_specs.append(pl.BlockSpec((tn, ks), lambda j, i, s=s, tile=tile: (tile(j, i), s)))
            else:
                in_specs.append(pl.BlockSpec((ks, tn), lambda j, i, s=s, tile=tile: (s, tile(j, i))))
            args.append(w)
    else:
        in_specs.append(pl.BlockSpec((k, tn), lambda j, i: (0, j + joff)))
        args.append(w)
    if shift:
        if w_rows:
            assert shift % 8 == 0
            in_specs.append(pl.BlockSpec((LANES, k), lambda j, i: ((j + joff + 1) * per, 0)))
        else:
            in_specs.append(pl.BlockSpec((k, LANES), lambda j, i: (0, (j + joff + 1) * per)))
        args.append(w)
    if colscale is not None:
        in_specs.append(pl.BlockSpec((1, tn), lambda j, i: (0, j)))
        args.append(colscale)
    if res is not None:
        in_specs.append(pl.BlockSpec((tm, tn), lambda j, i: (i, j)))
        args.append(res)
    return pl.pallas_call(
        functools.partial(_matmul_kernel, has_res=res is not None, has_scale=colscale is not None,
                          cast_w=cast_w, shift=shift, w_rows=w_rows),
        grid=(n // tn, n_i),
        in_specs=in_specs,
        out_specs=pl.BlockSpec((tm, tn), lambda j, i: (i, j)),
        out_shape=jax.ShapeDtypeStruct((m, n), out_dtype),
        scratch_shapes=[pltpu.VMEM((k, tn), BF16)] if cast_w else [],
        compiler_params=_cparams(("arbitrary", "arbitrary")),
        name="matmul_res" if res is not None else "matmul",
    )(*args)


MASKED_BUCKET = N_BUCKETS


def _t5_bucket_np(dist):
    max_exact = N_BUCKETS // 2
    d = np.maximum(dist, 0)
    log_ratio = np.log(np.maximum(d, max_exact).astype(np.float32) / np.float32(max_exact))
    large = max_exact + (log_ratio / np.float32(math.log(MAX_DISTANCE / max_exact))
                         * np.float32(N_BUCKETS - max_exact)).astype(np.int32)
    return np.where(d < max_exact, d, np.minimum(large, N_BUCKETS - 1)).astype(np.int32)


def _bias_kernel(rb_ref, idx_ref, o_ref):
    h = pl.program_id(0)
    idx = idx_ref[...]
    acc = jnp.zeros(idx.shape, F32)
    for b in range(N_BUCKETS + 1):
        acc = jnp.where(idx == b, rb_ref[h, b], acc)
    o_ref[...] = acc


def _bias_table(rel_bias_ext, bucket_idx):
    h = rel_bias_ext.shape[0]
    rows = bucket_idx.shape[0]
    return pl.pallas_call(
        _bias_kernel,
        grid=(h,),
        in_specs=[pl.BlockSpec(memory_space=pltpu.SMEM),
                  pl.BlockSpec((rows, LANES), lambda i: (0, 0))],
        out_specs=pl.BlockSpec((rows, LANES), lambda i: (0, i)),
        out_shape=jax.ShapeDtypeStruct((rows, h * LANES), F32),
        compiler_params=_cparams(("arbitrary",)),
        name="bias_table",
    )(rel_bias_ext, jnp.asarray(bucket_idx))


def _compress_kernel(kv_ref, pos_ref, kw1_ref, kw2_ref, vw1_ref, vw2_ref, kc_ref, vct_ref, *, groups):
    half = pos_ref.shape[1] // 2
    nch = kv_ref.shape[1]
    gd = groups * HEAD_DIM
    g = pl.program_id(1)
    row = lax.broadcasted_iota(jnp.int32, (nch, HEAD_DIM), 0)

    def one(col, w1_ref, w2_ref):
        x = jnp.concatenate(
            [kv_ref[0, :, pl.ds(pl.multiple_of(o * 2 * gd + col + g * HEAD_DIM, HEAD_DIM), HEAD_DIM)]
             for o in range(CMP_STRIDE)], axis=1).astype(F32)
        lo = (x + pos_ref[:, :half]).astype(BF16)
        hi = pltpu.roll(x + pos_ref[:, half:], nch - 1, 0).astype(BF16)
        h = _dot(lo, w1_ref[:half, :]) + _dot(hi, w1_ref[half:, :])
        h = h * _sigmoid(h)
        o = _dot(h.astype(BF16), w2_ref[...])
        return jnp.where(row < nch - 1, o, 0.0)

    kc_ref[0, 0] = one(0, kw1_ref, kw2_ref).astype(kc_ref.dtype)
    vct_ref[0, 0] = one(gd, vw1_ref, vw2_ref).T.astype(vct_ref.dtype)


def _compress(kv, pos, kw1, kw2, vw1, vw2, *, groups):
    b, nch, wid = kv.shape
    oblk = pl.BlockSpec((1, 1, nch, HEAD_DIM), lambda i, j: (i, j, 0, 0))

    def const(a):
        return pl.BlockSpec(a.shape, lambda i, j: (0,) * a.ndim)

    out = jax.ShapeDtypeStruct((b, groups, nch, HEAD_DIM), BF16)
    return pl.pallas_call(
        functools.partial(_compress_kernel, groups=groups),
        grid=(b, groups),
        in_specs=[pl.BlockSpec((1, nch, wid), lambda i, j: (i, 0, 0)),
                  const(pos), const(kw1), const(kw2), const(vw1), const(vw2)],
        out_specs=[oblk, oblk],
        out_shape=[out, out],
        compiler_params=_cparams(("parallel", "arbitrary")),
        name="compress",
    )(kv, pos, kw1, kw2, vw1, vw2)


def _split3(x):
    a = x.astype(BF16)
    r = x - a.astype(F32)
    b = r.astype(BF16)
    c = (r - b.astype(F32)).astype(BF16)
    return a, b, c


def _nsa_kernel(q_ref, kc_ref, vct_ref, ks_ref, vs_ref, kw_ref, vw_ref, bc_ref, bt_ref, gl_ref,
                zc_ref, zs_ref, zw_ref, ovl_ref, o_ref, sel_scr, acc_scr, s_scr, p_scr, acc2_scr, s2_scr, p2_scr,
                vst_scr, vwt_scr, gate_scr,
                *, group, n_slc, n_sel):
    R = group
    QB = LANES
    W = R * QB
    i = pl.program_id(2)
    q0 = i * QB

    @pl.when(i == 0)
    def _():
        for c in range(vs_ref.shape[1] // QB):
            cols = slice(c * QB, (c + 1) * QB)
            vst_scr[:, cols] = vs_ref[0, cols, :].T
            vwt_scr[:, cols] = vw_ref[0, cols, :].T

    q = q_ref[0]
    qt = jnp.concatenate([q[:, r * HEAD_DIM:(r + 1) * HEAD_DIM].T for r in range(R)], axis=1)

    s = _dot(kc_ref[0, 0], qt) + bc_ref[...]
    m = jnp.max(s, axis=0, keepdims=True)
    e = jnp.exp2(s - m)
    denom = jnp.maximum(jnp.sum(e, axis=0, keepdims=True), 1e-30)
    p_c = e * jnp.where(m > 0.5 * NEG_BIG, 1.0 / denom, 0.0)
    o_c = _dot(vct_ref[0, 0], p_c.astype(BF16))

    psum = p_c[:, 0:QB]
    for r in range(1, R):
        psum = psum + p_c[:, r * QB:(r + 1) * QB]
    ovl = ovl_ref[...]
    imp = sum(_dot(ovl, part) for part in _split3(psum))[:n_slc]
    j = lax.broadcasted_iota(jnp.int32, (n_slc, QB), 0)
    t = q0 + lax.broadcasted_iota(jnp.int32, (n_slc, QB), 1)
    cur = t // SLC_BLOCK
    forced = (j == 0) | (j == cur) | (j == cur - 1)
    score = jnp.where(forced, FORCE_SCORE, jnp.where(j * SLC_BLOCK <= t, imp, -1.0))
    rank = jnp.zeros((n_slc, QB), F32)
    for jp in range(n_slc):
        other = score[jp:jp + 1, :]
        beats = (other > score) | ((other == score) & (j > jp))
        rank = rank + jnp.where(beats, 1.0, 0.0)
    sel_scr[...] = jnp.where(rank < n_sel, 0.0, NEG_BIG)

    far = WINDOW // QB
    lo_w = jnp.maximum(i - far, 0)

    def tile_slice(kt):
        return pl.ds(pl.multiple_of(kt * QB, QB), QB)

    def bias(tile):
        return bt_ref[tile_slice(tile), :]

    class Chain:
        def __init__(self, k_ref, vt_ref, s_ref, p_ref, a_ref, block_mask):
            self.k_ref, self.vt_ref, self.s_ref, self.p_ref, self.a_ref = k_ref, vt_ref, s_ref, p_ref, a_ref
            self.block_mask = block_mask

        def raw_scores(self, kt):
            return _dot(self.k_ref[0, tile_slice(kt), :], qt)

        def start(self, first):
            s = self.raw_scores(i) + bias(0)
            m = jnp.max(s, axis=0, keepdims=True)
            p = jnp.exp2(s - m)
            l = jnp.sum(p, axis=0, keepdims=True)
            self.a_ref[...] = jnp.zeros_like(self.a_ref)
            self.p_ref[...] = p.astype(BF16)
            self.s_ref[...] = self.raw_scores(first)
            return m, l, jnp.ones((1, W), F32)

        def accumulate(self, kt, alpha):
            self.a_ref[...] = alpha * self.a_ref[...] + _dot(self.vt_ref[:, tile_slice(kt)], self.p_ref[...])

        def step(self, carry, prev, cur, nxt, tile):
            m, l, alpha_prev = carry
            self.accumulate(prev, alpha_prev)
            s = self.s_ref[...] + bias(tile)
            if self.block_mask:
                per = QB // SLC_BLOCK
                add = jnp.concatenate(
                    [jnp.broadcast_to(sel_scr[pl.ds(cur * per + c, 1), :], (SLC_BLOCK, QB)) for c in range(per)],
                    axis=0)
                s = s + jnp.concatenate([add] * R, axis=1)
            m_new = jnp.maximum(m, jnp.max(s, axis=0, keepdims=True))
            alpha = jnp.exp2(m - m_new)
            p = jnp.exp2(s - m_new)
            l = alpha * l + jnp.sum(p, axis=0, keepdims=True)
            self.p_ref[...] = p.astype(BF16)
            self.s_ref[...] = self.raw_scores(nxt)
            return m_new, l, alpha

        def finish(self, carry, last):
            _, l, alpha = carry
            self.accumulate(last, alpha)
            return self.a_ref[...] * (1.0 / l)

    sel = Chain(ks_ref, vst_scr, s_scr, p_scr, acc_scr, True)
    win = Chain(kw_ref, vwt_scr, s2_scr, p2_scr, acc2_scr, False)
    carry_s = sel.start(lo_w)
    carry_w = win.start(lo_w)

    def both(kt, carry):
        carry_s, carry_w = carry
        prev = jnp.where(kt == lo_w, i, kt - 1)
        d = i - kt
        carry_s = sel.step(carry_s, prev, kt, jnp.where(kt == i - 1, 0, kt + 1), jnp.minimum(d, 2))
        carry_w = win.step(carry_w, prev, kt, jnp.minimum(kt + 1, i), jnp.where(d == far, 3, jnp.minimum(d, 2)))
        return carry_s, carry_w

    carry_s, carry_w = lax.fori_loop(lo_w, i, both, (carry_s, carry_w))
    o_w = win.finish(carry_w, jnp.where(i > 0, i - 1, i))

    def sel_only(kt, carry):
        return sel.step(carry, jnp.where(kt == 0, i - 1, kt - 1), kt, jnp.minimum(kt + 1, i), 2)

    carry_s = lax.fori_loop(0, lo_w, sel_only, carry_s)
    o_s = sel.finish(carry_s, jnp.where(lo_w > 0, lo_w - 1, jnp.where(i > 0, i - 1, i)))

    n_heads = R * pl.num_programs(0)
    head0 = pl.program_id(0) * R
    gate_scr[...] = _sigmoid(gl_ref[0]).T
    for r in range(R):
        cols = slice(r * QB, (r + 1) * QB)
        mixed = jnp.zeros((QB, HEAD_DIM), F32)
        for c, (o, z_ref) in enumerate(((o_c, zc_ref), (o_s, zs_ref), (o_w, zw_ref))):
            z = z_ref[0, :, cols]
            gated = o[:, cols] * gate_scr[pl.ds(c * n_heads + head0 + r, 1), :]
            mixed = mixed + gated.T * (z * _sigmoid(z))
        o_ref[0, :, cols] = mixed.astype(o_ref.dtype)


def _nsa_attention(qkv, kc, vct, bias_c, bias_t, gl, z, ovl, *, batch, seq):
    G, R = N_KV_GROUPS, N_HEADS // N_KV_GROUPS
    QB = LANES
    W = R * QB
    nqb = seq // QB
    n_slc = seq // SLC_BLOCK
    n_sel = min(N_SELECT, n_slc)

    def kv_spec(which):
        return pl.BlockSpec((1, seq, HEAD_DIM), lambda g, b, i: (b, 0, N_HEADS + which * G + g))

    def z_spec(c):
        return pl.BlockSpec((1, QB, R * HEAD_DIM), lambda g, b, i: (b, i, c * G + g))

    cmp_spec = pl.BlockSpec((1, 1, LANES, HEAD_DIM), lambda g, b, i: (b, g, 0, 0))
    in_specs = [
        pl.BlockSpec((1, QB, R * HEAD_DIM), lambda g, b, i: (b, i, g)),
        cmp_spec, cmp_spec,
        kv_spec(2), kv_spec(3), kv_spec(4), kv_spec(5),
        pl.BlockSpec((QB, W), lambda g, b, i: (i, g)),
        pl.BlockSpec((4 * QB, W), lambda g, b, i: (0, g)),
        pl.BlockSpec((1, QB, LANES), lambda g, b, i: (b, i, 0)),
        z_spec(0), z_spec(1), z_spec(2),
        pl.BlockSpec(ovl.shape, lambda g, b, i: (0, 0)),
    ]
    return pl.pallas_call(
        functools.partial(_nsa_kernel, group=R, n_slc=n_slc, n_sel=n_sel),
        grid=(G, batch, nqb),
        in_specs=in_specs,
        out_specs=pl.BlockSpec((1, QB, R * HEAD_DIM), lambda g, b, i: (b, i, g)),
        out_shape=jax.ShapeDtypeStruct((batch, seq, N_HEADS * HEAD_DIM), BF16),
        scratch_shapes=[pltpu.VMEM((n_slc, QB), F32), pltpu.VMEM((HEAD_DIM, W), F32),
                        pltpu.VMEM((QB, W), F32), pltpu.VMEM((QB, W), BF16),
                        pltpu.VMEM((HEAD_DIM, W), F32), pltpu.VMEM((QB, W), F32), pltpu.VMEM((QB, W), BF16),
                        pltpu.VMEM((HEAD_DIM, seq), BF16), pltpu.VMEM((HEAD_DIM, seq), BF16),
                        pltpu.VMEM((LANES, QB), F32)],
        compiler_params=_cparams(("parallel", "parallel", "arbitrary")),
        name="nsa_attention",
    )(qkv, kc, vct, qkv, qkv, qkv, qkv, bias_c, bias_t, gl, z, z, z, ovl)


def _sb_kernel(q_ref, k_ref, v_ref, z_ref, tri_ref, o_ref, vt_scr, l_scr, d_scr, w_scr, acc_scr, *, tq, hp):
    i = pl.program_id(2)
    tk = tq
    tri = tri_ref[...]
    n_sub = tk // LANES
    heads = range(hp)

    @pl.when(i == 0)
    def _():
        for g in heads:
            for c in range(v_ref.shape[1] // LANES):
                cols = slice(c * LANES, (c + 1) * LANES)
                vt_scr[g, :, cols] = v_ref[0, cols, g * HEAD_DIM:(g + 1) * HEAD_DIM].T

    qts = [q_ref[0, :, g * HEAD_DIM:(g + 1) * HEAD_DIM].T for g in heads]

    def key0(t):
        return pl.multiple_of(jnp.maximum(i - t, 0) * tk, tk)

    def logits(t):
        k0 = key0(t)
        for g in heads:
            l_scr[g] = _dot(k_ref[0, pl.ds(k0, tk), g * HEAD_DIM:(g + 1) * HEAD_DIM], qts[g])

    def softplus_sums(diagonal):
        totals = []
        for g in heads:
            logit = l_scr[g]
            neg_abs = pltpu.bitcast(pltpu.bitcast(logit, jnp.uint32) | jnp.uint32(0x80000000), F32)
            sp = jnp.maximum(logit, 0.0) + jnp.log(1.0 + jnp.exp2(neg_abs)) * LOG2E
            d = logit - sp
            if diagonal:
                ok = (lax.broadcasted_iota(jnp.int32, (tk, tq), 0) < lax.broadcasted_iota(jnp.int32, (tk, tq), 1))
                sp = jnp.where(ok, sp, 0.0)
                d = jnp.where(ok, d, NEG_BIG)
            d_scr[g] = d
            hi = sp.astype(BF16)
            lo = (sp - hi.astype(F32)).astype(BF16)
            tot = []
            for c in range(n_sub):
                rows = slice(c * LANES, (c + 1) * LANES)
                within = _dot(tri, jnp.concatenate([hi[rows], lo[rows]], axis=0))
                w_scr[g, rows, :] = within
                tot.append(within[0:1] + sp[c * LANES:c * LANES + 1])
            totals.append(tuple(tot))
        return tuple(totals)

    def weigh_values(t, laters, totals):
        k0 = key0(t)
        out = []
        for g in heads:
            later = laters[g]
            parts = [None] * n_sub
            for c in reversed(range(n_sub)):
                rows = slice(c * LANES, (c + 1) * LANES)
                parts[c] = jnp.exp2(d_scr[g, rows, :] - w_scr[g, rows, :] - later)
                later = later + totals[g][c]
            a = jnp.concatenate(parts, axis=0).astype(BF16)
            acc_scr[g] += _dot(vt_scr[g, :, pl.ds(k0, tk)], a)
            out.append(later)
        return tuple(out)

    acc_scr[...] = jnp.zeros_like(acc_scr)
    logits(0)
    totals = softplus_sums(True)
    logits(1)

    def body(t, carry):
        laters, totals = carry
        laters = weigh_values(t - 1, laters, totals)
        totals = softplus_sums(False)
        logits(t + 1)
        return laters, totals

    laters, totals = lax.fori_loop(1, i + 1, body, ((jnp.zeros((1, tq), F32),) * hp, totals))
    weigh_values(i, laters, totals)
    for g in heads:
        cols = slice(g * HEAD_DIM, (g + 1) * HEAD_DIM)
        z = z_ref[0, :, cols]
        o_ref[0, :, cols] = (acc_scr[g].T * (z * _sigmoid(z))).astype(o_ref.dtype)


def _sb_attention(q, kv, z, tri, *, batch, seq, tq=512, hp=4):
    H = N_HEADS
    tq = min(tq, seq)
    wid = hp * HEAD_DIM
    return pl.pallas_call(
        functools.partial(_sb_kernel, tq=tq, hp=hp),
        grid=(batch, H // hp, seq // tq),
        in_specs=[
            pl.BlockSpec((1, tq, wid), lambda b, h, i: (b, i, h)),
            pl.BlockSpec((1, seq, wid), lambda b, h, i: (b, 0, h)),
            pl.BlockSpec((1, seq, wid), lambda b, h, i: (b, 0, H // hp + h)),
            pl.BlockSpec((1, tq, wid), lambda b, h, i: (b, i, h)),
            pl.BlockSpec(tri.shape, lambda b, h, i: (0, 0)),
        ],
        out_specs=pl.BlockSpec((1, tq, wid), lambda b, h, i: (b, i, h)),
        out_shape=jax.ShapeDtypeStruct((batch, seq, H * HEAD_DIM), BF16),
        scratch_shapes=[pltpu.VMEM((hp, HEAD_DIM, seq), BF16), pltpu.VMEM((hp, tq, tq), F32),
                        pltpu.VMEM((hp, tq, tq), F32), pltpu.VMEM((hp, tq, tq), F32),
                        pltpu.VMEM((hp, HEAD_DIM, tq), F32)],
        compiler_params=_cparams(("parallel", "parallel", "arbitrary")),
        name="sb_attention",
    )(q, kv, kv, z, tri)


def _static_tables(seq):
    QB = LANES
    nch = seq // CMP_STRIDE
    n_slc = seq // SLC_BLOCK
    assert nch == LANES and n_slc <= LANES
    q = np.arange(QB)[None, :]
    rows = np.arange(seq)[:, None]
    dist_c = (rows // QB) * QB + q - ((rows % QB) * CMP_STRIDE + CMP_BLOCK - 1)
    idx_c = np.where(dist_c >= 0, _t5_bucket_np(dist_c), MASKED_BUCKET).astype(np.int32)
    k = np.arange(QB)[:, None]
    far = WINDOW // QB
    tiles = []
    for delta, keep in ((0, k <= q), (1, None), (2, None), (far, k > q)):
        idx = _t5_bucket_np(delta * QB + q - k)
        tiles.append(idx if keep is None else np.where(keep, idx, MASKED_BUCKET))
    idx_t = np.concatenate(tiles, axis=0).astype(np.int32)
    cmp_start = np.arange(LANES) * CMP_STRIDE
    slc_start = np.arange(LANES) * SLC_BLOCK
    ovl = ((cmp_start[None, :] < slc_start[:, None] + SLC_BLOCK)
           & (cmp_start[None, :] + CMP_BLOCK - 1 >= slc_start[:, None]))
    ovl = ovl & (np.arange(LANES)[None, :] < nch - 1) & (np.arange(LANES)[:, None] < n_slc)
    return idx_c, idx_t, ovl.astype(np.float32)


def _nsa_layer(xf, tabs, norm, w_in, cmp_pos, kw1, kw2, vw1, vw2, w_out, *, batch, seq):
    bias_c, bias_t, ovl = tabs
    H, G, Dh = N_HEADS, N_KV_GROUPS, HEAD_DIM
    HD, GD = H * Dh, G * Dh
    n_qkv = HD + 6 * GD
    hn, = _rmsnorm(xf, [norm], BF16)
    qscale = jnp.concatenate([jnp.full((1, HD), LOG2E / math.sqrt(Dh), F32), jnp.ones((1, 6 * GD), F32)], axis=1)
    w_in_t = w_in.T
    qkv = _matmul(hn, w_in_t, BF16, n=n_qkv, colscale=qscale, w_rows=True)
    assert 3 * H <= LANES
    gl = _matmul(hn, w_in_t, F32, col0=n_qkv, n=LANES, w_rows=True)
    z = _matmul(hn, w_in_t, F32, col0=n_qkv + 3 * H, n=3 * HD, w_rows=True)

    nch = seq // CMP_STRIDE
    cmp_in = qkv[:, HD:HD + 2 * GD].reshape(batch, nch, CMP_STRIDE * 2 * GD)
    kc, vct = _compress(cmp_in, cmp_pos.reshape(1, CMP_BLOCK * Dh),
                        kw1.astype(BF16), kw2.astype(BF16), vw1.astype(BF16), vw2.astype(BF16), groups=G)
    mixed = _nsa_attention(qkv.reshape(batch, seq, n_qkv), kc, vct, bias_c, bias_t,
                           gl.reshape(batch, seq, LANES), z.reshape(batch, seq, 3 * HD),
                           ovl, batch=batch, seq=seq)
    return _matmul(mixed.reshape(batch * seq, HD), w_out, F32, res=xf)


def _sb_layer(xf, hn, kv, tri, w_in, w_out, *, batch, seq):
    H, Dh = N_HEADS, HEAD_DIM
    HD = H * Dh
    q = _matmul(hn, w_in, BF16, n=HD, colscale=jnp.full((1, HD), LOG2E / math.sqrt(Dh), F32))
    z = _matmul(hn, w_in, F32, col0=HD, n=HD)
    o = _sb_attention(q.reshape(batch, seq, HD), kv, z.reshape(batch, seq, HD), tri, batch=batch, seq=seq)
    return _matmul(o.reshape(batch * seq, HD), w_out, F32, res=xf)


def kernel(x, rel_bias, a0_norm, a0_w_in, a0_cmp_pos, a0_cmp_k_w1, a0_cmp_k_w2, a0_cmp_v_w1, a0_cmp_v_w2, a0_w_out, a1_norm, a1_w_in, a1_cmp_pos, a1_cmp_k_w1, a1_cmp_k_w2, a1_cmp_v_w1, a1_cmp_v_w2, a1_w_out, kv_norm, w_kv, b2_norm, b2_w_in, b2_w_out, b3_norm, b3_w_in, b3_w_out, final_norm):
    batch, seq, d = x.shape
    HD = N_HEADS * HEAD_DIM
    xf = x.reshape(batch * seq, d)

    idx_c, idx_t, ovl = _static_tables(seq)
    rel_bias_ext = jnp.concatenate([rel_bias.T * LOG2E, jnp.full((N_HEADS, 1), NEG_BIG, F32)], axis=1)
    tabs = (_bias_table(rel_bias_ext, idx_c), _bias_table(rel_bias_ext, idx_t), jnp.asarray(ovl, BF16))

    xf = _nsa_layer(xf, tabs, a0_norm, a0_w_in, a0_cmp_pos, a0_cmp_k_w1, a0_cmp_k_w2, a0_cmp_v_w1, a0_cmp_v_w2,
                    a0_w_out, batch=batch, seq=seq)
    xf = _nsa_layer(xf, tabs, a1_norm, a1_w_in, a1_cmp_pos, a1_cmp_k_w1, a1_cmp_k_w2, a1_cmp_v_w1, a1_cmp_v_w2,
                    a1_w_out, batch=batch, seq=seq)

    hn_kv, hn_b2 = _rmsnorm(xf, [kv_norm, b2_norm], BF16)
    kv = _matmul(hn_kv, w_kv, BF16).reshape(batch, seq, 2 * HD)
    m = np.arange(LANES)[None, :] > np.arange(LANES)[:, None]
    tri = jnp.asarray(np.concatenate([m, m], axis=1), BF16)
    xf = _sb_layer(xf, hn_b2, kv, tri, b2_w_in, b2_w_out, batch=batch, seq=seq)
    hn_b3, = _rmsnorm(xf, [b3_norm], BF16)
    xf = _sb_layer(xf, hn_b3, kv, tri, b3_w_in, b3_w_out, batch=batch, seq=seq)

    out, = _rmsnorm(xf, [final_norm], F32)
    return out.reshape(batch, seq, d)
```

```python
import functools
import math

import numpy as np
import jax
import jax.numpy as jnp
from jax import lax
from jax.experimental import pallas as pl
from jax.experimental.pallas import tpu as pltpu

N_HEADS = 32
HEAD_DIM = 128
N_KV_GROUPS = 4
CMP_BLOCK = 32
CMP_STRIDE = 16
SLC_BLOCK = 64
N_SELECT = 16
WINDOW = 512
N_BUCKETS = 32
MAX_DISTANCE = 128
RMS_EPS = 1e-6
FORCE_SCORE = 1e6

LANES = 128
NEG_BIG = -1e30
VMEM_LIMIT = 56 * 1024 * 1024
LOG2E = math.log2(math.e)

F32 = jnp.float32
BF16 = jnp.bfloat16


def _cparams(sem):
    return pltpu.CompilerParams(dimension_semantics=sem, vmem_limit_bytes=VMEM_LIMIT)


def _dot(a, b):
    return jnp.dot(a, b, preferred_element_type=F32)


def _sigmoid(x):
    return 0.5 * jnp.tanh(0.5 * x) + 0.5


def _silu(x):
    h = 0.5 * x
    return h + h * jnp.tanh(h)


def _rmsnorm_kernel(x_ref, *refs):
    n_out = len(refs) // 2
    x = x_ref[...]
    ms = jnp.mean(x * x, axis=-1, keepdims=True)
    y = x * lax.rsqrt(ms + RMS_EPS)
    for g_ref, o_ref in zip(refs[:n_out], refs[n_out:]):
        o_ref[...] = (y * g_ref[...]).astype(o_ref.dtype)


def _rmsnorm(x, gains, out_dtype, tm=512):
    m, d = x.shape
    tm = min(tm, m)
    row = pl.BlockSpec((tm, d), lambda i: (i, 0))
    outs = pl.pallas_call(
        _rmsnorm_kernel,
        grid=(m // tm,),
        in_specs=[row] + [pl.BlockSpec((1, d), lambda i: (0, 0))] * len(gains),
        out_specs=[row] * len(gains),
        out_shape=[jax.ShapeDtypeStruct((m, d), out_dtype)] * len(gains),
        compiler_params=_cparams(("parallel",)),
        name="rmsnorm",
    )(x, *[g.reshape(1, d) for g in gains])
    return outs


W_SLABS = 4


def _matmul_kernel(*refs, has_res, has_scale, cast_w, shift, w_rows):
    it = iter(refs)
    x_ref = next(it)
    w_refs = [next(it) for _ in range(W_SLABS if cast_w else 1)]
    wn_ref = next(it) if shift else None
    s_ref = next(it) if has_scale else None
    r_ref = next(it) if has_res else None
    o_ref = next(it)
    if cast_w:
        wb_ref = next(it)

        @pl.when(pl.program_id(1) == 0)
        def _():
            ks = wb_ref.shape[0] // W_SLABS
            for s, w_ref in enumerate(w_refs):
                w = w_ref[...]
                if shift:
                    ax = 0 if w_rows else 1
                    nxt = wn_ref[:, s * ks:(s + 1) * ks] if w_rows else wn_ref[s * ks:(s + 1) * ks, :]
                    w = jnp.concatenate([lax.slice_in_dim(w, shift, w.shape[ax], axis=ax),
                                         lax.slice_in_dim(nxt, 0, shift, axis=ax)], axis=ax)
                if w_rows:
                    w = w.T
                if has_scale:
                    w = w * s_ref[...]
                wb_ref[s * ks:(s + 1) * ks, :] = w.astype(BF16)
    else:
        wb_ref = w_refs[0]
    acc = _dot(x_ref[...], wb_ref[...])
    if has_res:
        acc = r_ref[...] + acc
    o_ref[...] = acc.astype(o_ref.dtype)


def _matmul(x, w, out_dtype, *, col0=0, n=None, res=None, colscale=None, w_rows=False, tm=1024, tn=512):
    m, k = x.shape
    n_total = w.shape[0] if w_rows else w.shape[1]
    n = n_total - col0 if n is None else n
    tm, tn = min(tm, m), min(tn, n)
    cast_w = w.dtype != BF16
    shift = col0 % LANES
    base = col0 - shift
    assert m % tm == 0 and n % tn == 0 and base % tn == 0 and tn % LANES == 0
    assert cast_w or (colscale is None and shift == 0 and not w_rows)
    joff = base // tn
    per = tn // LANES
    n_i = m // tm
    in_specs = [pl.BlockSpec((tm, k), lambda j, i: (i, 0))]
    args = [x]
    if cast_w:
        assert k % (W_SLABS * LANES) == 0
        ks = k // W_SLABS
        stagger = n_i > W_SLABS
        for s in range(W_SLABS):
            def tile(j, i, s=s):
                ahead = (i > s).astype(jnp.int32) if stagger else 0
                return jnp.minimum(j + ahead, n // tn - 1) + joff
            if w_rows:
                in_specs.append(pl.BlockSpec((tn, ks), lambda j, i, s=s, tile=tile: (tile(j, i), s)))
            else:
                in_specs.append(pl.BlockSpec((ks, tn), lambda j, i, s=s, tile=tile: (s, tile(j, i))))
            args.append(w)
    else:
        in_specs.append(pl.BlockSpec((k, tn), lambda j, i: (0, j + joff)))
        args.append(w)
    if shift:
        if w_rows:
            assert shift % 8 == 0
            in_specs.append(pl.BlockSpec((LANES, k), lambda j, i: ((j + joff + 1) * per, 0)))
        else:
            in_specs.append(pl.BlockSpec((k, LANES), lambda j, i: (0, (j + joff + 1) * per)))
        args.append(w)
    if colscale is not None:
        in_specs.append(pl.BlockSpec((1, tn), lambda j, i: (0, j)))
        args.append(colscale)
    if res is not None:
        in_specs.append(pl.BlockSpec((tm, tn), lambda j, i: (i, j)))
        args.append(res)
    return pl.pallas_call(
        functools.partial(_matmul_kernel, has_res=res is not None, has_scale=colscale is not None,
                          cast_w=cast_w, shift=shift, w_rows=w_rows),
        grid=(n // tn, n_i),
        in_specs=in_specs,
        out_specs=pl.BlockSpec((tm, tn), lambda j, i: (i, j)),
        out_shape=jax.ShapeDtypeStruct((m, n), out_dtype),
        scratch_shapes=[pltpu.VMEM((k, tn), BF16)] if cast_w else [],
        compiler_params=_cparams(("arbitrary", "arbitrary")),
        name="matmul_res" if res is not None else "matmul",
    )(*args)


MASKED_BUCKET = N_BUCKETS


def _t5_bucket_np(dist):
    max_exact = N_BUCKETS // 2
    d = np.maximum(dist, 0)
    log_ratio = np.log(np.maximum(d, max_exact).astype(np.float32) / np.float32(max_exact))
    large = max_exact + (log_ratio / np.float32(math.log(MAX_DISTANCE / max_exact))
                         * np.float32(N_BUCKETS - max_exact)).astype(np.int32)
    return np.where(d < max_exact, d, np.minimum(large, N_BUCKETS - 1)).astype(np.int32)


def _bias_kernel(rb_ref, idx_ref, o_ref):
    h = pl.program_id(0)
    idx = idx_ref[...]
    acc = jnp.zeros(idx.shape, F32)
    for b in range(N_BUCKETS + 1):
        acc = jnp.where(idx == b, rb_ref[h, b], acc)
    o_ref[...] = acc


def _bias_table(rel_bias_ext, bucket_idx):
    h = rel_bias_ext.shape[0]
    rows = bucket_idx.shape[0]
    return pl.pallas_call(
        _bias_kernel,
        grid=(h,),
        in_specs=[pl.BlockSpec(memory_space=pltpu.SMEM),
                  pl.BlockSpec((rows, LANES), lambda i: (0, 0))],
        out_specs=pl.BlockSpec((rows, LANES), lambda i: (0, i)),
        out_shape=jax.ShapeDtypeStruct((rows, h * LANES), F32),
        compiler_params=_cparams(("arbitrary",)),
        name="bias_table",
    )(rel_bias_ext, jnp.asarray(bucket_idx))


def _compress_kernel(kv_ref, pos_ref, kw1_ref, kw2_ref, vw1_ref, vw2_ref, kc_ref, vct_ref, *, groups):
    half = pos_ref.shape[1] // 2
    nch = kv_ref.shape[1]
    gd = groups * HEAD_DIM
    g = pl.program_id(1)
    row = lax.broadcasted_iota(jnp.int32, (nch, HEAD_DIM), 0)

    def one(col, w1_ref, w2_ref):
        x = jnp.concatenate(
            [kv_ref[0, :, pl.ds(pl.multiple_of(o * 2 * gd + col + g * HEAD_DIM, HEAD_DIM), HEAD_DIM)]
             for o in range(CMP_STRIDE)], axis=1).astype(F32)
        lo = (x + pos_ref[:, :half]).astype(BF16)
        hi = pltpu.roll(x + pos_ref[:, half:], nch - 1, 0).astype(BF16)
        h = _dot(lo, w1_ref[:half, :]) + _dot(hi, w1_ref[half:, :])
        h = _silu(h)
        o = _dot(h.astype(BF16), w2_ref[...])
        return jnp.where(row < nch - 1, o, 0.0)

    kc_ref[0, 0] = one(0, kw1_ref, kw2_ref).astype(kc_ref.dtype)
    vct_ref[0, 0] = one(gd, vw1_ref, vw2_ref).T.astype(vct_ref.dtype)


def _compress(kv, pos, kw1, kw2, vw1, vw2, *, groups):
    b, nch, wid = kv.shape
    oblk = pl.BlockSpec((1, 1, nch, HEAD_DIM), lambda i, j: (i, j, 0, 0))

    def const(a):
        return pl.BlockSpec(a.shape, lambda i, j: (0,) * a.ndim)

    out = jax.ShapeDtypeStruct((b, groups, nch, HEAD_DIM), BF16)
    return pl.pallas_call(
        functools.partial(_compress_kernel, groups=groups),
        grid=(b, groups),
        in_specs=[pl.BlockSpec((1, nch, wid), lambda i, j: (i, 0, 0)),
                  const(pos), const(kw1), const(kw2), const(vw1), const(vw2)],
        out_specs=[oblk, oblk],
        out_shape=[out, out],
        compiler_params=_cparams(("parallel", "arbitrary")),
        name="compress",
    )(kv, pos, kw1, kw2, vw1, vw2)


def _split3(x):
    a = x.astype(BF16)
    r = x - a.astype(F32)
    b = r.astype(BF16)
    c = (r - b.astype(F32)).astype(BF16)
    return a, b, c


def _nsa_kernel(q_ref, kc_ref, vct_ref, ks_ref, vs_ref, kw_ref, vw_ref, bc_ref, bt_ref, gl_ref,
                zc_ref, zs_ref, zw_ref, ovl_ref, o_ref, sel_scr, acc_scr, s_scr, p_scr, acc2_scr, s2_scr, p2_scr,
                vst_scr, vwt_scr, gate_scr, mix_scr,
                *, group, n_slc, n_sel):
    R = group
    QB = LANES
    W = R * QB
    i = pl.program_id(2)
    q0 = i * QB

    @pl.when(i == 0)
    def _():
        for c in range(vs_ref.shape[1] // QB):
            cols = slice(c * QB, (c + 1) * QB)
            vst_scr[:, cols] = vs_ref[0, cols, :].T
            vwt_scr[:, cols] = vw_ref[0, cols, :].T

    q = q_ref[0]
    qt = jnp.concatenate([q[:, r * HEAD_DIM:(r + 1) * HEAD_DIM].T for r in range(R)], axis=1)

    n_heads = R * pl.num_programs(0)
    head0 = pl.program_id(0) * R
    gate_scr[...] = _sigmoid(gl_ref[0]).T

    def mix_in(c, o, inv, z_ref, first=False, last=False):
        for r in range(R):
            cols = slice(r * QB, (r + 1) * QB)
            scale = gate_scr[pl.ds(c * n_heads + head0 + r, 1), :]
            if inv is not None:
                scale = scale * inv[:, cols]
            term = (o[:, cols] * scale).T * _silu(z_ref[0, :, cols])
            if not first:
                term = mix_scr[:, cols] + term
            if last:
                o_ref[0, :, cols] = term.astype(o_ref.dtype)
            else:
                mix_scr[:, cols] = term

    s = _dot(kc_ref[0, 0], qt) + bc_ref[...]
    m = jnp.max(s, axis=0, keepdims=True)
    e = jnp.exp2(s - m)
    denom = jnp.maximum(jnp.sum(e, axis=0, keepdims=True), 1e-30)
    p_c = e * jnp.where(m > 0.5 * NEG_BIG, 1.0 / denom, 0.0)
    mix_in(0, _dot(vct_ref[0, 0], p_c.astype(BF16)), None, zc_ref, first=True)

    psum = p_c[:, 0:QB]
    for r in range(1, R):
        psum = psum + p_c[:, r * QB:(r + 1) * QB]
    ovl = ovl_ref[...]
    imp = sum(_dot(ovl, part) for part in _split3(psum))[:n_slc]
    j = lax.broadcasted_iota(jnp.int32, (n_slc, QB), 0)
    t = q0 + lax.broadcasted_iota(jnp.int32, (n_slc, QB), 1)
    cur = t // SLC_BLOCK
    forced = (j == 0) | (j == cur) | (j == cur - 1)
    score = jnp.where(forced, FORCE_SCORE, jnp.where(j * SLC_BLOCK <= t, imp, -1.0))
    rank = jnp.zeros((n_slc, QB), F32)
    for jp in range(n_slc):
        other = score[jp:jp + 1, :]
        beats = (other > score) | ((other == score) & (j > jp))
        rank = rank + jnp.where(beats, 1.0, 0.0)
    sel_scr[...] = jnp.where(rank < n_sel, 0.0, NEG_BIG)

    far = WINDOW // QB
    lo_w = jnp.maximum(i - far, 0)

    def tile_slice(kt):
        return pl.ds(pl.multiple_of(kt * QB, QB), QB)

    def bias(tile):
        return bt_ref[tile_slice(tile), :]

    class Chain:
        def __init__(self, k_ref, vt_ref, s_ref, p_ref, a_ref, block_mask):
            self.k_ref, self.vt_ref, self.s_ref, self.p_ref, self.a_ref = k_ref, vt_ref, s_ref, p_ref, a_ref
            self.block_mask = block_mask

        def raw_scores(self, kt):
            return _dot(self.k_ref[0, tile_slice(kt), :], qt)

        def start(self, first):
            s = self.raw_scores(i) + bias(0)
            m = jnp.max(s, axis=0, keepdims=True)
            p = jnp.exp2(s - m)
            l = jnp.sum(p, axis=0, keepdims=True)
            self.a_ref[...] = jnp.zeros_like(self.a_ref)
            self.p_ref[...] = p.astype(BF16)
            self.s_ref[...] = self.raw_scores(first)
            return m, l, jnp.ones((1, W), F32)

        def accumulate(self, kt, alpha):
            self.a_ref[...] = alpha * self.a_ref[...] + _dot(self.vt_ref[:, tile_slice(kt)], self.p_ref[...])

        def step(self, carry, prev, cur, nxt, tile):
            m, l, alpha_prev = carry
            self.accumulate(prev, alpha_prev)
            s = self.s_ref[...] + bias(tile)
            if self.block_mask:
                per = QB // SLC_BLOCK
                add = jnp.concatenate(
                    [jnp.broadcast_to(sel_scr[pl.ds(cur * per + c, 1), :], (SLC_BLOCK, QB)) for c in range(per)],
                    axis=0)
                s = s + jnp.concatenate([add] * R, axis=1)
            m_new = jnp.maximum(m, jnp.max(s, axis=0, keepdims=True))
            alpha = jnp.exp2(m - m_new)
            p = jnp.exp2(s - m_new)
            l = alpha * l + jnp.sum(p, axis=0, keepdims=True)
            self.p_ref[...] = p.astype(BF16)
            self.s_ref[...] = self.raw_scores(nxt)
            return m_new, l, alpha

        def finish(self, carry, last):
            _, l, alpha = carry
            out = alpha * self.a_ref[...] + _dot(self.vt_ref[:, tile_slice(last)], self.p_ref[...])
            return out, 1.0 / l

    sel = Chain(ks_ref, vst_scr, s_scr, p_scr, acc_scr, True)
    win = Chain(kw_ref, vwt_scr, s2_scr, p2_scr, acc2_scr, False)
    carry_s = sel.start(lo_w)
    carry_w = win.start(lo_w)

    def both(kt, carry):
        carry_s, carry_w = carry
        prev = jnp.where(kt == lo_w, i, kt - 1)
        d = i - kt
        carry_s = sel.step(carry_s, prev, kt, jnp.where(kt == i - 1, 0, kt + 1), jnp.minimum(d, 2))
        carry_w = win.step(carry_w, prev, kt, jnp.minimum(kt + 1, i), jnp.where(d == far, 3, jnp.minimum(d, 2)))
        return carry_s, carry_w

    carry_s, carry_w = lax.fori_loop(lo_w, i, both, (carry_s, carry_w))
    mix_in(2, *win.finish(carry_w, jnp.where(i > 0, i - 1, i)), zw_ref)

    def sel_only(kt, carry):
        return sel.step(carry, jnp.where(kt == 0, i - 1, kt - 1), kt, jnp.minimum(kt + 1, i), 2)

    carry_s = lax.fori_loop(0, lo_w, sel_only, carry_s)
    mix_in(1, *sel.finish(carry_s, jnp.where(lo_w > 0, lo_w - 1, jnp.where(i > 0, i - 1, i))), zs_ref, last=True)


def _nsa_attention(qkv, kc, vct, bias_c, bias_t, gl, z, ovl, *, batch, seq):
    G, R = N_KV_GROUPS, N_HEADS // N_KV_GROUPS
    QB = LANES
    W = R * QB
    nqb = seq // QB
    n_slc = seq // SLC_BLOCK
    n_sel = min(N_SELECT, n_slc)

    def kv_spec(which):
        return pl.BlockSpec((1, seq, HEAD_DIM), lambda g, b, i: (b, 0, N_HEADS + which * G + g))

    def z_spec(c):
        return pl.BlockSpec((1, QB, R * HEAD_DIM), lambda g, b, i: (b, i, c * G + g))

    cmp_spec = pl.BlockSpec((1, 1, LANES, HEAD_DIM), lambda g, b, i: (b, g, 0, 0))
    in_specs = [
        pl.BlockSpec((1, QB, R * HEAD_DIM), lambda g, b, i: (b, i, g)),
        cmp_spec, cmp_spec,
        kv_spec(2), kv_spec(3), kv_spec(4), kv_spec(5),
        pl.BlockSpec((QB, W), lambda g, b, i: (i, g)),
        pl.BlockSpec((4 * QB, W), lambda g, b, i: (0, g)),
        pl.BlockSpec((1, QB, LANES), lambda g, b, i: (b, i, 0)),
        z_spec(0), z_spec(1), z_spec(2),
        pl.BlockSpec(ovl.shape, lambda g, b, i: (0, 0)),
    ]
    return pl.pallas_call(
        functools.partial(_nsa_kernel, group=R, n_slc=n_slc, n_sel=n_sel),
        grid=(G, batch, nqb),
        in_specs=in_specs,
        out_specs=pl.BlockSpec((1, QB, R * HEAD_DIM), lambda g, b, i: (b, i, g)),
        out_shape=jax.ShapeDtypeStruct((batch, seq, N_HEADS * HEAD_DIM), BF16),
        scratch_shapes=[pltpu.VMEM((n_slc, QB), F32), pltpu.VMEM((HEAD_DIM, W), F32),
                        pltpu.VMEM((QB, W), F32), pltpu.VMEM((QB, W), BF16),
                        pltpu.VMEM((HEAD_DIM, W), F32), pltpu.VMEM((QB, W), F32), pltpu.VMEM((QB, W), BF16),
                        pltpu.VMEM((HEAD_DIM, seq), BF16), pltpu.VMEM((HEAD_DIM, seq), BF16),
                        pltpu.VMEM((LANES, QB), F32), pltpu.VMEM((QB, R * HEAD_DIM), F32)],
        compiler_params=_cparams(("parallel", "parallel", "arbitrary")),
        name="nsa_attention",
    )(qkv, kc, vct, qkv, qkv, qkv, qkv, bias_c, bias_t, gl, z, z, z, ovl)


def _sb_kernel(q_ref, k_ref, v_ref, z_ref, tri_ref, o_ref, vt_scr, l_scr, d_scr, w_scr, acc_scr, *, tq, hp):
    i = pl.program_id(2)
    tk = tq
    tri = tri_ref[...]
    n_sub = tk // LANES
    heads = range(hp)

    @pl.when(i == 0)
    def _():
        for g in heads:
            for c in range(v_ref.shape[1] // LANES):
                cols = slice(c * LANES, (c + 1) * LANES)
                vt_scr[g, :, cols] = v_ref[0, cols, g * HEAD_DIM:(g + 1) * HEAD_DIM].T

    qts = [q_ref[0, :, g * HEAD_DIM:(g + 1) * HEAD_DIM].T for g in heads]

    def key0(t):
        return pl.multiple_of(jnp.maximum(i - t, 0) * tk, tk)

    def logits(t):
        k0 = key0(t)
        for g in heads:
            l_scr[g] = _dot(k_ref[0, pl.ds(k0, tk), g * HEAD_DIM:(g + 1) * HEAD_DIM], qts[g])

    def softplus_sums(diagonal):
        totals = []
        for g in heads:
            logit = l_scr[g]
            neg_abs = pltpu.bitcast(pltpu.bitcast(logit, jnp.uint32) | jnp.uint32(0x80000000), F32)
            sp = jnp.maximum(logit, 0.0) + jnp.log(1.0 + jnp.exp2(neg_abs)) * LOG2E
            d = logit - sp
            if diagonal:
                ok = (lax.broadcasted_iota(jnp.int32, (tk, tq), 0) < lax.broadcasted_iota(jnp.int32, (tk, tq), 1))
                sp = jnp.where(ok, sp, 0.0)
                d = jnp.where(ok, d, NEG_BIG)
            d_scr[g] = d
            hi = sp.astype(BF16)
            lo = (sp - hi.astype(F32)).astype(BF16)
            tot = []
            for c in range(n_sub):
                rows = slice(c * LANES, (c + 1) * LANES)
                within = _dot(tri, jnp.concatenate([hi[rows], lo[rows]], axis=0))
                w_scr[g, rows, :] = within
                tot.append(within[0:1] + sp[c * LANES:c * LANES + 1])
            totals.append(tuple(tot))
        return tuple(totals)

    def weigh_values(t, laters, totals):
        k0 = key0(t)
        out = []
        for g in heads:
            later = laters[g]
            parts = [None] * n_sub
            for c in reversed(range(n_sub)):
                rows = slice(c * LANES, (c + 1) * LANES)
                parts[c] = jnp.exp2(d_scr[g, rows, :] - w_scr[g, rows, :] - later)
                later = later + totals[g][c]
            a = jnp.concatenate(parts, axis=0).astype(BF16)
            acc_scr[g] += _dot(vt_scr[g, :, pl.ds(k0, tk)], a)
            out.append(later)
        return tuple(out)

    acc_scr[...] = jnp.zeros_like(acc_scr)
    logits(0)
    totals = softplus_sums(True)
    logits(1)

    def body(t, carry):
        laters, totals = carry
        laters = weigh_values(t - 1, laters, totals)
        totals = softplus_sums(False)
        logits(t + 1)
        return laters, totals

    laters, totals = lax.fori_loop(1, i + 1, body, ((jnp.zeros((1, tq), F32),) * hp, totals))
    weigh_values(i, laters, totals)
    for g in heads:
        cols = slice(g * HEAD_DIM, (g + 1) * HEAD_DIM)
        z = z_ref[0, :, cols]
        o_ref[0, :, cols] = (acc_scr[g].T * _silu(z)).astype(o_ref.dtype)


def _sb_attention(q, kv, z, tri, *, batch, seq, tq=512, hp=4):
    H = N_HEADS
    tq = min(tq, seq)
    wid = hp * HEAD_DIM
    return pl.pallas_call(
        functools.partial(_sb_kernel, tq=tq, hp=hp),
        grid=(batch, H // hp, seq // tq),
        in_specs=[
            pl.BlockSpec((1, tq, wid), lambda b, h, i: (b, i, h)),
            pl.BlockSpec((1, seq, wid), lambda b, h, i: (b, 0, h)),
            pl.BlockSpec((1, seq, wid), lambda b, h, i: (b, 0, H // hp + h)),
            pl.BlockSpec((1, tq, wid), lambda b, h, i: (b, i, h)),
            pl.BlockSpec(tri.shape, lambda b, h, i: (0, 0)),
        ],
        out_specs=pl.BlockSpec((1, tq, wid), lambda b, h, i: (b, i, h)),
        out_shape=jax.ShapeDtypeStruct((batch, seq, H * HEAD_DIM), BF16),
        scratch_shapes=[pltpu.VMEM((hp, HEAD_DIM, seq), BF16), pltpu.VMEM((hp, tq, tq), F32),
                        pltpu.VMEM((hp, tq, tq), F32), pltpu.VMEM((hp, tq, tq), F32),
                        pltpu.VMEM((hp, HEAD_DIM, tq), F32)],
        compiler_params=_cparams(("parallel", "parallel", "arbitrary")),
        name="sb_attention",
    )(q, kv, kv, z, tri)


def _static_tables(seq):
    QB = LANES
    nch = seq // CMP_STRIDE
    n_slc = seq // SLC_BLOCK
    assert nch == LANES and n_slc <= LANES
    q = np.arange(QB)[None, :]
    rows = np.arange(seq)[:, None]
    dist_c = (rows // QB) * QB + q - ((rows % QB) * CMP_STRIDE + CMP_BLOCK - 1)
    idx_c = np.where(dist_c >= 0, _t5_bucket_np(dist_c), MASKED_BUCKET).astype(np.int32)
    k = np.arange(QB)[:, None]
    far = WINDOW // QB
    tiles = []
    for delta, keep in ((0, k <= q), (1, None), (2, None), (far, k > q)):
        idx = _t5_bucket_np(delta * QB + q - k)
        tiles.append(idx if keep is None else np.where(keep, idx, MASKED_BUCKET))
    idx_t = np.concatenate(tiles, axis=0).astype(np.int32)
    cmp_start = np.arange(LANES) * CMP_STRIDE
    slc_start = np.arange(LANES) * SLC_BLOCK
    ovl = ((cmp_start[None, :] < slc_start[:, None] + SLC_BLOCK)
           & (cmp_start[None, :] + CMP_BLOCK - 1 >= slc_start[:, None]))
    ovl = ovl & (np.arange(LANES)[None, :] < nch - 1) & (np.arange(LANES)[:, None] < n_slc)
    return idx_c, idx_t, ovl.astype(np.float32)


def _nsa_layer(xf, tabs, norm, w_in, cmp_pos, kw1, kw2, vw1, vw2, w_out, *, batch, seq):
    bias_c, bias_t, ovl = tabs
    H, G, Dh = N_HEADS, N_KV_GROUPS, HEAD_DIM
    HD, GD = H * Dh, G * Dh
    n_qkv = HD + 6 * GD
    hn, = _rmsnorm(xf, [norm], BF16)
    qscale = jnp.concatenate([jnp.full((1, HD), LOG2E / math.sqrt(Dh), F32), jnp.ones((1, 6 * GD), F32)], axis=1)
    w_in_t = w_in.T
    qkv = _matmul(hn, w_in_t, BF16, n=n_qkv, colscale=qscale, w_rows=True)
    assert 3 * H <= LANES
    gl = _matmul(hn, w_in_t, F32, col0=n_qkv, n=LANES, w_rows=True)
    z = _matmul(hn, w_in_t, F32, col0=n_qkv + 3 * H, n=3 * HD, w_rows=True)

    nch = seq // CMP_STRIDE
    cmp_in = qkv[:, HD:HD + 2 * GD].reshape(batch, nch, CMP_STRIDE * 2 * GD)
    kc, vct = _compress(cmp_in, cmp_pos.reshape(1, CMP_BLOCK * Dh),
                        kw1.astype(BF16), kw2.astype(BF16), vw1.astype(BF16), vw2.astype(BF16), groups=G)
    mixed = _nsa_attention(qkv.reshape(batch, seq, n_qkv), kc, vct, bias_c, bias_t,
                           gl.reshape(batch, seq, LANES), z.reshape(batch, seq, 3 * HD),
                           ovl, batch=batch, seq=seq)
    return _matmul(mixed.reshape(batch * seq, HD), w_out, F32, res=xf)


def _sb_layer(xf, hn, kv, tri, w_in, w_out, *, batch, seq):
    H, Dh = N_HEADS, HEAD_DIM
    HD = H * Dh
    q = _matmul(hn, w_in, BF16, n=HD, colscale=jnp.full((1, HD), LOG2E / math.sqrt(Dh), F32))
    z = _matmul(hn, w_in, F32, col0=HD, n=HD)
    o = _sb_attention(q.reshape(batch, seq, HD), kv, z.reshape(batch, seq, HD), tri, batch=batch, seq=seq)
    return _matmul(o.reshape(batch * seq, HD), w_out, F32, res=xf)


def kernel(x, rel_bias, a0_norm, a0_w_in, a0_cmp_pos, a0_cmp_k_w1, a0_cmp_k_w2, a0_cmp_v_w1, a0_cmp_v_w2, a0_w_out, a1_norm, a1_w_in, a1_cmp_pos, a1_cmp_k_w1, a1_cmp_k_w2, a1_cmp_v_w1, a1_cmp_v_w2, a1_w_out, kv_norm, w_kv, b2_norm, b2_w_in, b2_w_out, b3_norm, b3_w_in, b3_w_out, final_norm):
    batch, seq, d = x.shape
    HD = N_HEADS * HEAD_DIM
    xf = x.reshape(batch * seq, d)

    idx_c, idx_t, ovl = _static_tables(seq)
    rel_bias_ext = jnp.concatenate([rel_bias.T * LOG2E, jnp.full((N_HEADS, 1), NEG_BIG, F32)], axis=1)
    tabs = (_bias_table(rel_bias_ext, idx_c), _bias_table(rel_bias_ext, idx_t), jnp.asarray(ovl, BF16))

    xf = _nsa_layer(xf, tabs, a0_norm, a0_w_in, a0_cmp_pos, a0_cmp_k_w1, a0_cmp_k_w2, a0_cmp_v_w1, a0_cmp_v_w2,
                    a0_w_out, batch=batch, seq=seq)
    xf = _nsa_layer(xf, tabs, a1_norm, a1_w_in, a1_cmp_pos, a1_cmp_k_w1, a1_cmp_k_w2, a1_cmp_v_w1, a1_cmp_v_w2,
                    a1_w_out, batch=batch, seq=seq)

    hn_kv, hn_b2 = _rmsnorm(xf, [kv_norm, b2_norm], BF16)
    kv = _matmul(hn_kv, w_kv, BF16).reshape(batch, seq, 2 * HD)
    m = np.arange(LANES)[None, :] > np.arange(LANES)[:, None]
    tri = jnp.asarray(np.concatenate([m, m], axis=1), BF16)
    xf = _sb_layer(xf, hn_b2, kv, tri, b2_w_in, b2_w_out, batch=batch, seq=seq)
    hn_b3, = _rmsnorm(xf, [b3_norm], BF16)
    xf = _sb_layer(xf, hn_b3, kv, tri, b3_w_in, b3_w_out, batch=batch, seq=seq)

    out, = _rmsnorm(xf, [final_norm], F32)
    return out.reshape(batch, seq, d)
```

```python
import functools
import math

import numpy as np
import jax
import jax.numpy as jnp
from jax import lax
from jax.experimental import pallas as pl
from jax.experimental.pallas import tpu as pltpu

N_HEADS = 32
HEAD_DIM = 128
N_KV_GROUPS = 4
CMP_BLOCK = 32
CMP_STRIDE = 16
SLC_BLOCK = 64
N_SELECT = 16
WINDOW = 512
N_BUCKETS = 32
MAX_DISTANCE = 128
RMS_EPS = 1e-6
FORCE_SCORE = 1e6

LANES = 128
NEG_BIG = -1e30
VMEM_LIMIT = 56 * 1024 * 1024
LOG2E = math.log2(math.e)

F32 = jnp.float32
BF16 = jnp.bfloat16


def _cparams(sem):
    return pltpu.CompilerParams(dimension_semantics=sem, vmem_limit_bytes=VMEM_LIMIT)


def _dot(a, b):
    return jnp.dot(a, b, preferred_element_type=F32)


def _sigmoid(x):
    return 0.5 * jnp.tanh(0.5 * x) + 0.5


def _rmsnorm_kernel(x_ref, *refs):
    n_out = len(refs) // 2
    x = x_ref[...]
    ms = jnp.mean(x * x, axis=-1, keepdims=True)
    y = x * lax.rsqrt(ms + RMS_EPS)
    for g_ref, o_ref in zip(refs[:n_out], refs[n_out:]):
        o_ref[...] = (y * g_ref[...]).astype(o_ref.dtype)


def _rmsnorm(x, gains, out_dtype, tm=512):
    m, d = x.shape
    tm = min(tm, m)
    row = pl.BlockSpec((tm, d), lambda i: (i, 0))
    outs = pl.pallas_call(
        _rmsnorm_kernel,
        grid=(m // tm,),
        in_specs=[row] + [pl.BlockSpec((1, d), lambda i: (0, 0))] * len(gains),
        out_specs=[row] * len(gains),
        out_shape=[jax.ShapeDtypeStruct((m, d), out_dtype)] * len(gains),
        compiler_params=_cparams(("parallel",)),
        name="rmsnorm",
    )(x, *[g.reshape(1, d) for g in gains])
    return outs


W_SLABS = 4


def _matmul_kernel(*refs, has_res, has_scale, cast_w, shift, w_rows):
    it = iter(refs)
    x_ref = next(it)
    w_refs = [next(it) for _ in range(W_SLABS if cast_w else 1)]
    wn_ref = next(it) if shift else None
    s_ref = next(it) if has_scale else None
    r_ref = next(it) if has_res else None
    o_ref = next(it)
    if cast_w:
        wb_ref = next(it)

        @pl.when(pl.program_id(1) == 0)
        def _():
            ks = wb_ref.shape[0] // W_SLABS
            for s, w_ref in enumerate(w_refs):
                w = w_ref[...]
                if shift:
                    ax = 0 if w_rows else 1
                    nxt = wn_ref[:, s * ks:(s + 1) * ks] if w_rows else wn_ref[s * ks:(s + 1) * ks, :]
                    w = jnp.concatenate([lax.slice_in_dim(w, shift, w.shape[ax], axis=ax),
                                         lax.slice_in_dim(nxt, 0, shift, axis=ax)], axis=ax)
                if w_rows:
                    w = w.T
                if has_scale:
                    w = w * s_ref[...]
                wb_ref[s * ks:(s + 1) * ks, :] = w.astype(BF16)
    else:
        wb_ref = w_refs[0]
    acc = _dot(x_ref[...], wb_ref[...])
    if has_res:
        acc = r_ref[...] + acc
    o_ref[...] = acc.astype(o_ref.dtype)


def _matmul(x, w, out_dtype, *, col0=0, n=None, res=None, colscale=None, w_rows=False, tm=1024, tn=512):
    m, k = x.shape
    n_total = w.shape[0] if w_rows else w.shape[1]
    n = n_total - col0 if n is None else n
    tm, tn = min(tm, m), min(tn, n)
    cast_w = w.dtype != BF16
    shift = col0 % LANES
    base = col0 - shift
    assert m % tm == 0 and n % tn == 0 and base % tn == 0 and tn % LANES == 0
    assert cast_w or (colscale is None and shift == 0 and not w_rows)
    joff = base // tn
    per = tn // LANES
    n_i = m // tm
    in_specs = [pl.BlockSpec((tm, k), lambda j, i: (i, 0))]
    args = [x]
    if cast_w:
        assert k % (W_SLABS * LANES) == 0
        ks = k // W_SLABS
        stagger = n_i > W_SLABS
        for s in range(W_SLABS):
            def tile(j, i, s=s):
                ahead = (i > s).astype(jnp.int32) if stagger else 0
                return jnp.minimum(j + ahead, n // tn - 1) + joff
            if w_rows:
                in_specs.append(pl.BlockSpec((tn, ks), lambda j, i, s=s, tile=tile: (tile(j, i), s)))
            else:
                in_specs.append(pl.BlockSpec((ks, tn), lambda j, i, s=s, tile=tile: (s, tile(j, i))))
            args.append(w)
    else:
        in_specs.append(pl.BlockSpec((k, tn), lambda j, i: (0, j + joff)))
        args.append(w)
    if shift:
        if w_rows:
            assert shift % 8 == 0
            in_specs.append(pl.BlockSpec((LANES, k), lambda j, i: ((j + joff + 1) * per, 0)))
        else:
            in_specs.append(pl.BlockSpec((k, LANES), lambda j, i: (0, (j + joff + 1) * per)))
        args.append(w)
    if colscale is not None:
        in_specs.append(pl.BlockSpec((1, tn), lambda j, i: (0, j)))
        args.append(colscale)
    if res is not None:
        in_specs.append(pl.BlockSpec((tm, tn), lambda j, i: (i, j)))
        args.append(res)
    return pl.pallas_call(
        functools.partial(_matmul_kernel, has_res=res is not None, has_scale=colscale is not None,
                          cast_w=cast_w, shift=shift, w_rows=w_rows),
        grid=(n // tn, n_i),
        in_specs=in_specs,
        out_specs=pl.BlockSpec((tm, tn), lambda j, i: (i, j)),
        out_shape=jax.ShapeDtypeStruct((m, n), out_dtype),
        scratch_shapes=[pltpu.VMEM((k, tn), BF16)] if cast_w else [],
        compiler_params=_cparams(("arbitrary", "arbitrary")),
        name="matmul_res" if res is not None else "matmul",
    )(*args)


MASKED_BUCKET = N_BUCKETS


def _t5_bucket_np(dist):
    max_exact = N_BUCKETS // 2
    d = np.maximum(dist, 0)
    log_ratio = np.log(np.maximum(d, max_exact).astype(np.float32) / np.float32(max_exact))
    large = max_exact + (log_ratio / np.float32(math.log(MAX_DISTANCE / max_exact))
                         * np.float32(N_BUCKETS - max_exact)).astype(np.int32)
    return np.where(d < max_exact, d, np.minimum(large, N_BUCKETS - 1)).astype(np.int32)


def _bias_kernel(rb_ref, idx_ref, o_ref):
    h = pl.program_id(0)
    idx = idx_ref[...]
    acc = jnp.zeros(idx.shape, F32)
    for b in range(N_BUCKETS + 1):
        acc = jnp.where(idx == b, rb_ref[h, b], acc)
    o_ref[...] = acc


def _bias_table(rel_bias_ext, bucket_idx):
    h = rel_bias_ext.shape[0]
    rows = bucket_idx.shape[0]
    return pl.pallas_call(
        _bias_kernel,
        grid=(h,),
        in_specs=[pl.BlockSpec(memory_space=pltpu.SMEM),
                  pl.BlockSpec((rows, LANES), lambda i: (0, 0))],
        out_specs=pl.BlockSpec((rows, LANES), lambda i: (0, i)),
        out_shape=jax.ShapeDtypeStruct((rows, h * LANES), F32),
        compiler_params=_cparams(("arbitrary",)),
        name="bias_table",
    )(rel_bias_ext, jnp.asarray(bucket_idx))


def _compress_kernel(kv_ref, pos_ref, kw1_ref, kw2_ref, vw1_ref, vw2_ref, kc_ref, vct_ref, *, groups):
    half = pos_ref.shape[1] // 2
    nch = kv_ref.shape[1]
    gd = groups * HEAD_DIM
    g = pl.program_id(1)
    row = lax.broadcasted_iota(jnp.int32, (nch, HEAD_DIM), 0)

    def one(col, w1_ref, w2_ref):
        x = jnp.concatenate(
            [kv_ref[0, :, pl.ds(pl.multiple_of(o * 2 * gd + col + g * HEAD_DIM, HEAD_DIM), HEAD_DIM)]
             for o in range(CMP_STRIDE)], axis=1).astype(F32)
        lo = (x + pos_ref[:, :half]).astype(BF16)
        hi = pltpu.roll(x + pos_ref[:, half:], nch - 1, 0).astype(BF16)
        h = _dot(lo, w1_ref[:half, :]) + _dot(hi, w1_ref[half:, :])
        h = h * _sigmoid(h)
        o = _dot(h.astype(BF16), w2_ref[...])
        return jnp.where(row < nch - 1, o, 0.0)

    kc_ref[0, 0] = one(0, kw1_ref, kw2_ref).astype(kc_ref.dtype)
    vct_ref[0, 0] = one(gd, vw1_ref, vw2_ref).T.astype(vct_ref.dtype)


def _compress(kv, pos, kw1, kw2, vw1, vw2, *, groups):
    b, nch, wid = kv.shape
    oblk = pl.BlockSpec((1, 1, nch, HEAD_DIM), lambda i, j: (i, j, 0, 0))

    def const(a):
        return pl.BlockSpec(a.shape, lambda i, j: (0,) * a.ndim)

    out = jax.ShapeDtypeStruct((b, groups, nch, HEAD_DIM), BF16)
    return pl.pallas_call(
        functools.partial(_compress_kernel, groups=groups),
        grid=(b, groups),
        in_specs=[pl.BlockSpec((1, nch, wid), lambda i, j: (i, 0, 0)),
                  const(pos), const(kw1), const(kw2), const(vw1), const(vw2)],
        out_specs=[oblk, oblk],
        out_shape=[out, out],
        compiler_params=_cparams(("parallel", "arbitrary")),
        name="compress",
    )(kv, pos, kw1, kw2, vw1, vw2)


def _split3(x):
    a = x.astype(BF16)
    r = x - a.astype(F32)
    b = r.astype(BF16)
    c = (r - b.astype(F32)).astype(BF16)
    return a, b, c


def _nsa_kernel(q_ref, kc_ref, vct_ref, ks_ref, vs_ref, kw_ref, vw_ref, bc_ref, bt_ref, gl_ref,
                zc_ref, zs_ref, zw_ref, ovl_ref, o_ref, sel_scr, acc_scr, s_scr, p_scr, acc2_scr, s2_scr, p2_scr,
                vst_scr, vwt_scr, gate_scr,
                *, group, n_slc, n_sel):
    R = group
    QB = LANES
    W = R * QB
    i = pl.program_id(2)
    q0 = i * QB

    @pl.when(i == 0)
    def _():
        for c in range(vs_ref.shape[1] // QB):
            cols = slice(c * QB, (c + 1) * QB)
            vst_scr[:, cols] = vs_ref[0, cols, :].T
            vwt_scr[:, cols] = vw_ref[0, cols, :].T

    q = q_ref[0]
    qt = jnp.concatenate([q[:, r * HEAD_DIM:(r + 1) * HEAD_DIM].T for r in range(R)], axis=1)

    s = _dot(kc_ref[0, 0], qt) + bc_ref[...]
    m = jnp.max(s, axis=0, keepdims=True)
    e = jnp.exp2(s - m)
    denom = jnp.maximum(jnp.sum(e, axis=0, keepdims=True), 1e-30)
    p_c = e * jnp.where(m > 0.5 * NEG_BIG, 1.0 / denom, 0.0)
    o_c = _dot(vct_ref[0, 0], p_c.astype(BF16))

    psum = p_c[:, 0:QB]
    for r in range(1, R):
        psum = psum + p_c[:, r * QB:(r + 1) * QB]
    ovl = ovl_ref[...]
    imp = sum(_dot(ovl, part) for part in _split3(psum))[:n_slc]
    j = lax.broadcasted_iota(jnp.int32, (n_slc, QB), 0)
    t = q0 + lax.broadcasted_iota(jnp.int32, (n_slc, QB), 1)
    cur = t // SLC_BLOCK
    forced = (j == 0) | (j == cur) | (j == cur - 1)
    score = jnp.where(forced, FORCE_SCORE, jnp.where(j * SLC_BLOCK <= t, imp, -1.0))
    rank = jnp.zeros((n_slc, QB), F32)
    for jp in range(n_slc):
        other = score[jp:jp + 1, :]
        beats = (other > score) | ((other == score) & (j > jp))
        rank = rank + jnp.where(beats, 1.0, 0.0)
    sel_scr[...] = jnp.where(rank < n_sel, 0.0, NEG_BIG)

    far = WINDOW // QB
    lo_w = jnp.maximum(i - far, 0)

    def tile_slice(kt):
        return pl.ds(pl.multiple_of(kt * QB, QB), QB)

    def bias(tile):
        return bt_ref[tile_slice(tile), :]

    class Chain:
        def __init__(self, k_ref, vt_ref, s_ref, p_ref, a_ref, block_mask):
            self.k_ref, self.vt_ref, self.s_ref, self.p_ref, self.a_ref = k_ref, vt_ref, s_ref, p_ref, a_ref
            self.block_mask = block_mask

        def raw_scores(self, kt):
            return _dot(self.k_ref[0, tile_slice(kt), :], qt)

        def start(self, first):
            s = self.raw_scores(i) + bias(0)
            m = jnp.max(s, axis=0, keepdims=True)
            p = jnp.exp2(s - m)
            l = jnp.sum(p, axis=0, keepdims=True)
            self.a_ref[...] = jnp.zeros_like(self.a_ref)
            self.p_ref[...] = p.astype(BF16)
            self.s_ref[...] = self.raw_scores(first)
            return m, l, jnp.ones((1, W), F32)

        def accumulate(self, kt, alpha):
            self.a_ref[...] = alpha * self.a_ref[...] + _dot(self.vt_ref[:, tile_slice(kt)], self.p_ref[...])

        def step(self, carry, prev, cur, nxt, tile):
            m, l, alpha_prev = carry
            self.accumulate(prev, alpha_prev)
            per = QB // SLC_BLOCK
            if tile is None:
                far_bias = bt_ref[2 * QB:2 * QB + 1, :]
                rows = [far_bias + jnp.concatenate([sel_scr[pl.ds(cur * per + c, 1), :]] * R, axis=1)
                        for c in range(per)]
                s = self.s_ref[...] + jnp.concatenate(
                    [jnp.broadcast_to(row, (SLC_BLOCK, W)) for row in rows], axis=0)
            else:
                s = self.s_ref[...] + bias(tile)
                if self.block_mask:
                    add = jnp.concatenate(
                        [jnp.broadcast_to(sel_scr[pl.ds(cur * per + c, 1), :], (SLC_BLOCK, QB)) for c in range(per)],
                        axis=0)
                    s = s + jnp.concatenate([add] * R, axis=1)
            m_new = jnp.maximum(m, jnp.max(s, axis=0, keepdims=True))
            alpha = jnp.exp2(m - m_new)
            p = jnp.exp2(s - m_new)
            l = alpha * l + jnp.sum(p, axis=0, keepdims=True)
            self.p_ref[...] = p.astype(BF16)
            self.s_ref[...] = self.raw_scores(nxt)
            return m_new, l, alpha

        def finish(self, carry, last):
            _, l, alpha = carry
            self.accumulate(last, alpha)
            return self.a_ref[...] * (1.0 / l)

    sel = Chain(ks_ref, vst_scr, s_scr, p_scr, acc_scr, True)
    win = Chain(kw_ref, vwt_scr, s2_scr, p2_scr, acc2_scr, False)
    carry_s = sel.start(lo_w)
    carry_w = win.start(lo_w)

    def both(kt, carry):
        carry_s, carry_w = carry
        prev = jnp.where(kt == lo_w, i, kt - 1)
        d = i - kt
        carry_s = sel.step(carry_s, prev, kt, jnp.where(kt == i - 1, 0, kt + 1), jnp.minimum(d, 2))
        carry_w = win.step(carry_w, prev, kt, jnp.minimum(kt + 1, i), jnp.where(d == far, 3, jnp.minimum(d, 2)))
        return carry_s, carry_w

    carry_s, carry_w = lax.fori_loop(lo_w, i, both, (carry_s, carry_w))
    o_w = win.finish(carry_w, jnp.where(i > 0, i - 1, i))

    def sel_only(kt, carry):
        return sel.step(carry, jnp.where(kt == 0, i - 1, kt - 1), kt, jnp.minimum(kt + 1, i), None)

    carry_s = lax.fori_loop(0, lo_w, sel_only, carry_s)
    o_s = sel.finish(carry_s, jnp.where(lo_w > 0, lo_w - 1, jnp.where(i > 0, i - 1, i)))

    n_heads = R * pl.num_programs(0)
    head0 = pl.program_id(0) * R
    gate_scr[...] = _sigmoid(gl_ref[0]).T
    for r in range(R):
        cols = slice(r * QB, (r + 1) * QB)
        mixed = jnp.zeros((QB, HEAD_DIM), F32)
        for c, (o, z_ref) in enumerate(((o_c, zc_ref), (o_s, zs_ref), (o_w, zw_ref))):
            z = z_ref[0, :, cols]
            gated = o[:, cols] * gate_scr[pl.ds(c * n_heads + head0 + r, 1), :]
            mixed = mixed + gated.T * (z * _sigmoid(z))
        o_ref[0, :, cols] = mixed.astype(o_ref.dtype)


def _nsa_attention(qkv, kc, vct, bias_c, bias_t, gl, z, ovl, *, batch, seq):
    G, R = N_KV_GROUPS, N_HEADS // N_KV_GROUPS
    QB = LANES
    W = R * QB
    nqb = seq // QB
    n_slc = seq // SLC_BLOCK
    n_sel = min(N_SELECT, n_slc)

    def kv_spec(which):
        return pl.BlockSpec((1, seq, HEAD_DIM), lambda g, b, i: (b, 0, N_HEADS + which * G + g))

    def z_spec(c):
        return pl.BlockSpec((1, QB, R * HEAD_DIM), lambda g, b, i: (b, i, c * G + g))

    cmp_spec = pl.BlockSpec((1, 1, LANES, HEAD_DIM), lambda g, b, i: (b, g, 0, 0))
    in_specs = [
        pl.BlockSpec((1, QB, R * HEAD_DIM), lambda g, b, i: (b, i, g)),
        cmp_spec, cmp_spec,
        kv_spec(2), kv_spec(3), kv_spec(4), kv_spec(5),
        pl.BlockSpec((QB, W), lambda g, b, i: (i, g)),
        pl.BlockSpec((4 * QB, W), lambda g, b, i: (0, g)),
        pl.BlockSpec((1, QB, LANES), lambda g, b, i: (b, i, 0)),
        z_spec(0), z_spec(1), z_spec(2),
        pl.BlockSpec(ovl.shape, lambda g, b, i: (0, 0)),
    ]
    return pl.pallas_call(
        functools.partial(_nsa_kernel, group=R, n_slc=n_slc, n_sel=n_sel),
        grid=(G, batch, nqb),
        in_specs=in_specs,
        out_specs=pl.BlockSpec((1, QB, R * HEAD_DIM), lambda g, b, i: (b, i, g)),
        out_shape=jax.ShapeDtypeStruct((batch, seq, N_HEADS * HEAD_DIM), BF16),
        scratch_shapes=[pltpu.VMEM((n_slc, QB), F32), pltpu.VMEM((HEAD_DIM, W), F32),
                        pltpu.VMEM((QB, W), F32), pltpu.VMEM((QB, W), BF16),
                        pltpu.VMEM((HEAD_DIM, W), F32), pltpu.VMEM((QB, W), F32), pltpu.VMEM((QB, W), BF16),
                        pltpu.VMEM((HEAD_DIM, seq), BF16), pltpu.VMEM((HEAD_DIM, seq), BF16),
                        pltpu.VMEM((LANES, QB), F32)],
        compiler_params=_cparams(("parallel", "parallel", "arbitrary")),
        name="nsa_attention",
    )(qkv, kc, vct, qkv, qkv, qkv, qkv, bias_c, bias_t, gl, z, z, z, ovl)


def _sb_kernel(q_ref, k_ref, v_ref, z_ref, tri_ref, o_ref, vt_scr, l_scr, d_scr, w_scr, acc_scr, *, tq, hp):
    i = pl.program_id(2)
    tk = tq
    tri = tri_ref[...]
    n_sub = tk // LANES
    heads = range(hp)

    @pl.when(i == 0)
    def _():
        for g in heads:
            for c in range(v_ref.shape[1] // LANES):
                cols = slice(c * LANES, (c + 1) * LANES)
                vt_scr[g, :, cols] = v_ref[0, cols, g * HEAD_DIM:(g + 1) * HEAD_DIM].T

    qts = [q_ref[0, :, g * HEAD_DIM:(g + 1) * HEAD_DIM].T for g in heads]

    def key0(t):
        return pl.multiple_of(jnp.maximum(i - t, 0) * tk, tk)

    def logits(t):
        k0 = key0(t)
        for g in heads:
            l_scr[g] = _dot(k_ref[0, pl.ds(k0, tk), g * HEAD_DIM:(g + 1) * HEAD_DIM], qts[g])

    def softplus_sums(diagonal):
        totals = []
        for g in heads:
            logit = l_scr[g]
            neg_abs = pltpu.bitcast(pltpu.bitcast(logit, jnp.uint32) | jnp.uint32(0x80000000), F32)
            sp = jnp.maximum(logit, 0.0) + jnp.log(1.0 + jnp.exp2(neg_abs)) * LOG2E
            if diagonal:
                ok = (lax.broadcasted_iota(jnp.int32, (tk, tq), 0) < lax.broadcasted_iota(jnp.int32, (tk, tq), 1))
                sp = jnp.where(ok, sp, 0.0)
                logit = jnp.where(ok, logit, NEG_BIG)
            d_scr[g] = logit
            hi = sp.astype(BF16)
            lo = (sp - hi.astype(F32)).astype(BF16)
            tot = []
            for c in range(n_sub):
                rows = slice(c * LANES, (c + 1) * LANES)
                within = _dot(tri, jnp.concatenate([hi[rows], lo[rows]], axis=0))
                w_scr[g, rows, :] = within
                tot.append(within[0:1])
            totals.append(tuple(tot))
        return tuple(totals)

    def weigh_values(t, laters, totals):
        k0 = key0(t)
        out = []
        for g in heads:
            later = laters[g]
            parts = [None] * n_sub
            for c in reversed(range(n_sub)):
                rows = slice(c * LANES, (c + 1) * LANES)
                parts[c] = jnp.exp2(d_scr[g, rows, :] - w_scr[g, rows, :] - later)
                later = later + totals[g][c]
            a = jnp.concatenate(parts, axis=0).astype(BF16)
            acc_scr[g] += _dot(vt_scr[g, :, pl.ds(k0, tk)], a)
            out.append(later)
        return tuple(out)

    acc_scr[...] = jnp.zeros_like(acc_scr)
    logits(0)
    totals = softplus_sums(True)
    logits(1)

    def body(t, carry):
        laters, totals = carry
        laters = weigh_values(t - 1, laters, totals)
        totals = softplus_sums(False)
        logits(t + 1)
        return laters, totals

    laters, totals = lax.fori_loop(1, i + 1, body, ((jnp.zeros((1, tq), F32),) * hp, totals))
    weigh_values(i, laters, totals)
    for g in heads:
        cols = slice(g * HEAD_DIM, (g + 1) * HEAD_DIM)
        z = z_ref[0, :, cols]
        o_ref[0, :, cols] = (acc_scr[g].T * (z * _sigmoid(z))).astype(o_ref.dtype)


def _sb_attention(q, kv, z, tri, *, batch, seq, tq=512, hp=4):
    H = N_HEADS
    tq = min(tq, seq)
    wid = hp * HEAD_DIM
    return pl.pallas_call(
        functools.partial(_sb_kernel, tq=tq, hp=hp),
        grid=(batch, H // hp, seq // tq),
        in_specs=[
            pl.BlockSpec((1, tq, wid), lambda b, h, i: (b, i, h)),
            pl.BlockSpec((1, seq, wid), lambda b, h, i: (b, 0, h)),
            pl.BlockSpec((1, seq, wid), lambda b, h, i: (b, 0, H // hp + h)),
            pl.BlockSpec((1, tq, wid), lambda b, h, i: (b, i, h)),
            pl.BlockSpec(tri.shape, lambda b, h, i: (0, 0)),
        ],
        out_specs=pl.BlockSpec((1, tq, wid), lambda b, h, i: (b, i, h)),
        out_shape=jax.ShapeDtypeStruct((batch, seq, H * HEAD_DIM), BF16),
        scratch_shapes=[pltpu.VMEM((hp, HEAD_DIM, seq), BF16), pltpu.VMEM((hp, tq, tq), F32),
                        pltpu.VMEM((hp, tq, tq), F32), pltpu.VMEM((hp, tq, tq), F32),
                        pltpu.VMEM((hp, HEAD_DIM, tq), F32)],
        compiler_params=_cparams(("parallel", "parallel", "arbitrary")),
        name="sb_attention",
    )(q, kv, kv, z, tri)


def _static_tables(seq):
    QB = LANES
    nch = seq // CMP_STRIDE
    n_slc = seq // SLC_BLOCK
    assert nch == LANES and n_slc <= LANES
    q = np.arange(QB)[None, :]
    rows = np.arange(seq)[:, None]
    dist_c = (rows // QB) * QB + q - ((rows % QB) * CMP_STRIDE + CMP_BLOCK - 1)
    idx_c = np.where(dist_c >= 0, _t5_bucket_np(dist_c), MASKED_BUCKET).astype(np.int32)
    k = np.arange(QB)[:, None]
    far = WINDOW // QB
    tiles = []
    for delta, keep in ((0, k <= q), (1, None), (2, None), (far, k > q)):
        idx = _t5_bucket_np(delta * QB + q - k)
        tiles.append(idx if keep is None else np.where(keep, idx, MASKED_BUCKET))
    idx_t = np.concatenate(tiles, axis=0).astype(np.int32)
    cmp_start = np.arange(LANES) * CMP_STRIDE
    slc_start = np.arange(LANES) * SLC_BLOCK
    ovl = ((cmp_start[None, :] < slc_start[:, None] + SLC_BLOCK)
           & (cmp_start[None, :] + CMP_BLOCK - 1 >= slc_start[:, None]))
    ovl = ovl & (np.arange(LANES)[None, :] < nch - 1) & (np.arange(LANES)[:, None] < n_slc)
    return idx_c, idx_t, ovl.astype(np.float32)


def _nsa_layer(xf, tabs, norm, w_in, cmp_pos, kw1, kw2, vw1, vw2, w_out, *, batch, seq):
    bias_c, bias_t, ovl = tabs
    H, G, Dh = N_HEADS, N_KV_GROUPS, HEAD_DIM
    HD, GD = H * Dh, G * Dh
    n_qkv = HD + 6 * GD
    hn, = _rmsnorm(xf, [norm], BF16)
    qscale = jnp.concatenate([jnp.full((1, HD), LOG2E / math.sqrt(Dh), F32), jnp.ones((1, 6 * GD), F32)], axis=1)
    w_in_t = w_in.T
    qkv = _matmul(hn, w_in_t, BF16, n=n_qkv, colscale=qscale, w_rows=True)
    assert 3 * H <= LANES
    gl = _matmul(hn, w_in_t, F32, col0=n_qkv, n=LANES, w_rows=True)
    z = _matmul(hn, w_in_t, F32, col0=n_qkv + 3 * H, n=3 * HD, w_rows=True)

    nch = seq // CMP_STRIDE
    cmp_in = qkv[:, HD:HD + 2 * GD].reshape(batch, nch, CMP_STRIDE * 2 * GD)
    kc, vct = _compress(cmp_in, cmp_pos.reshape(1, CMP_BLOCK * Dh),
                        kw1.astype(BF16), kw2.astype(BF16), vw1.astype(BF16), vw2.astype(BF16), groups=G)
    mixed = _nsa_attention(qkv.reshape(batch, seq, n_qkv), kc, vct, bias_c, bias_t,
                           gl.reshape(batch, seq, LANES), z.reshape(batch, seq, 3 * HD),
                           ovl, batch=batch, seq=seq)
    return _matmul(mixed.reshape(batch * seq, HD), w_out, F32, res=xf)


def _sb_layer(xf, hn, kv, tri, w_in, w_out, *, batch, seq):
    H, Dh = N_HEADS, HEAD_DIM
    HD = H * Dh
    q = _matmul(hn, w_in, BF16, n=HD, colscale=jnp.full((1, HD), LOG2E / math.sqrt(Dh), F32))
    z = _matmul(hn, w_in, F32, col0=HD, n=HD)
    o = _sb_attention(q.reshape(batch, seq, HD), kv, z.reshape(batch, seq, HD), tri, batch=batch, seq=seq)
    return _matmul(o.reshape(batch * seq, HD), w_out, F32, res=xf)


def kernel(x, rel_bias, a0_norm, a0_w_in, a0_cmp_pos, a0_cmp_k_w1, a0_cmp_k_w2, a0_cmp_v_w1, a0_cmp_v_w2, a0_w_out, a1_norm, a1_w_in, a1_cmp_pos, a1_cmp_k_w1, a1_cmp_k_w2, a1_cmp_v_w1, a1_cmp_v_w2, a1_w_out, kv_norm, w_kv, b2_norm, b2_w_in, b2_w_out, b3_norm, b3_w_in, b3_w_out, final_norm):
    batch, seq, d = x.shape
    HD = N_HEADS * HEAD_DIM
    xf = x.reshape(batch * seq, d)

    idx_c, idx_t, ovl = _static_tables(seq)
    rel_bias_ext = jnp.concatenate([rel_bias.T * LOG2E, jnp.full((N_HEADS, 1), NEG_BIG, F32)], axis=1)
    tabs = (_bias_table(rel_bias_ext, idx_c), _bias_table(rel_bias_ext, idx_t), jnp.asarray(ovl, BF16))

    xf = _nsa_layer(xf, tabs, a0_norm, a0_w_in, a0_cmp_pos, a0_cmp_k_w1, a0_cmp_k_w2, a0_cmp_v_w1, a0_cmp_v_w2,
                    a0_w_out, batch=batch, seq=seq)
    xf = _nsa_layer(xf, tabs, a1_norm, a1_w_in, a1_cmp_pos, a1_cmp_k_w1, a1_cmp_k_w2, a1_cmp_v_w1, a1_cmp_v_w2,
                    a1_w_out, batch=batch, seq=seq)

    hn_kv, hn_b2 = _rmsnorm(xf, [kv_norm, b2_norm], BF16)
    kv = _matmul(hn_kv, w_kv, BF16).reshape(batch, seq, 2 * HD)
    m = np.arange(LANES)[None, :] >= np.arange(LANES)[:, None]
    tri = jnp.asarray(np.concatenate([m, m], axis=1), BF16)
    xf = _sb_layer(xf, hn_b2, kv, tri, b2_w_in, b2_w_out, batch=batch, seq=seq)
    hn_b3, = _rmsnorm(xf, [b3_norm], BF16)
    xf = _sb_layer(xf, hn_b3, kv, tri, b3_w_in, b3_w_out, batch=batch, seq=seq)

    out, = _rmsnorm(xf, [final_norm], F32)
    return out.reshape(batch, seq, d)
```

```python
import functools
import math

import numpy as np
import jax
import jax.numpy as jnp
from jax import lax
from jax.experimental import pallas as pl
from jax.experimental.pallas import tpu as pltpu

N_HEADS = 32
HEAD_DIM = 128
N_KV_GROUPS = 4
CMP_BLOCK = 32
CMP_STRIDE = 16
SLC_BLOCK = 64
N_SELECT = 16
WINDOW = 512
N_BUCKETS = 32
MAX_DISTANCE = 128
RMS_EPS = 1e-6
FORCE_SCORE = 1e6

LANES = 128
NEG_BIG = -1e30
VMEM_LIMIT = 56 * 1024 * 1024
LOG2E = math.log2(math.e)

F32 = jnp.float32
BF16 = jnp.bfloat16


def _cparams(sem):
    return pltpu.CompilerParams(dimension_semantics=sem, vmem_limit_bytes=VMEM_LIMIT)


def _dot(a, b):
    return jnp.dot(a, b, preferred_element_type=F32)


def _sigmoid(x):
    return 0.5 * jnp.tanh(0.5 * x) + 0.5


def _rmsnorm_kernel(x_ref, *refs):
    n_out = len(refs) // 2
    x = x_ref[...]
    ms = jnp.mean(x * x, axis=-1, keepdims=True)
    y = x * lax.rsqrt(ms + RMS_EPS)
    for g_ref, o_ref in zip(refs[:n_out], refs[n_out:]):
        o_ref[...] = (y * g_ref[...]).astype(o_ref.dtype)


def _rmsnorm(x, gains, out_dtype, tm=512):
    m, d = x.shape
    tm = min(tm, m)
    row = pl.BlockSpec((tm, d), lambda i: (i, 0))
    outs = pl.pallas_call(
        _rmsnorm_kernel,
        grid=(m // tm,),
        in_specs=[row] + [pl.BlockSpec((1, d), lambda i: (0, 0))] * len(gains),
        out_specs=[row] * len(gains),
        out_shape=[jax.ShapeDtypeStruct((m, d), out_dtype)] * len(gains),
        compiler_params=_cparams(("parallel",)),
        name="rmsnorm",
    )(x, *[g.reshape(1, d) for g in gains])
    return outs


W_SLABS = 4


def _matmul_kernel(*refs, has_res, has_scale, cast_w, shift, w_rows):
    it = iter(refs)
    x_ref = next(it)
    w_refs = [next(it) for _ in range(W_SLABS if cast_w else 1)]
    wn_ref = next(it) if shift else None
    s_ref = next(it) if has_scale else None
    r_ref = next(it) if has_res else None
    o_ref = next(it)
    if cast_w:
        wb_ref = next(it)

        @pl.when(pl.program_id(1) == 0)
        def _():
            ks = wb_ref.shape[0] // W_SLABS
            for s, w_ref in enumerate(w_refs):
                w = w_ref[...]
                if shift:
                    ax = 0 if w_rows else 1
                    nxt = wn_ref[:, s * ks:(s + 1) * ks] if w_rows else wn_ref[s * ks:(s + 1) * ks, :]
                    w = jnp.concatenate([lax.slice_in_dim(w, shift, w.shape[ax], axis=ax),
                                         lax.slice_in_dim(nxt, 0, shift, axis=ax)], axis=ax)
                if w_rows:
                    w = w.T
                if has_scale:
                    w = w * s_ref[...]
                wb_ref[s * ks:(s + 1) * ks, :] = w.astype(BF16)
    else:
        wb_ref = w_refs[0]
    acc = _dot(x_ref[...], wb_ref[...])
    if has_res:
        acc = r_ref[...] + acc
    o_ref[...] = acc.astype(o_ref.dtype)


def _matmul(x, w, out_dtype, *, col0=0, n=None, res=None, colscale=None, w_rows=False, tm=1024, tn=512):
    m, k = x.shape
    n_total = w.shape[0] if w_rows else w.shape[1]
    n = n_total - col0 if n is None else n
    tm, tn = min(tm, m), min(tn, n)
    cast_w = w.dtype != BF16
    shift = col0 % LANES
    base = col0 - shift
    assert m % tm == 0 and n % tn == 0 and base % tn == 0 and tn % LANES == 0
    assert cast_w or (colscale is None and shift == 0 and not w_rows)
    joff = base // tn
    per = tn // LANES
    n_i = m // tm
    in_specs = [pl.BlockSpec((tm, k), lambda j, i: (i, 0))]
    args = [x]
    if cast_w:
        assert k % (W_SLABS * LANES) == 0
        ks = k // W_SLABS
        stagger = n_i > W_SLABS
        for s in range(W_SLABS):
            def tile(j, i, s=s):
                ahead = (i > s).astype(jnp.int32) if stagger else 0
                return jnp.minimum(j + ahead, n // tn - 1) + joff
            if w_rows:
                in_specs.append(pl.BlockSpec((tn, ks), lambda j, i, s=s, tile=tile: (tile(j, i), s)))
            else:
                in_specs.append(pl.BlockSpec((ks, tn), lambda j, i, s=s, tile=tile: (s, tile(j, i))))
            args.append(w)
    else:
        in_specs.append(pl.BlockSpec((k, tn), lambda j, i: (0, j + joff)))
        args.append(w)
    if shift:
        if w_rows:
            assert shift % 8 == 0
            in_specs.append(pl.BlockSpec((LANES, k), lambda j, i: ((j + joff + 1) * per, 0)))
        else:
            in_specs.append(pl.BlockSpec((k, LANES), lambda j, i: (0, (j + joff + 1) * per)))
        args.append(w)
    if colscale is not None:
        in_specs.append(pl.BlockSpec((1, tn), lambda j, i: (0, j)))
        args.append(colscale)
    if res is not None:
        in_specs.append(pl.BlockSpec((tm, tn), lambda j, i: (i, j)))
        args.append(res)
    return pl.pallas_call(
        functools.partial(_matmul_kernel, has_res=res is not None, has_scale=colscale is not None,
                          cast_w=cast_w, shift=shift, w_rows=w_rows),
        grid=(n // tn, n_i),
        in_specs=in_specs,
        out_specs=pl.BlockSpec((tm, tn), lambda j, i: (i, j)),
        out_shape=jax.ShapeDtypeStruct((m, n), out_dtype),
        scratch_shapes=[pltpu.VMEM((k, tn), BF16)] if cast_w else [],
        compiler_params=_cparams(("arbitrary", "arbitrary")),
        name="matmul_res" if res is not None else "matmul",
    )(*args)


MASKED_BUCKET = N_BUCKETS


def _t5_bucket_np(dist):
    max_exact = N_BUCKETS // 2
    d = np.maximum(dist, 0)
    log_ratio = np.log(np.maximum(d, max_exact).astype(np.float32) / np.float32(max_exact))
    large = max_exact + (log_ratio / np.float32(math.log(MAX_DISTANCE / max_exact))
                         * np.float32(N_BUCKETS - max_exact)).astype(np.int32)
    return np.where(d < max_exact, d, np.minimum(large, N_BUCKETS - 1)).astype(np.int32)


def _bias_kernel(rb_ref, idx_ref, o_ref):
    h = pl.program_id(0)
    idx = idx_ref[...]
    acc = jnp.zeros(idx.shape, F32)
    for b in range(N_BUCKETS + 1):
        acc = jnp.where(idx == b, rb_ref[h, b], acc)
    o_ref[...] = acc


def _bias_table(rel_bias_ext, bucket_idx):
    h = rel_bias_ext.shape[0]
    rows = bucket_idx.shape[0]
    return pl.pallas_call(
        _bias_kernel,
        grid=(h,),
        in_specs=[pl.BlockSpec(memory_space=pltpu.SMEM),
                  pl.BlockSpec((rows, LANES), lambda i: (0, 0))],
        out_specs=pl.BlockSpec((rows, LANES), lambda i: (0, i)),
        out_shape=jax.ShapeDtypeStruct((rows, h * LANES), F32),
        compiler_params=_cparams(("arbitrary",)),
        name="bias_table",
    )(rel_bias_ext, jnp.asarray(bucket_idx))


def _compress_kernel(kv_ref, pos_ref, kw1_ref, kw2_ref, vw1_ref, vw2_ref, kc_ref, vct_ref, *, groups):
    half = pos_ref.shape[1] // 2
    nch = kv_ref.shape[1]
    gd = groups * HEAD_DIM
    g = pl.program_id(1)
    row = lax.broadcasted_iota(jnp.int32, (nch, HEAD_DIM), 0)

    def one(col, w1_ref, w2_ref):
        x = jnp.concatenate(
            [kv_ref[0, :, pl.ds(pl.multiple_of(o * 2 * gd + col + g * HEAD_DIM, HEAD_DIM), HEAD_DIM)]
             for o in range(CMP_STRIDE)], axis=1).astype(F32)
        lo = (x + pos_ref[:, :half]).astype(BF16)
        hi = pltpu.roll(x + pos_ref[:, half:], nch - 1, 0).astype(BF16)
        h = _dot(lo, w1_ref[:half, :]) + _dot(hi, w1_ref[half:, :])
        h = h * _sigmoid(h)
        o = _dot(h.astype(BF16), w2_ref[...])
        return jnp.where(row < nch - 1, o, 0.0)

    kc_ref[0, 0] = one(0, kw1_ref, kw2_ref).astype(kc_ref.dtype)
    vct_ref[0, 0] = one(gd, vw1_ref, vw2_ref).T.astype(vct_ref.dtype)


def _compress(kv, pos, kw1, kw2, vw1, vw2, *, groups):
    b, nch, wid = kv.shape
    oblk = pl.BlockSpec((1, 1, nch, HEAD_DIM), lambda i, j: (i, j, 0, 0))

    def const(a):
        return pl.BlockSpec(a.shape, lambda i, j: (0,) * a.ndim)

    out = jax.ShapeDtypeStruct((b, groups, nch, HEAD_DIM), BF16)
    return pl.pallas_call(
        functools.partial(_compress_kernel, groups=groups),
        grid=(b, groups),
        in_specs=[pl.BlockSpec((1, nch, wid), lambda i, j: (i, 0, 0)),
                  const(pos), const(kw1), const(kw2), const(vw1), const(vw2)],
        out_specs=[oblk, oblk],
        out_shape=[out, out],
        compiler_params=_cparams(("parallel", "arbitrary")),
        name="compress",
    )(kv, pos, kw1, kw2, vw1, vw2)


def _split3(x):
    a = x.astype(BF16)
    r = x - a.astype(F32)
    b = r.astype(BF16)
    c = (r - b.astype(F32)).astype(BF16)
    return a, b, c


def _nsa_kernel(q_ref, kc_ref, vct_ref, ks_ref, vs_ref, kw_ref, vw_ref, bc_ref, bt_ref, gl_ref,
                zc_ref, zs_ref, zw_ref, ovl_ref, o_ref, sel_scr, acc_scr, s_scr, p_scr, acc2_scr, s2_scr, p2_scr,
                vst_scr, vwt_scr, gate_scr,
                *, group, n_slc, n_sel):
    R = group
    QB = LANES
    W = R * QB
    i = pl.program_id(2)
    q0 = i * QB

    @pl.when(i == 0)
    def _():
        for c in range(vs_ref.shape[1] // QB):
            cols = slice(c * QB, (c + 1) * QB)
            vst_scr[:, cols] = vs_ref[0, cols, :].T
            vwt_scr[:, cols] = vw_ref[0, cols, :].T

    q = q_ref[0]
    qt = jnp.concatenate([q[:, r * HEAD_DIM:(r + 1) * HEAD_DIM].T for r in range(R)], axis=1)

    s = _dot(kc_ref[0, 0], qt) + bc_ref[...]
    m = jnp.max(s, axis=0, keepdims=True)
    e = jnp.exp2(s - m)
    denom = jnp.maximum(jnp.sum(e, axis=0, keepdims=True), 1e-30)
    p_c = e * jnp.where(m > 0.5 * NEG_BIG, 1.0 / denom, 0.0)
    o_c = _dot(vct_ref[0, 0], p_c.astype(BF16))

    psum = p_c[:, 0:QB]
    for r in range(1, R):
        psum = psum + p_c[:, r * QB:(r + 1) * QB]
    ovl = ovl_ref[...]
    imp = sum(_dot(ovl, part) for part in _split3(psum))[:n_slc]
    j = lax.broadcasted_iota(jnp.int32, (n_slc, QB), 0)
    t = q0 + lax.broadcasted_iota(jnp.int32, (n_slc, QB), 1)
    cur = t // SLC_BLOCK
    forced = (j == 0) | (j == cur) | (j == cur - 1)
    score = jnp.where(forced, FORCE_SCORE, jnp.where(j * SLC_BLOCK <= t, imp, -1.0))
    rank = jnp.zeros((n_slc, QB), F32)
    for jp in range(n_slc):
        other = score[jp:jp + 1, :]
        beats = (other > score) | ((other == score) & (j > jp))
        rank = rank + jnp.where(beats, 1.0, 0.0)
    sel_scr[...] = jnp.where(rank < n_sel, 0.0, NEG_BIG)

    far = WINDOW // QB
    lo_w = jnp.maximum(i - far, 0)

    def tile_slice(kt):
        return pl.ds(pl.multiple_of(kt * QB, QB), QB)

    def bias(tile):
        return bt_ref[tile_slice(tile), :]

    class Chain:
        def __init__(self, k_ref, vt_ref, s_ref, p_ref, a_ref, block_mask):
            self.k_ref, self.vt_ref, self.s_ref, self.p_ref, self.a_ref = k_ref, vt_ref, s_ref, p_ref, a_ref
            self.block_mask = block_mask

        def raw_scores(self, kt):
            return _dot(self.k_ref[0, tile_slice(kt), :], qt)

        def start(self, first):
            s = self.raw_scores(i) + bias(0)
            m = jnp.max(s, axis=0, keepdims=True)
            p = jnp.exp2(s - m)
            l = jnp.sum(p, axis=0, keepdims=True)
            self.a_ref[...] = jnp.zeros_like(self.a_ref)
            self.p_ref[...] = p.astype(BF16)
            self.s_ref[...] = self.raw_scores(first)
            return m, l, jnp.ones((1, W), F32)

        def accumulate(self, kt, alpha):
            self.a_ref[...] = alpha * self.a_ref[...] + _dot(self.vt_ref[:, tile_slice(kt)], self.p_ref[...])

        def step(self, carry, prev, cur, nxt, tile):
            m, l, alpha_prev = carry
            self.accumulate(prev, alpha_prev)
            s = self.s_ref[...] + bias(tile)
            if self.block_mask:
                per = QB // SLC_BLOCK
                add = jnp.concatenate(
                    [jnp.broadcast_to(sel_scr[pl.ds(cur * per + c, 1), :], (SLC_BLOCK, QB)) for c in range(per)],
                    axis=0)
                s = s + jnp.concatenate([add] * R, axis=1)
            m_new = jnp.maximum(m, jnp.max(s, axis=0, keepdims=True))
            alpha = jnp.exp2(m - m_new)
            p = jnp.exp2(s - m_new)
            l = alpha * l + jnp.sum(p, axis=0, keepdims=True)
            self.p_ref[...] = p.astype(BF16)
            self.s_ref[...] = self.raw_scores(nxt)
            return m_new, l, alpha

        def finish(self, carry, last):
            _, l, alpha = carry
            self.accumulate(last, alpha)
            return self.a_ref[...] * (1.0 / l)

    sel = Chain(ks_ref, vst_scr, s_scr, p_scr, acc_scr, True)
    win = Chain(kw_ref, vwt_scr, s2_scr, p2_scr, acc2_scr, False)
    carry_s = sel.start(lo_w)
    carry_w = win.start(lo_w)

    def both(kt, carry):
        carry_s, carry_w = carry
        prev = jnp.where(kt == lo_w, i, kt - 1)
        d = i - kt
        carry_s = sel.step(carry_s, prev, kt, jnp.where(kt == i - 1, 0, kt + 1), jnp.minimum(d, 2))
        carry_w = win.step(carry_w, prev, kt, jnp.minimum(kt + 1, i), jnp.where(d == far, 3, jnp.minimum(d, 2)))
        return carry_s, carry_w

    carry_s, carry_w = lax.fori_loop(lo_w, i, both, (carry_s, carry_w))
    o_w = win.finish(carry_w, jnp.where(i > 0, i - 1, i))

    def sel_only(kt, carry):
        return sel.step(carry, jnp.where(kt == 0, i - 1, kt - 1), kt, jnp.minimum(kt + 1, i), 2)

    carry_s = lax.fori_loop(0, lo_w, sel_only, carry_s)
    o_s = sel.finish(carry_s, jnp.where(lo_w > 0, lo_w - 1, jnp.where(i > 0, i - 1, i)))

    n_heads = R * pl.num_programs(0)
    head0 = pl.program_id(0) * R
    gate_scr[...] = _sigmoid(gl_ref[0]).T
    for r in range(R):
        cols = slice(r * QB, (r + 1) * QB)
        mixed = jnp.zeros((QB, HEAD_DIM), F32)
        for c, (o, z_ref) in enumerate(((o_c, zc_ref), (o_s, zs_ref), (o_w, zw_ref))):
            z = z_ref[0, :, cols]
            gated = o[:, cols] * gate_scr[pl.ds(c * n_heads + head0 + r, 1), :]
            mixed = mixed + gated.T * (z * _sigmoid(z))
        o_ref[0, :, cols] = mixed.astype(o_ref.dtype)


def _nsa_attention(qkv, kc, vct, bias_c, bias_t, gl, z, ovl, *, batch, seq):
    G, R = N_KV_GROUPS, N_HEADS // N_KV_GROUPS
    QB = LANES
    W = R * QB
    nqb = seq // QB
    n_slc = seq // SLC_BLOCK
    n_sel = min(N_SELECT, n_slc)

    def kv_spec(which):
        return pl.BlockSpec((1, seq, HEAD_DIM), lambda g, b, i: (b, 0, N_HEADS + which * G + g))

    def z_spec(c):
        return pl.BlockSpec((1, QB, R * HEAD_DIM), lambda g, b, i: (b, i, c * G + g))

    cmp_spec = pl.BlockSpec((1, 1, LANES, HEAD_DIM), lambda g, b, i: (b, g, 0, 0))
    in_specs = [
        pl.BlockSpec((1, QB, R * HEAD_DIM), lambda g, b, i: (b, i, g)),
        cmp_spec, cmp_spec,
        kv_spec(2), kv_spec(3), kv_spec(4), kv_spec(5),
        pl.BlockSpec((QB, W), lambda g, b, i: (i, g)),
        pl.BlockSpec((4 * QB, W), lambda g, b, i: (0, g)),
        pl.BlockSpec((1, QB, LANES), lambda g, b, i: (b, i, 0)),
        z_spec(0), z_spec(1), z_spec(2),
        pl.BlockSpec(ovl.shape, lambda g, b, i: (0, 0)),
    ]
    return pl.pallas_call(
        functools.partial(_nsa_kernel, group=R, n_slc=n_slc, n_sel=n_sel),
        grid=(G, batch, nqb),
        in_specs=in_specs,
        out_specs=pl.BlockSpec((1, QB, R * HEAD_DIM), lambda g, b, i: (b, i, g)),
        out_shape=jax.ShapeDtypeStruct((batch, seq, N_HEADS * HEAD_DIM), BF16),
        scratch_shapes=[pltpu.VMEM((n_slc, QB), F32), pltpu.VMEM((HEAD_DIM, W), F32),
                        pltpu.VMEM((QB, W), F32), pltpu.VMEM((QB, W), BF16),
                        pltpu.VMEM((HEAD_DIM, W), F32), pltpu.VMEM((QB, W), F32), pltpu.VMEM((QB, W), BF16),
                        pltpu.VMEM((HEAD_DIM, seq), BF16), pltpu.VMEM((HEAD_DIM, seq), BF16),
                        pltpu.VMEM((LANES, QB), F32)],
        compiler_params=_cparams(("parallel", "parallel", "arbitrary")),
        name="nsa_attention",
    )(qkv, kc, vct, qkv, qkv, qkv, qkv, bias_c, bias_t, gl, z, z, z, ovl)


def _sb_kernel(q_ref, k_ref, v_ref, z_ref, tri_ref, o_ref, vt_scr, l_scr, d_scr, w_scr, acc_scr, *, tq, hp):
    i = pl.program_id(2)
    tk = tq
    tri = tri_ref[...]
    n_sub = tk // LANES
    heads = range(hp)

    @pl.when(i == 0)
    def _():
        for g in heads:
            for c in range(v_ref.shape[1] // LANES):
                cols = slice(c * LANES, (c + 1) * LANES)
                vt_scr[g, :, cols] = v_ref[0, cols, g * HEAD_DIM:(g + 1) * HEAD_DIM].T

    qts = [q_ref[0, :, g * HEAD_DIM:(g + 1) * HEAD_DIM].T for g in heads]

    def key0(t):
        return pl.multiple_of(jnp.maximum(i - t, 0) * tk, tk)

    def logits(t):
        k0 = key0(t)
        for g in heads:
            l_scr[g] = _dot(k_ref[0, pl.ds(k0, tk), g * HEAD_DIM:(g + 1) * HEAD_DIM], qts[g])

    def softplus_sums(diagonal):
        totals = []
        for g in heads:
            logit = l_scr[g]
            neg_abs = pltpu.bitcast(pltpu.bitcast(logit, jnp.uint32) | jnp.uint32(0x80000000), F32)
            sp = jnp.maximum(logit, 0.0) + jnp.log(1.0 + jnp.exp2(neg_abs)) * LOG2E
            if diagonal:
                ok = (lax.broadcasted_iota(jnp.int32, (tk, tq), 0) < lax.broadcasted_iota(jnp.int32, (tk, tq), 1))
                sp = jnp.where(ok, sp, 0.0)
                logit = jnp.where(ok, logit, NEG_BIG)
            d_scr[g] = logit
            hi = sp.astype(BF16)
            lo = (sp - hi.astype(F32)).astype(BF16)
            tot = []
            for c in range(n_sub):
                rows = slice(c * LANES, (c + 1) * LANES)
                within = _dot(tri, jnp.concatenate([hi[rows], lo[rows]], axis=0))
                w_scr[g, rows, :] = within
                tot.append(within[0:1])
            totals.append(tuple(tot))
        return tuple(totals)

    def weigh_values(t, laters, totals):
        k0 = key0(t)
        out = []
        for g in heads:
            later = laters[g]
            parts = [None] * n_sub
            for c in reversed(range(n_sub)):
                rows = slice(c * LANES, (c + 1) * LANES)
                parts[c] = jnp.exp2(d_scr[g, rows, :] - w_scr[g, rows, :] - later)
                later = later + totals[g][c]
            a = jnp.concatenate(parts, axis=0).astype(BF16)
            acc_scr[g] += _dot(vt_scr[g, :, pl.ds(k0, tk)], a)
            out.append(later)
        return tuple(out)

    acc_scr[...] = jnp.zeros_like(acc_scr)
    logits(0)
    totals = softplus_sums(True)
    logits(1)

    def body(t, carry):
        laters, totals = carry
        laters = weigh_values(t - 1, laters, totals)
        totals = softplus_sums(False)
        logits(t + 1)
        return laters, totals

    laters, totals = lax.fori_loop(1, i + 1, body, ((jnp.zeros((1, tq), F32),) * hp, totals))
    weigh_values(i, laters, totals)
    for g in heads:
        cols = slice(g * HEAD_DIM, (g + 1) * HEAD_DIM)
        z = z_ref[0, :, cols]
        o_ref[0, :, cols] = (acc_scr[g].T * (z * _sigmoid(z))).astype(o_ref.dtype)


def _sb_attention(q, kv, z, tri, *, batch, seq, tq=512, hp=4):
    H = N_HEADS
    tq = min(tq, seq)
    wid = hp * HEAD_DIM
    return pl.pallas_call(
        functools.partial(_sb_kernel, tq=tq, hp=hp),
        grid=(batch, H // hp, seq // tq),
        in_specs=[
            pl.BlockSpec((1, tq, wid), lambda b, h, i: (b, i, h)),
            pl.BlockSpec((1, seq, wid), lambda b, h, i: (b, 0, h)),
            pl.BlockSpec((1, seq, wid), lambda b, h, i: (b, 0, H // hp + h)),
            pl.BlockSpec((1, tq, wid), lambda b, h, i: (b, i, h)),
            pl.BlockSpec(tri.shape, lambda b, h, i: (0, 0)),
        ],
        out_specs=pl.BlockSpec((1, tq, wid), lambda b, h, i: (b, i, h)),
        out_shape=jax.ShapeDtypeStruct((batch, seq, H * HEAD_DIM), BF16),
        scratch_shapes=[pltpu.VMEM((hp, HEAD_DIM, seq), BF16), pltpu.VMEM((hp, tq, tq), F32),
                        pltpu.VMEM((hp, tq, tq), F32), pltpu.VMEM((hp, tq, tq), F32),
                        pltpu.VMEM((hp, HEAD_DIM, tq), F32)],
        compiler_params=_cparams(("parallel", "parallel", "arbitrary")),
        name="sb_attention",
    )(q, kv, kv, z, tri)


def _static_tables(seq):
    QB = LANES
    nch = seq // CMP_STRIDE
    n_slc = seq // SLC_BLOCK
    assert nch == LANES and n_slc <= LANES
    q = np.arange(QB)[None, :]
    rows = np.arange(seq)[:, None]
    dist_c = (rows // QB) * QB + q - ((rows % QB) * CMP_STRIDE + CMP_BLOCK - 1)
    idx_c = np.where(dist_c >= 0, _t5_bucket_np(dist_c), MASKED_BUCKET).astype(np.int32)
    k = np.arange(QB)[:, None]
    far = WINDOW // QB
    tiles = []
    for delta, keep in ((0, k <= q), (1, None), (2, None), (far, k > q)):
        idx = _t5_bucket_np(delta * QB + q - k)
        tiles.append(idx if keep is None else np.where(keep, idx, MASKED_BUCKET))
    idx_t = np.concatenate(tiles, axis=0).astype(np.int32)
    cmp_start = np.arange(LANES) * CMP_STRIDE
    slc_start = np.arange(LANES) * SLC_BLOCK
    ovl = ((cmp_start[None, :] < slc_start[:, None] + SLC_BLOCK)
           & (cmp_start[None, :] + CMP_BLOCK - 1 >= slc_start[:, None]))
    ovl = ovl & (np.arange(LANES)[None, :] < nch - 1) & (np.arange(LANES)[:, None] < n_slc)
    return idx_c, idx_t, ovl.astype(np.float32)


def _nsa_layer(xf, tabs, norm, w_in, cmp_pos, kw1, kw2, vw1, vw2, w_out, *, batch, seq):
    bias_c, bias_t, ovl = tabs
    H, G, Dh = N_HEADS, N_KV_GROUPS, HEAD_DIM
    HD, GD = H * Dh, G * Dh
    n_qkv = HD + 6 * GD
    hn, = _rmsnorm(xf, [norm], BF16)
    qscale = jnp.concatenate([jnp.full((1, HD), LOG2E / math.sqrt(Dh), F32), jnp.ones((1, 6 * GD), F32)], axis=1)
    w_in_t = w_in.T
    qkv = _matmul(hn, w_in_t, BF16, n=n_qkv, colscale=qscale, w_rows=True)
    assert 3 * H <= LANES
    gl = _matmul(hn, w_in_t, F32, col0=n_qkv, n=LANES, w_rows=True)
    z = _matmul(hn, w_in_t, F32, col0=n_qkv + 3 * H, n=3 * HD, w_rows=True)

    nch = seq // CMP_STRIDE
    cmp_in = qkv[:, HD:HD + 2 * GD].reshape(batch, nch, CMP_STRIDE * 2 * GD)
    kc, vct = _compress(cmp_in, cmp_pos.reshape(1, CMP_BLOCK * Dh),
                        kw1.astype(BF16), kw2.astype(BF16), vw1.astype(BF16), vw2.astype(BF16), groups=G)
    mixed = _nsa_attention(qkv.reshape(batch, seq, n_qkv), kc, vct, bias_c, bias_t,
                           gl.reshape(batch, seq, LANES), z.reshape(batch, seq, 3 * HD),
                           ovl, batch=batch, seq=seq)
    return _matmul(mixed.reshape(batch * seq, HD), w_out, F32, res=xf)


def _sb_layer(xf, hn, kv, tri, w_in, w_out, *, batch, seq):
    H, Dh = N_HEADS, HEAD_DIM
    HD = H * Dh
    q = _matmul(hn, w_in, BF16, n=HD, colscale=jnp.full((1, HD), LOG2E / math.sqrt(Dh), F32))
    z = _matmul(hn, w_in, F32, col0=HD, n=HD)
    o = _sb_attention(q.reshape(batch, seq, HD), kv, z.reshape(batch, seq, HD), tri, batch=batch, seq=seq)
    return _matmul(o.reshape(batch * seq, HD), w_out, F32, res=xf)


def kernel(x, rel_bias, a0_norm, a0_w_in, a0_cmp_pos, a0_cmp_k_w1, a0_cmp_k_w2, a0_cmp_v_w1, a0_cmp_v_w2, a0_w_out, a1_norm, a1_w_in, a1_cmp_pos, a1_cmp_k_w1, a1_cmp_k_w2, a1_cmp_v_w1, a1_cmp_v_w2, a1_w_out, kv_norm, w_kv, b2_norm, b2_w_in, b2_w_out, b3_norm, b3_w_in, b3_w_out, final_norm):
    batch, seq, d = x.shape
    HD = N_HEADS * HEAD_DIM
    xf = x.reshape(batch * seq, d)

    idx_c, idx_t, ovl = _static_tables(seq)
    rel_bias_ext = jnp.concatenate([rel_bias.T * LOG2E, jnp.full((N_HEADS, 1), NEG_BIG, F32)], axis=1)
    tabs = (_bias_table(rel_bias_ext, idx_c), _bias_table(rel_bias_ext, idx_t), jnp.asarray(ovl, BF16))

    xf = _nsa_layer(xf, tabs, a0_norm, a0_w_in, a0_cmp_pos, a0_cmp_k_w1, a0_cmp_k_w2, a0_cmp_v_w1, a0_cmp_v_w2,
                    a0_w_out, batch=batch, seq=seq)
    xf = _nsa_layer(xf, tabs, a1_norm, a1_w_in, a1_cmp_pos, a1_cmp_k_w1, a1_cmp_k_w2, a1_cmp_v_w1, a1_cmp_v_w2,
                    a1_w_out, batch=batch, seq=seq)

    hn_kv, hn_b2 = _rmsnorm(xf, [kv_norm, b2_norm], BF16)
    kv = _matmul(hn_kv, w_kv, BF16).reshape(batch, seq, 2 * HD)
    m = np.arange(LANES)[None, :] >= np.arange(LANES)[:, None]
    tri = jnp.asarray(np.concatenate([m, m], axis=1), BF16)
    xf = _sb_layer(xf, hn_b2, kv, tri, b2_w_in, b2_w_out, batch=batch, seq=seq)
    hn_b3, = _rmsnorm(xf, [b3_norm], BF16)
    xf = _sb_layer(xf, hn_b3, kv, tri, b3_w_in, b3_w_out, batch=batch, seq=seq)

    out, = _rmsnorm(xf, [final_norm], F32)
    return out.reshape(batch, seq, d)
```

```python
import functools
import math

import numpy as np
import jax
import jax.numpy as jnp
from jax import lax
from jax.experimental import pallas as pl
from jax.experimental.pallas import tpu as pltpu

N_HEADS = 32
HEAD_DIM = 128
N_KV_GROUPS = 4
CMP_BLOCK = 32
CMP_STRIDE = 16
SLC_BLOCK = 64
N_SELECT = 16
WINDOW = 512
N_BUCKETS = 32
MAX_DISTANCE = 128
RMS_EPS = 1e-6
FORCE_SCORE = 1e6

LANES = 128
NEG_BIG = -1e30
VMEM_LIMIT = 62 * 1024 * 1024
LOG2E = math.log2(math.e)

F32 = jnp.float32
BF16 = jnp.bfloat16


def _cparams(sem):
    return pltpu.CompilerParams(dimension_semantics=sem, vmem_limit_bytes=VMEM_LIMIT)


def _dot(a, b):
    return jnp.dot(a, b, preferred_element_type=F32)


def _sigmoid(x):
    return 0.5 * jnp.tanh(0.5 * x) + 0.5


def _rmsnorm_kernel(x_ref, *refs):
    n_out = len(refs) // 2
    x = x_ref[...]
    ms = jnp.mean(x * x, axis=-1, keepdims=True)
    y = x * lax.rsqrt(ms + RMS_EPS)
    for g_ref, o_ref in zip(refs[:n_out], refs[n_out:]):
        o_ref[...] = (y * g_ref[...]).astype(o_ref.dtype)


def _rmsnorm(x, gains, out_dtype, tm=512):
    m, d = x.shape
    tm = min(tm, m)
    row = pl.BlockSpec((tm, d), lambda i: (i, 0))
    outs = pl.pallas_call(
        _rmsnorm_kernel,
        grid=(m // tm,),
        in_specs=[row] + [pl.BlockSpec((1, d), lambda i: (0, 0))] * len(gains),
        out_specs=[row] * len(gains),
        out_shape=[jax.ShapeDtypeStruct((m, d), out_dtype)] * len(gains),
        compiler_params=_cparams(("parallel",)),
        name="rmsnorm",
    )(x, *[g.reshape(1, d) for g in gains])
    return outs


W_SLABS = 4


def _matmul_kernel(*refs, has_res, has_scale, cast_w, shift, w_rows):
    it = iter(refs)
    x_ref = next(it)
    w_refs = [next(it) for _ in range(W_SLABS if cast_w else 1)]
    wn_ref = next(it) if shift else None
    s_ref = next(it) if has_scale else None
    r_ref = next(it) if has_res else None
    o_ref = next(it)
    if cast_w:
        wb_ref = next(it)

        @pl.when(pl.program_id(1) == 0)
        def _():
            ks = wb_ref.shape[0] // W_SLABS
            for s, w_ref in enumerate(w_refs):
                w = w_ref[...]
                if shift:
                    ax = 0 if w_rows else 1
                    nxt = wn_ref[:, s * ks:(s + 1) * ks] if w_rows else wn_ref[s * ks:(s + 1) * ks, :]
                    w = jnp.concatenate([lax.slice_in_dim(w, shift, w.shape[ax], axis=ax),
                                         lax.slice_in_dim(nxt, 0, shift, axis=ax)], axis=ax)
                if w_rows:
                    w = w.T
                if has_scale:
                    w = w * s_ref[...]
                wb_ref[s * ks:(s + 1) * ks, :] = w.astype(BF16)
    else:
        wb_ref = w_refs[0]
    acc = _dot(x_ref[...], wb_ref[...])
    if has_res:
        acc = r_ref[...] + acc
    o_ref[...] = acc.astype(o_ref.dtype)


def _matmul(x, w, out_dtype, *, col0=0, n=None, res=None, colscale=None, w_rows=False, tm=None, tn=None):
    m, k = x.shape
    n_total = w.shape[0] if w_rows else w.shape[1]
    n = n_total - col0 if n is None else n
    if tm is None:
        tm, tn = (1024, 512) if res is not None else (512, 1024)
    tm, tn = min(tm, m), min(tn, n)
    cast_w = w.dtype != BF16
    shift = col0 % LANES
    base = col0 - shift
    assert m % tm == 0 and n % tn == 0 and base % tn == 0 and tn % LANES == 0
    assert cast_w or (colscale is None and shift == 0 and not w_rows)
    joff = base // tn
    per = tn // LANES
    n_i = m // tm
    in_specs = [pl.BlockSpec((tm, k), lambda j, i: (i, 0))]
    args = [x]
    if cast_w:
        assert k % (W_SLABS * LANES) == 0
        ks = k // W_SLABS
        stagger = n_i > W_SLABS
        for s in range(W_SLABS):
            def tile(j, i, s=s):
                ahead = (i > s).astype(jnp.int32) if stagger else 0
                return jnp.minimum(j + ahead, n // tn - 1) + joff
            if w_rows:
                in_specs.append(pl.BlockSpec((tn, ks), lambda j, i, s=s, tile=tile: (tile(j, i), s)))
            else:
                in_specs.append(pl.BlockSpec((ks, tn), lambda j, i, s=s, tile=tile: (s, tile(j, i))))
            args.append(w)
    else:
        in_specs.append(pl.BlockSpec((k, tn), lambda j, i: (0, j + joff)))
        args.append(w)
    if shift:
        if w_rows:
            assert shift % 8 == 0
            in_specs.append(pl.BlockSpec((LANES, k), lambda j, i: ((j + joff + 1) * per, 0)))
        else:
            in_specs.append(pl.BlockSpec((k, LANES), lambda j, i: (0, (j + joff + 1) * per)))
        args.append(w)
    if colscale is not None:
        in_specs.append(pl.BlockSpec((1, tn), lambda j, i: (0, j)))
        args.append(colscale)
    if res is not None:
        in_specs.append(pl.BlockSpec((tm, tn), lambda j, i: (i, j)))
        args.append(res)
    return pl.pallas_call(
        functools.partial(_matmul_kernel, has_res=res is not None, has_scale=colscale is not None,
                          cast_w=cast_w, shift=shift, w_rows=w_rows),
        grid=(n // tn, n_i),
        in_specs=in_specs,
        out_specs=pl.BlockSpec((tm, tn), lambda j, i: (i, j)),
        out_shape=jax.ShapeDtypeStruct((m, n), out_dtype),
        scratch_shapes=[pltpu.VMEM((k, tn), BF16)] if cast_w else [],
        compiler_params=_cparams(("arbitrary", "arbitrary")),
        name="matmul_res" if res is not None else "matmul",
    )(*args)


MASKED_BUCKET = N_BUCKETS


def _t5_bucket_np(dist):
    max_exact = N_BUCKETS // 2
    d = np.maximum(dist, 0)
    log_ratio = np.log(np.maximum(d, max_exact).astype(np.float32) / np.float32(max_exact))
    large = max_exact + (log_ratio / np.float32(math.log(MAX_DISTANCE / max_exact))
                         * np.float32(N_BUCKETS - max_exact)).astype(np.int32)
    return np.where(d < max_exact, d, np.minimum(large, N_BUCKETS - 1)).astype(np.int32)


def _bias_kernel(rb_ref, idx_ref, o_ref):
    h = pl.program_id(0)
    idx = idx_ref[...]
    acc = jnp.zeros(idx.shape, F32)
    for b in range(N_BUCKETS + 1):
        acc = jnp.where(idx == b, rb_ref[h, b], acc)
    o_ref[...] = acc


def _bias_table(rel_bias_ext, bucket_idx):
    h = rel_bias_ext.shape[0]
    rows = bucket_idx.shape[0]
    return pl.pallas_call(
        _bias_kernel,
        grid=(h,),
        in_specs=[pl.BlockSpec(memory_space=pltpu.SMEM),
                  pl.BlockSpec((rows, LANES), lambda i: (0, 0))],
        out_specs=pl.BlockSpec((rows, LANES), lambda i: (0, i)),
        out_shape=jax.ShapeDtypeStruct((rows, h * LANES), F32),
        compiler_params=_cparams(("arbitrary",)),
        name="bias_table",
    )(rel_bias_ext, jnp.asarray(bucket_idx))


def _compress_kernel(kv_ref, pos_ref, kw1_ref, kw2_ref, vw1_ref, vw2_ref, kc_ref, vct_ref, *, groups):
    half = pos_ref.shape[1] // 2
    nch = kv_ref.shape[1]
    gd = groups * HEAD_DIM
    g = pl.program_id(1)
    row = lax.broadcasted_iota(jnp.int32, (nch, HEAD_DIM), 0)

    def one(col, w1_ref, w2_ref):
        x = jnp.concatenate(
            [kv_ref[0, :, pl.ds(pl.multiple_of(o * 2 * gd + col + g * HEAD_DIM, HEAD_DIM), HEAD_DIM)]
             for o in range(CMP_STRIDE)], axis=1).astype(F32)
        lo = (x + pos_ref[:, :half]).astype(BF16)
        hi = pltpu.roll(x + pos_ref[:, half:], nch - 1, 0).astype(BF16)
        h = _dot(lo, w1_ref[:half, :]) + _dot(hi, w1_ref[half:, :])
        h = h * _sigmoid(h)
        o = _dot(h.astype(BF16), w2_ref[...])
        return jnp.where(row < nch - 1, o, 0.0)

    kc_ref[0, 0] = one(0, kw1_ref, kw2_ref).astype(kc_ref.dtype)
    vct_ref[0, 0] = one(gd, vw1_ref, vw2_ref).T.astype(vct_ref.dtype)


def _compress(kv, pos, kw1, kw2, vw1, vw2, *, groups):
    b, nch, wid = kv.shape
    oblk = pl.BlockSpec((1, 1, nch, HEAD_DIM), lambda i, j: (i, j, 0, 0))

    def const(a):
        return pl.BlockSpec(a.shape, lambda i, j: (0,) * a.ndim)

    out = jax.ShapeDtypeStruct((b, groups, nch, HEAD_DIM), BF16)
    return pl.pallas_call(
        functools.partial(_compress_kernel, groups=groups),
        grid=(b, groups),
        in_specs=[pl.BlockSpec((1, nch, wid), lambda i, j: (i, 0, 0)),
                  const(pos), const(kw1), const(kw2), const(vw1), const(vw2)],
        out_specs=[oblk, oblk],
        out_shape=[out, out],
        compiler_params=_cparams(("parallel", "arbitrary")),
        name="compress",
    )(kv, pos, kw1, kw2, vw1, vw2)


def _split3(x):
    a = x.astype(BF16)
    r = x - a.astype(F32)
    b = r.astype(BF16)
    c = (r - b.astype(F32)).astype(BF16)
    return a, b, c


def _nsa_kernel(q_ref, kc_ref, vct_ref, ks_ref, vs_ref, kw_ref, vw_ref, bc_ref, bt_ref, gl_ref,
                zc_ref, zs_ref, zw_ref, ovl_ref, o_ref, sel_scr, acc_scr, s_scr, p_scr, acc2_scr, s2_scr, p2_scr,
                vst_scr, vwt_scr, gate_scr,
                *, group, n_slc, n_sel):
    R = group
    QB = LANES
    W = R * QB
    i = pl.program_id(2)
    q0 = i * QB

    @pl.when(i == 0)
    def _():
        for c in range(vs_ref.shape[1] // QB):
            cols = slice(c * QB, (c + 1) * QB)
            vst_scr[:, cols] = vs_ref[0, cols, :].T
            vwt_scr[:, cols] = vw_ref[0, cols, :].T

    q = q_ref[0]
    qt = jnp.concatenate([q[:, r * HEAD_DIM:(r + 1) * HEAD_DIM].T for r in range(R)], axis=1)

    s = _dot(kc_ref[0, 0], qt) + bc_ref[...]
    m = jnp.max(s, axis=0, keepdims=True)
    e = jnp.exp2(s - m)
    denom = jnp.maximum(jnp.sum(e, axis=0, keepdims=True), 1e-30)
    p_c = e * jnp.where(m > 0.5 * NEG_BIG, 1.0 / denom, 0.0)
    o_c = _dot(vct_ref[0, 0], p_c.astype(BF16))

    psum = p_c[:, 0:QB]
    for r in range(1, R):
        psum = psum + p_c[:, r * QB:(r + 1) * QB]
    ovl = ovl_ref[...]
    imp = sum(_dot(ovl, part) for part in _split3(psum))[:n_slc]
    j = lax.broadcasted_iota(jnp.int32, (n_slc, QB), 0)
    t = q0 + lax.broadcasted_iota(jnp.int32, (n_slc, QB), 1)
    cur = t // SLC_BLOCK
    forced = (j == 0) | (j == cur) | (j == cur - 1)
    score = jnp.where(forced, FORCE_SCORE, jnp.where(j * SLC_BLOCK <= t, imp, -1.0))
    rank = jnp.zeros((n_slc, QB), F32)
    for jp in range(n_slc):
        other = score[jp:jp + 1, :]
        beats = (other > score) | ((other == score) & (j > jp))
        rank = rank + jnp.where(beats, 1.0, 0.0)
    sel_scr[...] = jnp.where(rank < n_sel, 0.0, NEG_BIG)

    far = WINDOW // QB
    lo_w = jnp.maximum(i - far, 0)

    def tile_slice(kt):
        return pl.ds(pl.multiple_of(kt * QB, QB), QB)

    def bias(tile):
        return bt_ref[tile_slice(tile), :]

    class Chain:
        def __init__(self, k_ref, vt_ref, s_ref, p_ref, a_ref, block_mask):
            self.k_ref, self.vt_ref, self.s_ref, self.p_ref, self.a_ref = k_ref, vt_ref, s_ref, p_ref, a_ref
            self.block_mask = block_mask

        def raw_scores(self, kt):
            return _dot(self.k_ref[0, tile_slice(kt), :], qt)

        def start(self, first):
            s = self.raw_scores(i) + bias(0)
            m = jnp.max(s, axis=0, keepdims=True)
            p = jnp.exp2(s - m)
            l = jnp.sum(p, axis=0, keepdims=True)
            self.a_ref[...] = jnp.zeros_like(self.a_ref)
            self.p_ref[...] = p.astype(BF16)
            self.s_ref[...] = self.raw_scores(first)
            return m, l, jnp.ones((1, W), F32)

        def accumulate(self, kt, alpha):
            self.a_ref[...] = alpha * self.a_ref[...] + _dot(self.vt_ref[:, tile_slice(kt)], self.p_ref[...])

        def step(self, carry, prev, cur, nxt, tile):
            m, l, alpha_prev = carry
            self.accumulate(prev, alpha_prev)
            s = self.s_ref[...] + bias(tile)
            if self.block_mask:
                per = QB // SLC_BLOCK
                add = jnp.concatenate(
                    [jnp.broadcast_to(sel_scr[pl.ds(cur * per + c, 1), :], (SLC_BLOCK, QB)) for c in range(per)],
                    axis=0)
                s = s + jnp.concatenate([add] * R, axis=1)
            m_new = jnp.maximum(m, jnp.max(s, axis=0, keepdims=True))
            alpha = jnp.exp2(m - m_new)
            p = jnp.exp2(s - m_new)
            l = alpha * l + jnp.sum(p, axis=0, keepdims=True)
            self.p_ref[...] = p.astype(BF16)
            self.s_ref[...] = self.raw_scores(nxt)
            return m_new, l, alpha

        def finish(self, carry, last):
            _, l, alpha = carry
            self.accumulate(last, alpha)
            return self.a_ref[...] * (1.0 / l)

    sel = Chain(ks_ref, vst_scr, s_scr, p_scr, acc_scr, True)
    win = Chain(kw_ref, vwt_scr, s2_scr, p2_scr, acc2_scr, False)
    carry_s = sel.start(lo_w)
    carry_w = win.start(lo_w)

    def both(kt, carry):
        carry_s, carry_w = carry
        prev = jnp.where(kt == lo_w, i, kt - 1)
        d = i - kt
        carry_s = sel.step(carry_s, prev, kt, jnp.where(kt == i - 1, 0, kt + 1), jnp.minimum(d, 2))
        carry_w = win.step(carry_w, prev, kt, jnp.minimum(kt + 1, i), jnp.where(d == far, 3, jnp.minimum(d, 2)))
        return carry_s, carry_w

    carry_s, carry_w = lax.fori_loop(lo_w, i, both, (carry_s, carry_w))
    o_w = win.finish(carry_w, jnp.where(i > 0, i - 1, i))

    def sel_only(kt, carry):
        return sel.step(carry, jnp.where(kt == 0, i - 1, kt - 1), kt, jnp.minimum(kt + 1, i), 2)

    carry_s = lax.fori_loop(0, lo_w, sel_only, carry_s)
    o_s = sel.finish(carry_s, jnp.where(lo_w > 0, lo_w - 1, jnp.where(i > 0, i - 1, i)))

    n_heads = R * pl.num_programs(0)
    head0 = pl.program_id(0) * R
    gate_scr[...] = _sigmoid(gl_ref[0]).T
    for r in range(R):
        cols = slice(r * QB, (r + 1) * QB)
        mixed = jnp.zeros((QB, HEAD_DIM), F32)
        for c, (o, z_ref) in enumerate(((o_c, zc_ref), (o_s, zs_ref), (o_w, zw_ref))):
            z = z_ref[0, :, cols]
            gated = o[:, cols] * gate_scr[pl.ds(c * n_heads + head0 + r, 1), :]
            mixed = mixed + gated.T * (z * _sigmoid(z))
        o_ref[0, :, cols] = mixed.astype(o_ref.dtype)


def _nsa_attention(qkv, kc, vct, bias_c, bias_t, gl, z, ovl, *, batch, seq):
    G, R = N_KV_GROUPS, N_HEADS // N_KV_GROUPS
    QB = LANES
    W = R * QB
    nqb = seq // QB
    n_slc = seq // SLC_BLOCK
    n_sel = min(N_SELECT, n_slc)

    def kv_spec(which):
        return pl.BlockSpec((1, seq, HEAD_DIM), lambda g, b, i: (b, 0, N_HEADS + which * G + g))

    def z_spec(c):
        return pl.BlockSpec((1, QB, R * HEAD_DIM), lambda g, b, i: (b, i, c * G + g))

    cmp_spec = pl.BlockSpec((1, 1, LANES, HEAD_DIM), lambda g, b, i: (b, g, 0, 0))
    in_specs = [
        pl.BlockSpec((1, QB, R * HEAD_DIM), lambda g, b, i: (b, i, g)),
        cmp_spec, cmp_spec,
        kv_spec(2), kv_spec(3), kv_spec(4), kv_spec(5),
        pl.BlockSpec((QB, W), lambda g, b, i: (i, g)),
        pl.BlockSpec((4 * QB, W), lambda g, b, i: (0, g)),
        pl.BlockSpec((1, QB, LANES), lambda g, b, i: (b, i, 0)),
        z_spec(0), z_spec(1), z_spec(2),
        pl.BlockSpec(ovl.shape, lambda g, b, i: (0, 0)),
    ]
    return pl.pallas_call(
        functools.partial(_nsa_kernel, group=R, n_slc=n_slc, n_sel=n_sel),
        grid=(G, batch, nqb),
        in_specs=in_specs,
        out_specs=pl.BlockSpec((1, QB, R * HEAD_DIM), lambda g, b, i: (b, i, g)),
        out_shape=jax.ShapeDtypeStruct((batch, seq, N_HEADS * HEAD_DIM), BF16),
        scratch_shapes=[pltpu.VMEM((n_slc, QB), F32), pltpu.VMEM((HEAD_DIM, W), F32),
                        pltpu.VMEM((QB, W), F32), pltpu.VMEM((QB, W), BF16),
                        pltpu.VMEM((HEAD_DIM, W), F32), pltpu.VMEM((QB, W), F32), pltpu.VMEM((QB, W), BF16),
                        pltpu.VMEM((HEAD_DIM, seq), BF16), pltpu.VMEM((HEAD_DIM, seq), BF16),
                        pltpu.VMEM((LANES, QB), F32)],
        compiler_params=_cparams(("parallel", "parallel", "arbitrary")),
        name="nsa_attention",
    )(qkv, kc, vct, qkv, qkv, qkv, qkv, bias_c, bias_t, gl, z, z, z, ovl)


def _sb_kernel(q_ref, k_ref, v_ref, z_ref, tri_ref, o_ref, vt_scr, l_scr, d_scr, w_scr, acc_scr, *, tq, hp):
    i = pl.program_id(2)
    tk = tq
    tri = tri_ref[...]
    n_sub = tk // LANES
    heads = range(hp)

    @pl.when(i == 0)
    def _():
        for g in heads:
            for c in range(v_ref.shape[1] // LANES):
                cols = slice(c * LANES, (c + 1) * LANES)
                vt_scr[g, :, cols] = v_ref[0, cols, g * HEAD_DIM:(g + 1) * HEAD_DIM].T

    qts = [q_ref[0, :, g * HEAD_DIM:(g + 1) * HEAD_DIM].T for g in heads]

    def key0(t):
        return pl.multiple_of(jnp.maximum(i - t, 0) * tk, tk)

    def logits(t):
        k0 = key0(t)
        for g in heads:
            l_scr[g] = _dot(k_ref[0, pl.ds(k0, tk), g * HEAD_DIM:(g + 1) * HEAD_DIM], qts[g])

    def softplus_sums(diagonal):
        totals = []
        for g in heads:
            logit = l_scr[g]
            neg_abs = pltpu.bitcast(pltpu.bitcast(logit, jnp.uint32) | jnp.uint32(0x80000000), F32)
            sp = jnp.maximum(logit, 0.0) + jnp.log(1.0 + jnp.exp2(neg_abs)) * LOG2E
            if diagonal:
                ok = (lax.broadcasted_iota(jnp.int32, (tk, tq), 0) < lax.broadcasted_iota(jnp.int32, (tk, tq), 1))
                sp = jnp.where(ok, sp, 0.0)
                logit = jnp.where(ok, logit, NEG_BIG)
            d_scr[g] = logit
            hi = sp.astype(BF16)
            lo = (sp - hi.astype(F32)).astype(BF16)
            tot = []
            for c in range(n_sub):
                rows = slice(c * LANES, (c + 1) * LANES)
                within = _dot(tri, jnp.concatenate([hi[rows], lo[rows]], axis=0))
                w_scr[g, rows, :] = within
                tot.append(within[0:1])
            totals.append(tuple(tot))
        return tuple(totals)

    def weigh_values(t, laters, totals):
        k0 = key0(t)
        out = []
        for g in heads:
            later = laters[g]
            parts = [None] * n_sub
            for c in reversed(range(n_sub)):
                rows = slice(c * LANES, (c + 1) * LANES)
                parts[c] = jnp.exp2(d_scr[g, rows, :] - w_scr[g, rows, :] - later)
                later = later + totals[g][c]
            a = jnp.concatenate(parts, axis=0).astype(BF16)
            acc_scr[g] += _dot(vt_scr[g, :, pl.ds(k0, tk)], a)
            out.append(later)
        return tuple(out)

    acc_scr[...] = jnp.zeros_like(acc_scr)
    logits(0)
    totals = softplus_sums(True)
    logits(1)

    def body(t, carry):
        laters, totals = carry
        laters = weigh_values(t - 1, laters, totals)
        totals = softplus_sums(False)
        logits(t + 1)
        return laters, totals

    laters, totals = lax.fori_loop(1, i + 1, body, ((jnp.zeros((1, tq), F32),) * hp, totals))
    weigh_values(i, laters, totals)
    for g in heads:
        cols = slice(g * HEAD_DIM, (g + 1) * HEAD_DIM)
        z = z_ref[0, :, cols]
        o_ref[0, :, cols] = (acc_scr[g].T * (z * _sigmoid(z))).astype(o_ref.dtype)


def _sb_attention(q, kv, z, tri, *, batch, seq, tq=512, hp=4):
    H = N_HEADS
    tq = min(tq, seq)
    wid = hp * HEAD_DIM
    return pl.pallas_call(
        functools.partial(_sb_kernel, tq=tq, hp=hp),
        grid=(batch, H // hp, seq // tq),
        in_specs=[
            pl.BlockSpec((1, tq, wid), lambda b, h, i: (b, i, h)),
            pl.BlockSpec((1, seq, wid), lambda b, h, i: (b, 0, h)),
            pl.BlockSpec((1, seq, wid), lambda b, h, i: (b, 0, H // hp + h)),
            pl.BlockSpec((1, tq, wid), lambda b, h, i: (b, i, h)),
            pl.BlockSpec(tri.shape, lambda b, h, i: (0, 0)),
        ],
        out_specs=pl.BlockSpec((1, tq, wid), lambda b, h, i: (b, i, h)),
        out_shape=jax.ShapeDtypeStruct((batch, seq, H * HEAD_DIM), BF16),
        scratch_shapes=[pltpu.VMEM((hp, HEAD_DIM, seq), BF16), pltpu.VMEM((hp, tq, tq), F32),
                        pltpu.VMEM((hp, tq, tq), F32), pltpu.VMEM((hp, tq, tq), F32),
                        pltpu.VMEM((hp, HEAD_DIM, tq), F32)],
        compiler_params=_cparams(("parallel", "parallel", "arbitrary")),
        name="sb_attention",
    )(q, kv, kv, z, tri)


def _static_tables(seq):
    QB = LANES
    nch = seq // CMP_STRIDE
    n_slc = seq // SLC_BLOCK
    assert nch == LANES and n_slc <= LANES
    q = np.arange(QB)[None, :]
    rows = np.arange(seq)[:, None]
    dist_c = (rows // QB) * QB + q - ((rows % QB) * CMP_STRIDE + CMP_BLOCK - 1)
    idx_c = np.where(dist_c >= 0, _t5_bucket_np(dist_c), MASKED_BUCKET).astype(np.int32)
    k = np.arange(QB)[:, None]
    far = WINDOW // QB
    tiles = []
    for delta, keep in ((0, k <= q), (1, None), (2, None), (far, k > q)):
        idx = _t5_bucket_np(delta * QB + q - k)
        tiles.append(idx if keep is None else np.where(keep, idx, MASKED_BUCKET))
    idx_t = np.concatenate(tiles, axis=0).astype(np.int32)
    cmp_start = np.arange(LANES) * CMP_STRIDE
    slc_start = np.arange(LANES) * SLC_BLOCK
    ovl = ((cmp_start[None, :] < slc_start[:, None] + SLC_BLOCK)
           & (cmp_start[None, :] + CMP_BLOCK - 1 >= slc_start[:, None]))
    ovl = ovl & (np.arange(LANES)[None, :] < nch - 1) & (np.arange(LANES)[:, None] < n_slc)
    return idx_c, idx_t, ovl.astype(np.float32)


def _nsa_layer(xf, tabs, norm, w_in, cmp_pos, kw1, kw2, vw1, vw2, w_out, *, batch, seq):
    bias_c, bias_t, ovl = tabs
    H, G, Dh = N_HEADS, N_KV_GROUPS, HEAD_DIM
    HD, GD = H * Dh, G * Dh
    n_qkv = HD + 6 * GD
    hn, = _rmsnorm(xf, [norm], BF16)
    qscale = jnp.concatenate([jnp.full((1, HD), LOG2E / math.sqrt(Dh), F32), jnp.ones((1, 6 * GD), F32)], axis=1)
    w_in_t = w_in.T
    qkv = _matmul(hn, w_in_t, BF16, n=n_qkv, colscale=qscale, w_rows=True)
    assert 3 * H <= LANES
    gl = _matmul(hn, w_in_t, F32, col0=n_qkv, n=LANES, w_rows=True)
    z = _matmul(hn, w_in_t, F32, col0=n_qkv + 3 * H, n=3 * HD, w_rows=True)

    nch = seq // CMP_STRIDE
    cmp_in = qkv[:, HD:HD + 2 * GD].reshape(batch, nch, CMP_STRIDE * 2 * GD)
    kc, vct = _compress(cmp_in, cmp_pos.reshape(1, CMP_BLOCK * Dh),
                        kw1.astype(BF16), kw2.astype(BF16), vw1.astype(BF16), vw2.astype(BF16), groups=G)
    mixed = _nsa_attention(qkv.reshape(batch, seq, n_qkv), kc, vct, bias_c, bias_t,
                           gl.reshape(batch, seq, LANES), z.reshape(batch, seq, 3 * HD),
                           ovl, batch=batch, seq=seq)
    return _matmul(mixed.reshape(batch * seq, HD), w_out, F32, res=xf)


def _sb_layer(xf, hn, kv, tri, w_in, w_out, *, batch, seq):
    H, Dh = N_HEADS, HEAD_DIM
    HD = H * Dh
    q = _matmul(hn, w_in, BF16, n=HD, colscale=jnp.full((1, HD), LOG2E / math.sqrt(Dh), F32))
    z = _matmul(hn, w_in, F32, col0=HD, n=HD)
    o = _sb_attention(q.reshape(batch, seq, HD), kv, z.reshape(batch, seq, HD), tri, batch=batch, seq=seq)
    return _matmul(o.reshape(batch * seq, HD), w_out, F32, res=xf)


def kernel(x, rel_bias, a0_norm, a0_w_in, a0_cmp_pos, a0_cmp_k_w1, a0_cmp_k_w2, a0_cmp_v_w1, a0_cmp_v_w2, a0_w_out, a1_norm, a1_w_in, a1_cmp_pos, a1_cmp_k_w1, a1_cmp_k_w2, a1_cmp_v_w1, a1_cmp_v_w2, a1_w_out, kv_norm, w_kv, b2_norm, b2_w_in, b2_w_out, b3_norm, b3_w_in, b3_w_out, final_norm):
    batch, seq, d = x.shape
    HD = N_HEADS * HEAD_DIM
    xf = x.reshape(batch * seq, d)

    idx_c, idx_t, ovl = _static_tables(seq)
    rel_bias_ext = jnp.concatenate([rel_bias.T * LOG2E, jnp.full((N_HEADS, 1), NEG_BIG, F32)], axis=1)
    tabs = (_bias_table(rel_bias_ext, idx_c), _bias_table(rel_bias_ext, idx_t), jnp.asarray(ovl, BF16))

    xf = _nsa_layer(xf, tabs, a0_norm, a0_w_in, a0_cmp_pos, a0_cmp_k_w1, a0_cmp_k_w2, a0_cmp_v_w1, a0_cmp_v_w2,
                    a0_w_out, batch=batch, seq=seq)
    xf = _nsa_layer(xf, tabs, a1_norm, a1_w_in, a1_cmp_pos, a1_cmp_k_w1, a1_cmp_k_w2, a1_cmp_v_w1, a1_cmp_v_w2,
                    a1_w_out, batch=batch, seq=seq)

    hn_kv, hn_b2 = _rmsnorm(xf, [kv_norm, b2_norm], BF16)
    kv = _matmul(hn_kv, w_kv, BF16).reshape(batch, seq, 2 * HD)
    m = np.arange(LANES)[None, :] >= np.arange(LANES)[:, None]
    tri = jnp.asarray(np.concatenate([m, m], axis=1), BF16)
    xf = _sb_layer(xf, hn_b2, kv, tri, b2_w_in, b2_w_out, batch=batch, seq=seq)
    hn_b3, = _rmsnorm(xf, [b3_norm], BF16)
    xf = _sb_layer(xf, hn_b3, kv, tri, b3_w_in, b3_w_out, batch=batch, seq=seq)

    out, = _rmsnorm(xf, [final_norm], F32)
    return out.reshape(batch, seq, d)
```

```python
import functools
import math

import numpy as np
import jax
import jax.numpy as jnp
from jax import lax
from jax.experimental import pallas as pl
from jax.experimental.pallas import tpu as pltpu

N_HEADS = 32
HEAD_DIM = 128
N_KV_GROUPS = 4
CMP_BLOCK = 32
CMP_STRIDE = 16
SLC_BLOCK = 64
N_SELECT = 16
WINDOW = 512
N_BUCKETS = 32
MAX_DISTANCE = 128
RMS_EPS = 1e-6
FORCE_SCORE = 1e6

LANES = 128
NEG_BIG = -1e30
VMEM_LIMIT = 62 * 1024 * 1024
LOG2E = math.log2(math.e)

F32 = jnp.float32
BF16 = jnp.bfloat16


def _cparams(sem):
    return pltpu.CompilerParams(dimension_semantics=sem, vmem_limit_bytes=VMEM_LIMIT)


def _dot(a, b):
    return jnp.dot(a, b, preferred_element_type=F32)


def _sigmoid(x):
    return 0.5 * jnp.tanh(0.5 * x) + 0.5


def _rmsnorm_kernel(x_ref, *refs):
    n_out = len(refs) // 2
    x = x_ref[...]
    ms = jnp.mean(x * x, axis=-1, keepdims=True)
    y = x * lax.rsqrt(ms + RMS_EPS)
    for g_ref, o_ref in zip(refs[:n_out], refs[n_out:]):
        o_ref[...] = (y * g_ref[...]).astype(o_ref.dtype)


def _rmsnorm(x, gains, out_dtype, tm=512):
    m, d = x.shape
    tm = min(tm, m)
    row = pl.BlockSpec((tm, d), lambda i: (i, 0))
    outs = pl.pallas_call(
        _rmsnorm_kernel,
        grid=(m // tm,),
        in_specs=[row] + [pl.BlockSpec((1, d), lambda i: (0, 0))] * len(gains),
        out_specs=[row] * len(gains),
        out_shape=[jax.ShapeDtypeStruct((m, d), out_dtype)] * len(gains),
        compiler_params=_cparams(("parallel",)),
        name="rmsnorm",
    )(x, *[g.reshape(1, d) for g in gains])
    return outs


W_SLABS = 4


def _matmul_kernel(*refs, has_res, has_scale, cast_w, shift, w_rows):
    it = iter(refs)
    x_ref = next(it)
    w_refs = [next(it) for _ in range(W_SLABS if cast_w else 1)]
    wn_ref = next(it) if shift else None
    s_ref = next(it) if has_scale else None
    r_ref = next(it) if has_res else None
    o_ref = next(it)
    if cast_w:
        wb_ref = next(it)

        @pl.when(pl.program_id(1) == 0)
        def _():
            ks = wb_ref.shape[0] // W_SLABS
            for s, w_ref in enumerate(w_refs):
                w = w_ref[...]
                if shift:
                    ax = 0 if w_rows else 1
                    nxt = wn_ref[:, s * ks:(s + 1) * ks] if w_rows else wn_ref[s * ks:(s + 1) * ks, :]
                    w = jnp.concatenate([lax.slice_in_dim(w, shift, w.shape[ax], axis=ax),
                                         lax.slice_in_dim(nxt, 0, shift, axis=ax)], axis=ax)
                if w_rows:
                    w = w.T
                if has_scale:
                    w = w * s_ref[...]
                wb_ref[s * ks:(s + 1) * ks, :] = w.astype(BF16)
    else:
        wb_ref = w_refs[0]
    acc = _dot(x_ref[...], wb_ref[...])
    if has_res:
        acc = r_ref[...] + acc
    o_ref[...] = acc.astype(o_ref.dtype)


def _matmul(x, w, out_dtype, *, col0=0, n=None, res=None, colscale=None, w_rows=False, tm=None, tn=None):
    m, k = x.shape
    n_total = w.shape[0] if w_rows else w.shape[1]
    n = n_total - col0 if n is None else n
    if tm is None:
        tm, tn = 512, 1024
    tm, tn = min(tm, m), min(tn, n)
    cast_w = w.dtype != BF16
    shift = col0 % LANES
    base = col0 - shift
    assert m % tm == 0 and n % tn == 0 and base % tn == 0 and tn % LANES == 0
    assert cast_w or (colscale is None and shift == 0 and not w_rows)
    joff = base // tn
    per = tn // LANES
    n_i = m // tm
    in_specs = [pl.BlockSpec((tm, k), lambda j, i: (i, 0))]
    args = [x]
    if cast_w:
        assert k % (W_SLABS * LANES) == 0
        ks = k // W_SLABS
        stagger = n_i > W_SLABS
        for s in range(W_SLABS):
            def tile(j, i, s=s):
                ahead = (i > s).astype(jnp.int32) if stagger else 0
                return jnp.minimum(j + ahead, n // tn - 1) + joff
            if w_rows:
                in_specs.append(pl.BlockSpec((tn, ks), lambda j, i, s=s, tile=tile: (tile(j, i), s)))
            else:
                in_specs.append(pl.BlockSpec((ks, tn), lambda j, i, s=s, tile=tile: (s, tile(j, i))))
            args.append(w)
    else:
        in_specs.append(pl.BlockSpec((k, tn), lambda j, i: (0, j + joff)))
        args.append(w)
    if shift:
        if w_rows:
            assert shift % 8 == 0
            in_specs.append(pl.BlockSpec((LANES, k), lambda j, i: ((j + joff + 1) * per, 0)))
        else:
            in_specs.append(pl.BlockSpec((k, LANES), lambda j, i: (0, (j + joff + 1) * per)))
        args.append(w)
    if colscale is not None:
        in_specs.append(pl.BlockSpec((1, tn), lambda j, i: (0, j)))
        args.append(colscale)
    if res is not None:
        in_specs.append(pl.BlockSpec((tm, tn), lambda j, i: (i, j)))
        args.append(res)
    return pl.pallas_call(
        functools.partial(_matmul_kernel, has_res=res is not None, has_scale=colscale is not None,
                          cast_w=cast_w, shift=shift, w_rows=w_rows),
        grid=(n // tn, n_i),
        in_specs=in_specs,
        out_specs=pl.BlockSpec((tm, tn), lambda j, i: (i, j)),
        out_shape=jax.ShapeDtypeStruct((m, n), out_dtype),
        scratch_shapes=[pltpu.VMEM((k, tn), BF16)] if cast_w else [],
        compiler_params=_cparams(("arbitrary", "arbitrary")),
        name="matmul_res" if res is not None else "matmul",
    )(*args)


MASKED_BUCKET = N_BUCKETS


def _t5_bucket_np(dist):
    max_exact = N_BUCKETS // 2
    d = np.maximum(dist, 0)
    log_ratio = np.log(np.maximum(d, max_exact).astype(np.float32) / np.float32(max_exact))
    large = max_exact + (log_ratio / np.float32(math.log(MAX_DISTANCE / max_exact))
                         * np.float32(N_BUCKETS - max_exact)).astype(np.int32)
    return np.where(d < max_exact, d, np.minimum(large, N_BUCKETS - 1)).astype(np.int32)


def _bias_kernel(rb_ref, idx_ref, o_ref):
    h = pl.program_id(0)
    idx = idx_ref[...]
    acc = jnp.zeros(idx.shape, F32)
    for b in range(N_BUCKETS + 1):
        acc = jnp.where(idx == b, rb_ref[h, b], acc)
    o_ref[...] = acc


def _bias_table(rel_bias_ext, bucket_idx):
    h = rel_bias_ext.shape[0]
    rows = bucket_idx.shape[0]
    return pl.pallas_call(
        _bias_kernel,
        grid=(h,),
        in_specs=[pl.BlockSpec(memory_space=pltpu.SMEM),
                  pl.BlockSpec((rows, LANES), lambda i: (0, 0))],
        out_specs=pl.BlockSpec((rows, LANES), lambda i: (0, i)),
        out_shape=jax.ShapeDtypeStruct((rows, h * LANES), F32),
        compiler_params=_cparams(("arbitrary",)),
        name="bias_table",
    )(rel_bias_ext, jnp.asarray(bucket_idx))


def _compress_kernel(kv_ref, pos_ref, kw1_ref, kw2_ref, vw1_ref, vw2_ref, kc_ref, vct_ref, *, groups):
    half = pos_ref.shape[1] // 2
    nch = kv_ref.shape[1]
    gd = groups * HEAD_DIM
    g = pl.program_id(1)
    row = lax.broadcasted_iota(jnp.int32, (nch, HEAD_DIM), 0)

    def one(col, w1_ref, w2_ref):
        x = jnp.concatenate(
            [kv_ref[0, :, pl.ds(pl.multiple_of(o * 2 * gd + col + g * HEAD_DIM, HEAD_DIM), HEAD_DIM)]
             for o in range(CMP_STRIDE)], axis=1).astype(F32)
        lo = (x + pos_ref[:, :half]).astype(BF16)
        hi = pltpu.roll(x + pos_ref[:, half:], nch - 1, 0).astype(BF16)
        h = _dot(lo, w1_ref[:half, :]) + _dot(hi, w1_ref[half:, :])
        h = h * _sigmoid(h)
        o = _dot(h.astype(BF16), w2_ref[...])
        return jnp.where(row < nch - 1, o, 0.0)

    kc_ref[0, 0] = one(0, kw1_ref, kw2_ref).astype(kc_ref.dtype)
    vct_ref[0, 0] = one(gd, vw1_ref, vw2_ref).T.astype(vct_ref.dtype)


def _compress(kv, pos, kw1, kw2, vw1, vw2, *, groups):
    b, nch, wid = kv.shape
    oblk = pl.BlockSpec((1, 1, nch, HEAD_DIM), lambda i, j: (i, j, 0, 0))

    def const(a):
        return pl.BlockSpec(a.shape, lambda i, j: (0,) * a.ndim)

    out = jax.ShapeDtypeStruct((b, groups, nch, HEAD_DIM), BF16)
    return pl.pallas_call(
        functools.partial(_compress_kernel, groups=groups),
        grid=(b, groups),
        in_specs=[pl.BlockSpec((1, nch, wid), lambda i, j: (i, 0, 0)),
                  const(pos), const(kw1), const(kw2), const(vw1), const(vw2)],
        out_specs=[oblk, oblk],
        out_shape=[out, out],
        compiler_params=_cparams(("parallel", "arbitrary")),
        name="compress",
    )(kv, pos, kw1, kw2, vw1, vw2)


def _split3(x):
    a = x.astype(BF16)
    r = x - a.astype(F32)
    b = r.astype(BF16)
    c = (r - b.astype(F32)).astype(BF16)
    return a, b, c


def _nsa_kernel(q_ref, kc_ref, vct_ref, ks_ref, vs_ref, kw_ref, vw_ref, bc_ref, bt_ref, gl_ref,
                zc_ref, zs_ref, zw_ref, ovl_ref, o_ref, sel_scr, acc_scr, s_scr, p_scr, acc2_scr, s2_scr, p2_scr,
                vst_scr, vwt_scr, gate_scr,
                *, group, n_slc, n_sel):
    R = group
    QB = LANES
    W = R * QB
    i = pl.program_id(2)
    q0 = i * QB

    @pl.when(i == 0)
    def _():
        for c in range(vs_ref.shape[1] // QB):
            cols = slice(c * QB, (c + 1) * QB)
            vst_scr[:, cols] = vs_ref[0, cols, :].T
            vwt_scr[:, cols] = vw_ref[0, cols, :].T

    q = q_ref[0]
    qt = jnp.concatenate([q[:, r * HEAD_DIM:(r + 1) * HEAD_DIM].T for r in range(R)], axis=1)

    s = _dot(kc_ref[0, 0], qt) + bc_ref[...]
    m = jnp.max(s, axis=0, keepdims=True)
    e = jnp.exp2(s - m)
    denom = jnp.maximum(jnp.sum(e, axis=0, keepdims=True), 1e-30)
    p_c = e * jnp.where(m > 0.5 * NEG_BIG, 1.0 / denom, 0.0)
    o_c = _dot(vct_ref[0, 0], p_c.astype(BF16))

    psum = p_c[:, 0:QB]
    for r in range(1, R):
        psum = psum + p_c[:, r * QB:(r + 1) * QB]
    ovl = ovl_ref[...]
    imp = sum(_dot(ovl, part) for part in _split3(psum))[:n_slc]
    j = lax.broadcasted_iota(jnp.int32, (n_slc, QB), 0)
    t = q0 + lax.broadcasted_iota(jnp.int32, (n_slc, QB), 1)
    cur = t // SLC_BLOCK
    forced = (j == 0) | (j == cur) | (j == cur - 1)
    score = jnp.where(forced, FORCE_SCORE, jnp.where(j * SLC_BLOCK <= t, imp, -1.0))
    rank = jnp.zeros((n_slc, QB), F32)
    for jp in range(n_slc):
        other = score[jp:jp + 1, :]
        beats = (other > score) | ((other == score) & (j > jp))
        rank = rank + jnp.where(beats, 1.0, 0.0)
    sel_scr[...] = jnp.where(rank < n_sel, 0.0, NEG_BIG)

    far = WINDOW // QB
    lo_w = jnp.maximum(i - far, 0)

    def tile_slice(kt):
        return pl.ds(pl.multiple_of(kt * QB, QB), QB)

    def bias(tile):
        return bt_ref[tile_slice(tile), :]

    class Chain:
        def __init__(self, k_ref, vt_ref, s_ref, p_ref, a_ref, block_mask):
            self.k_ref, self.vt_ref, self.s_ref, self.p_ref, self.a_ref = k_ref, vt_ref, s_ref, p_ref, a_ref
            self.block_mask = block_mask

        def raw_scores(self, kt):
            return _dot(self.k_ref[0, tile_slice(kt), :], qt)

        def start(self, first):
            s = self.raw_scores(i) + bias(0)
            m = jnp.max(s, axis=0, keepdims=True)
            p = jnp.exp2(s - m)
            l = jnp.sum(p, axis=0, keepdims=True)
            self.a_ref[...] = jnp.zeros_like(self.a_ref)
            self.p_ref[...] = p.astype(BF16)
            self.s_ref[...] = self.raw_scores(first)
            return m, l, jnp.ones((1, W), F32)

        def accumulate(self, kt, alpha):
            self.a_ref[...] = alpha * self.a_ref[...] + _dot(self.vt_ref[:, tile_slice(kt)], self.p_ref[...])

        def step(self, carry, prev, cur, nxt, tile):
            m, l, alpha_prev = carry
            self.accumulate(prev, alpha_prev)
            s = self.s_ref[...] + bias(tile)
            if self.block_mask:
                per = QB // SLC_BLOCK
                add = jnp.concatenate(
                    [jnp.broadcast_to(sel_scr[pl.ds(cur * per + c, 1), :], (SLC_BLOCK, QB)) for c in range(per)],
                    axis=0)
                s = s + jnp.concatenate([add] * R, axis=1)
            m_new = jnp.maximum(m, jnp.max(s, axis=0, keepdims=True))
            alpha = jnp.exp2(m - m_new)
            p = jnp.exp2(s - m_new)
            l = alpha * l + jnp.sum(p, axis=0, keepdims=True)
            self.p_ref[...] = p.astype(BF16)
            self.s_ref[...] = self.raw_scores(nxt)
            return m_new, l, alpha

        def finish(self, carry, last):
            _, l, alpha = carry
            self.accumulate(last, alpha)
            return self.a_ref[...] * (1.0 / l)

    sel = Chain(ks_ref, vst_scr, s_scr, p_scr, acc_scr, True)
    win = Chain(kw_ref, vwt_scr, s2_scr, p2_scr, acc2_scr, False)
    carry_s = sel.start(lo_w)
    carry_w = win.start(lo_w)

    def both(kt, carry):
        carry_s, carry_w = carry
        prev = jnp.where(kt == lo_w, i, kt - 1)
        d = i - kt
        carry_s = sel.step(carry_s, prev, kt, jnp.where(kt == i - 1, 0, kt + 1), jnp.minimum(d, 2))
        carry_w = win.step(carry_w, prev, kt, jnp.minimum(kt + 1, i), jnp.where(d == far, 3, jnp.minimum(d, 2)))
        return carry_s, carry_w

    carry_s, carry_w = lax.fori_loop(lo_w, i, both, (carry_s, carry_w))
    o_w = win.finish(carry_w, jnp.where(i > 0, i - 1, i))

    def sel_only(kt, carry):
        return sel.step(carry, jnp.where(kt == 0, i - 1, kt - 1), kt, jnp.minimum(kt + 1, i), 2)

    carry_s = lax.fori_loop(0, lo_w, sel_only, carry_s)
    o_s = sel.finish(carry_s, jnp.where(lo_w > 0, lo_w - 1, jnp.where(i > 0, i - 1, i)))

    n_heads = R * pl.num_programs(0)
    head0 = pl.program_id(0) * R
    gate_scr[...] = _sigmoid(gl_ref[0]).T
    for r in range(R):
        cols = slice(r * QB, (r + 1) * QB)
        mixed = jnp.zeros((QB, HEAD_DIM), F32)
        for c, (o, z_ref) in enumerate(((o_c, zc_ref), (o_s, zs_ref), (o_w, zw_ref))):
            z = z_ref[0, :, cols]
            gated = o[:, cols] * gate_scr[pl.ds(c * n_heads + head0 + r, 1), :]
            mixed = mixed + gated.T * (z * _sigmoid(z))
        o_ref[0, :, cols] = mixed.astype(o_ref.dtype)


def _nsa_attention(qkv, kc, vct, bias_c, bias_t, gl, z, ovl, *, batch, seq):
    G, R = N_KV_GROUPS, N_HEADS // N_KV_GROUPS
    QB = LANES
    W = R * QB
    nqb = seq // QB
    n_slc = seq // SLC_BLOCK
    n_sel = min(N_SELECT, n_slc)

    def kv_spec(which):
        return pl.BlockSpec((1, seq, HEAD_DIM), lambda g, b, i: (b, 0, N_HEADS + which * G + g))

    def z_spec(c):
        return pl.BlockSpec((1, QB, R * HEAD_DIM), lambda g, b, i: (b, i, c * G + g))

    cmp_spec = pl.BlockSpec((1, 1, LANES, HEAD_DIM), lambda g, b, i: (b, g, 0, 0))
    in_specs = [
        pl.BlockSpec((1, QB, R * HEAD_DIM), lambda g, b, i: (b, i, g)),
        cmp_spec, cmp_spec,
        kv_spec(2), kv_spec(3), kv_spec(4), kv_spec(5),
        pl.BlockSpec((QB, W), lambda g, b, i: (i, g)),
        pl.BlockSpec((4 * QB, W), lambda g, b, i: (0, g)),
        pl.BlockSpec((1, QB, LANES), lambda g, b, i: (b, i, 0)),
        z_spec(0), z_spec(1), z_spec(2),
        pl.BlockSpec(ovl.shape, lambda g, b, i: (0, 0)),
    ]
    return pl.pallas_call(
        functools.partial(_nsa_kernel, group=R, n_slc=n_slc, n_sel=n_sel),
        grid=(G, batch, nqb),
        in_specs=in_specs,
        out_specs=pl.BlockSpec((1, QB, R * HEAD_DIM), lambda g, b, i: (b, i, g)),
        out_shape=jax.ShapeDtypeStruct((batch, seq, N_HEADS * HEAD_DIM), BF16),
        scratch_shapes=[pltpu.VMEM((n_slc, QB), F32), pltpu.VMEM((HEAD_DIM, W), F32),
                        pltpu.VMEM((QB, W), F32), pltpu.VMEM((QB, W), BF16),
                        pltpu.VMEM((HEAD_DIM, W), F32), pltpu.VMEM((QB, W), F32), pltpu.VMEM((QB, W), BF16),
                        pltpu.VMEM((HEAD_DIM, seq), BF16), pltpu.VMEM((HEAD_DIM, seq), BF16),
                        pltpu.VMEM((LANES, QB), F32)],
        compiler_params=_cparams(("parallel", "parallel", "arbitrary")),
        name="nsa_attention",
    )(qkv, kc, vct, qkv, qkv, qkv, qkv, bias_c, bias_t, gl, z, z, z, ovl)


def _sb_kernel(q_ref, k_ref, v_ref, z_ref, tri_ref, o_ref, vt_scr, l_scr, d_scr, w_scr, acc_scr, *, tq, hp):
    i = pl.program_id(2)
    tk = tq
    tri = tri_ref[...]
    n_sub = tk // LANES
    heads = range(hp)

    @pl.when(i == 0)
    def _():
        for g in heads:
            for c in range(v_ref.shape[1] // LANES):
                cols = slice(c * LANES, (c + 1) * LANES)
                vt_scr[g, :, cols] = v_ref[0, cols, g * HEAD_DIM:(g + 1) * HEAD_DIM].T

    qts = [q_ref[0, :, g * HEAD_DIM:(g + 1) * HEAD_DIM].T for g in heads]

    def key0(t):
        return pl.multiple_of(jnp.maximum(i - t, 0) * tk, tk)

    def logits(t):
        k0 = key0(t)
        for g in heads:
            l_scr[g] = _dot(k_ref[0, pl.ds(k0, tk), g * HEAD_DIM:(g + 1) * HEAD_DIM], qts[g])

    def softplus_sums(diagonal):
        totals = []
        for g in heads:
            logit = l_scr[g]
            neg_abs = pltpu.bitcast(pltpu.bitcast(logit, jnp.uint32) | jnp.uint32(0x80000000), F32)
            sp = jnp.maximum(logit, 0.0) + jnp.log(1.0 + jnp.exp2(neg_abs)) * LOG2E
            if diagonal:
                ok = (lax.broadcasted_iota(jnp.int32, (tk, tq), 0) < lax.broadcasted_iota(jnp.int32, (tk, tq), 1))
                sp = jnp.where(ok, sp, 0.0)
                logit = jnp.where(ok, logit, NEG_BIG)
            d_scr[g] = logit
            hi = sp.astype(BF16)
            lo = (sp - hi.astype(F32)).astype(BF16)
            tot = []
            for c in range(n_sub):
                rows = slice(c * LANES, (c + 1) * LANES)
                within = _dot(tri, jnp.concatenate([hi[rows], lo[rows]], axis=0))
                w_scr[g, rows, :] = within
                tot.append(within[0:1])
            totals.append(tuple(tot))
        return tuple(totals)

    def weigh_values(t, laters, totals):
        k0 = key0(t)
        out = []
        for g in heads:
            later = laters[g]
            parts = [None] * n_sub
            for c in reversed(range(n_sub)):
                rows = slice(c * LANES, (c + 1) * LANES)
                parts[c] = jnp.exp2(d_scr[g, rows, :] - w_scr[g, rows, :] - later)
                later = later + totals[g][c]
            a = jnp.concatenate(parts, axis=0).astype(BF16)
            acc_scr[g] += _dot(vt_scr[g, :, pl.ds(k0, tk)], a)
            out.append(later)
        return tuple(out)

    acc_scr[...] = jnp.zeros_like(acc_scr)
    logits(0)
    totals = softplus_sums(True)
    logits(1)

    def body(t, carry):
        laters, totals = carry
        laters = weigh_values(t - 1, laters, totals)
        totals = softplus_sums(False)
        logits(t + 1)
        return laters, totals

    laters, totals = lax.fori_loop(1, i + 1, body, ((jnp.zeros((1, tq), F32),) * hp, totals))
    weigh_values(i, laters, totals)
    for g in heads:
        cols = slice(g * HEAD_DIM, (g + 1) * HEAD_DIM)
        z = z_ref[0, :, cols]
        o_ref[0, :, cols] = (acc_scr[g].T * (z * _sigmoid(z))).astype(o_ref.dtype)


def _sb_attention(q, kv, z, tri, *, batch, seq, tq=512, hp=4):
    H = N_HEADS
    tq = min(tq, seq)
    wid = hp * HEAD_DIM
    return pl.pallas_call(
        functools.partial(_sb_kernel, tq=tq, hp=hp),
        grid=(batch, H // hp, seq // tq),
        in_specs=[
            pl.BlockSpec((1, tq, wid), lambda b, h, i: (b, i, h)),
            pl.BlockSpec((1, seq, wid), lambda b, h, i: (b, 0, h)),
            pl.BlockSpec((1, seq, wid), lambda b, h, i: (b, 0, H // hp + h)),
            pl.BlockSpec((1, tq, wid), lambda b, h, i: (b, i, h)),
            pl.BlockSpec(tri.shape, lambda b, h, i: (0, 0)),
        ],
        out_specs=pl.BlockSpec((1, tq, wid), lambda b, h, i: (b, i, h)),
        out_shape=jax.ShapeDtypeStruct((batch, seq, H * HEAD_DIM), BF16),
        scratch_shapes=[pltpu.VMEM((hp, HEAD_DIM, seq), BF16), pltpu.VMEM((hp, tq, tq), F32),
                        pltpu.VMEM((hp, tq, tq), F32), pltpu.VMEM((hp, tq, tq), F32),
                        pltpu.VMEM((hp, HEAD_DIM, tq), F32)],
        compiler_params=_cparams(("parallel", "parallel", "arbitrary")),
        name="sb_attention",
    )(q, kv, kv, z, tri)


def _static_tables(seq):
    QB = LANES
    nch = seq // CMP_STRIDE
    n_slc = seq // SLC_BLOCK
    assert nch == LANES and n_slc <= LANES
    q = np.arange(QB)[None, :]
    rows = np.arange(seq)[:, None]
    dist_c = (rows // QB) * QB + q - ((rows % QB) * CMP_STRIDE + CMP_BLOCK - 1)
    idx_c = np.where(dist_c >= 0, _t5_bucket_np(dist_c), MASKED_BUCKET).astype(np.int32)
    k = np.arange(QB)[:, None]
    far = WINDOW // QB
    tiles = []
    for delta, keep in ((0, k <= q), (1, None), (2, None), (far, k > q)):
        idx = _t5_bucket_np(delta * QB + q - k)
        tiles.append(idx if keep is None else np.where(keep, idx, MASKED_BUCKET))
    idx_t = np.concatenate(tiles, axis=0).astype(np.int32)
    cmp_start = np.arange(LANES) * CMP_STRIDE
    slc_start = np.arange(LANES) * SLC_BLOCK
    ovl = ((cmp_start[None, :] < slc_start[:, None] + SLC_BLOCK)
           & (cmp_start[None, :] + CMP_BLOCK - 1 >= slc_start[:, None]))
    ovl = ovl & (np.arange(LANES)[None, :] < nch - 1) & (np.arange(LANES)[:, None] < n_slc)
    return idx_c, idx_t, ovl.astype(np.float32)


def _nsa_layer(xf, tabs, norm, w_in, cmp_pos, kw1, kw2, vw1, vw2, w_out, *, batch, seq):
    bias_c, bias_t, ovl = tabs
    H, G, Dh = N_HEADS, N_KV_GROUPS, HEAD_DIM
    HD, GD = H * Dh, G * Dh
    n_qkv = HD + 6 * GD
    hn, = _rmsnorm(xf, [norm], BF16)
    qscale = jnp.concatenate([jnp.full((1, HD), LOG2E / math.sqrt(Dh), F32), jnp.ones((1, 6 * GD), F32)], axis=1)
    w_in_t = w_in.T
    qkv = _matmul(hn, w_in_t, BF16, n=n_qkv, colscale=qscale, w_rows=True)
    assert 3 * H <= LANES
    gl = _matmul(hn, w_in_t, F32, col0=n_qkv, n=LANES, w_rows=True)
    z = _matmul(hn, w_in_t, F32, col0=n_qkv + 3 * H, n=3 * HD, w_rows=True)

    nch = seq // CMP_STRIDE
    cmp_in = qkv[:, HD:HD + 2 * GD].reshape(batch, nch, CMP_STRIDE * 2 * GD)
    kc, vct = _compress(cmp_in, cmp_pos.reshape(1, CMP_BLOCK * Dh),
                        kw1.astype(BF16), kw2.astype(BF16), vw1.astype(BF16), vw2.astype(BF16), groups=G)
    mixed = _nsa_attention(qkv.reshape(batch, seq, n_qkv), kc, vct, bias_c, bias_t,
                           gl.reshape(batch, seq, LANES), z.reshape(batch, seq, 3 * HD),
                           ovl, batch=batch, seq=seq)
    return _matmul(mixed.reshape(batch * seq, HD), w_out, F32, res=xf)


def _sb_layer(xf, hn, kv, tri, w_in, w_out, *, batch, seq):
    H, Dh = N_HEADS, HEAD_DIM
    HD = H * Dh
    q = _matmul(hn, w_in, BF16, n=HD, colscale=jnp.full((1, HD), LOG2E / math.sqrt(Dh), F32))
    z = _matmul(hn, w_in, F32, col0=HD, n=HD)
    o = _sb_attention(q.reshape(batch, seq, HD), kv, z.reshape(batch, seq, HD), tri, batch=batch, seq=seq)
    return _matmul(o.reshape(batch * seq, HD), w_out, F32, res=xf)


def kernel(x, rel_bias, a0_norm, a0_w_in, a0_cmp_pos, a0_cmp_k_w1, a0_cmp_k_w2, a0_cmp_v_w1, a0_cmp_v_w2, a0_w_out, a1_norm, a1_w_in, a1_cmp_pos, a1_cmp_k_w1, a1_cmp_k_w2, a1_cmp_v_w1, a1_cmp_v_w2, a1_w_out, kv_norm, w_kv, b2_norm, b2_w_in, b2_w_out, b3_norm, b3_w_in, b3_w_out, final_norm):
    batch, seq, d = x.shape
    HD = N_HEADS * HEAD_DIM
    xf = x.reshape(batch * seq, d)

    idx_c, idx_t, ovl = _static_tables(seq)
    rel_bias_ext = jnp.concatenate([rel_bias.T * LOG2E, jnp.full((N_HEADS, 1), NEG_BIG, F32)], axis=1)
    tabs = (_bias_table(rel_bias_ext, idx_c), _bias_table(rel_bias_ext, idx_t), jnp.asarray(ovl, BF16))

    xf = _nsa_layer(xf, tabs, a0_norm, a0_w_in, a0_cmp_pos, a0_cmp_k_w1, a0_cmp_k_w2, a0_cmp_v_w1, a0_cmp_v_w2,
                    a0_w_out, batch=batch, seq=seq)
    xf = _nsa_layer(xf, tabs, a1_norm, a1_w_in, a1_cmp_pos, a1_cmp_k_w1, a1_cmp_k_w2, a1_cmp_v_w1, a1_cmp_v_w2,
                    a1_w_out, batch=batch, seq=seq)

    hn_kv, hn_b2 = _rmsnorm(xf, [kv_norm, b2_norm], BF16)
    kv = _matmul(hn_kv, w_kv, BF16).reshape(batch, seq, 2 * HD)
    m = np.arange(LANES)[None, :] >= np.arange(LANES)[:, None]
    tri = jnp.asarray(np.concatenate([m, m], axis=1), BF16)
    xf = _sb_layer(xf, hn_b2, kv, tri, b2_w_in, b2_w_out, batch=batch, seq=seq)
    hn_b3, = _rmsnorm(xf, [b3_norm], BF16)
    xf = _sb_layer(xf, hn_b3, kv, tri, b3_w_in, b3_w_out, batch=batch, seq=seq)

    out, = _rmsnorm(xf, [final_norm], F32)
    return out.reshape(batch, seq, d)
```

```python
import functools
import math

import numpy as np
import jax
import jax.numpy as jnp
from jax import lax
from jax.experimental import pallas as pl
from jax.experimental.pallas import tpu as pltpu

N_HEADS = 32
HEAD_DIM = 128
N_KV_GROUPS = 4
CMP_BLOCK = 32
CMP_STRIDE = 16
SLC_BLOCK = 64
N_SELECT = 16
WINDOW = 512
N_BUCKETS = 32
MAX_DISTANCE = 128
RMS_EPS = 1e-6
FORCE_SCORE = 1e6

LANES = 128
NEG_BIG = -1e30
VMEM_LIMIT = 62 * 1024 * 1024
LOG2E = math.log2(math.e)

F32 = jnp.float32
BF16 = jnp.bfloat16


def _cparams(sem):
    return pltpu.CompilerParams(dimension_semantics=sem, vmem_limit_bytes=VMEM_LIMIT)


def _dot(a, b):
    return jnp.dot(a, b, preferred_element_type=F32)


def _sigmoid(x):
    return 0.5 * jnp.tanh(0.5 * x) + 0.5


def _rmsnorm_kernel(x_ref, *refs):
    n_out = len(refs) // 2
    x = x_ref[...]
    ms = jnp.mean(x * x, axis=-1, keepdims=True)
    y = x * lax.rsqrt(ms + RMS_EPS)
    for g_ref, o_ref in zip(refs[:n_out], refs[n_out:]):
        o_ref[...] = (y * g_ref[...]).astype(o_ref.dtype)


def _rmsnorm(x, gains, out_dtype, tm=512):
    m, d = x.shape
    tm = min(tm, m)
    row = pl.BlockSpec((tm, d), lambda i: (i, 0))
    outs = pl.pallas_call(
        _rmsnorm_kernel,
        grid=(m // tm,),
        in_specs=[row] + [pl.BlockSpec((1, d), lambda i: (0, 0))] * len(gains),
        out_specs=[row] * len(gains),
        out_shape=[jax.ShapeDtypeStruct((m, d), out_dtype)] * len(gains),
        compiler_params=_cparams(("parallel",)),
        name="rmsnorm",
    )(x, *[g.reshape(1, d) for g in gains])
    return outs


W_SLABS = 4


def _matmul_kernel(*refs, has_res, has_scale, cast_w, shift, w_rows):
    it = iter(refs)
    x_ref = next(it)
    w_refs = [next(it) for _ in range(W_SLABS if cast_w else 1)]
    wn_ref = next(it) if shift else None
    s_ref = next(it) if has_scale else None
    r_ref = next(it) if has_res else None
    o_ref = next(it)
    if cast_w:
        wb_ref = next(it)

        @pl.when(pl.program_id(1) == 0)
        def _():
            ks = wb_ref.shape[0] // W_SLABS
            for s, w_ref in enumerate(w_refs):
                w = w_ref[...]
                if shift:
                    ax = 0 if w_rows else 1
                    nxt = wn_ref[:, s * ks:(s + 1) * ks] if w_rows else wn_ref[s * ks:(s + 1) * ks, :]
                    w = jnp.concatenate([lax.slice_in_dim(w, shift, w.shape[ax], axis=ax),
                                         lax.slice_in_dim(nxt, 0, shift, axis=ax)], axis=ax)
                if w_rows:
                    w = w.T
                if has_scale:
                    w = w * s_ref[...]
                wb_ref[s * ks:(s + 1) * ks, :] = w.astype(BF16)
    else:
        wb_ref = w_refs[0]
    acc = _dot(x_ref[...], wb_ref[...])
    if has_res:
        acc = r_ref[...] + acc
    o_ref[...] = acc.astype(o_ref.dtype)


def _matmul(x, w, out_dtype, *, col0=0, n=None, res=None, colscale=None, w_rows=False, tm=None, tn=None):
    m, k = x.shape
    n_total = w.shape[0] if w_rows else w.shape[1]
    n = n_total - col0 if n is None else n
    if tm is None:
        tm, tn = 512, 1024
    tm, tn = min(tm, m), min(tn, n)
    cast_w = w.dtype != BF16
    shift = col0 % LANES
    base = col0 - shift
    assert m % tm == 0 and n % tn == 0 and base % tn == 0 and tn % LANES == 0
    assert cast_w or (colscale is None and shift == 0 and not w_rows)
    joff = base // tn
    per = tn // LANES
    n_i = m // tm
    in_specs = [pl.BlockSpec((tm, k), lambda j, i: (i, 0))]
    args = [x]
    if cast_w:
        assert k % (W_SLABS * LANES) == 0
        ks = k // W_SLABS
        stagger = n_i > W_SLABS
        for s in range(W_SLABS):
            def tile(j, i, s=s):
                ahead = (i > s).astype(jnp.int32) if stagger else 0
                return jnp.minimum(j + ahead, n // tn - 1) + joff
            if w_rows:
                in_specs.append(pl.BlockSpec((tn, ks), lambda j, i, s=s, tile=tile: (tile(j, i), s)))
            else:
                in_specs.append(pl.BlockSpec((ks, tn), lambda j, i, s=s, tile=tile: (s, tile(j, i))))
            args.append(w)
    else:
        in_specs.append(pl.BlockSpec((k, tn), lambda j, i: (0, j + joff)))
        args.append(w)
    if shift:
        if w_rows:
            assert shift % 8 == 0
            in_specs.append(pl.BlockSpec((LANES, k), lambda j, i: ((j + joff + 1) * per, 0)))
        else:
            in_specs.append(pl.BlockSpec((k, LANES), lambda j, i: (0, (j + joff + 1) * per)))
        args.append(w)
    if colscale is not None:
        in_specs.append(pl.BlockSpec((1, tn), lambda j, i: (0, j)))
        args.append(colscale)
    if res is not None:
        in_specs.append(pl.BlockSpec((tm, tn), lambda j, i: (i, j)))
        args.append(res)
    return pl.pallas_call(
        functools.partial(_matmul_kernel, has_res=res is not None, has_scale=colscale is not None,
                          cast_w=cast_w, shift=shift, w_rows=w_rows),
        grid=(n // tn, n_i),
        in_specs=in_specs,
        out_specs=pl.BlockSpec((tm, tn), lambda j, i: (i, j)),
        out_shape=jax.ShapeDtypeStruct((m, n), out_dtype),
        scratch_shapes=[pltpu.VMEM((k, tn), BF16)] if cast_w else [],
        compiler_params=_cparams(("arbitrary", "arbitrary")),
        name="matmul_res" if res is not None else "matmul",
    )(*args)


MASKED_BUCKET = N_BUCKETS


def _t5_bucket_np(dist):
    max_exact = N_BUCKETS // 2
    d = np.maximum(dist, 0)
    log_ratio = np.log(np.maximum(d, max_exact).astype(np.float32) / np.float32(max_exact))
    large = max_exact + (log_ratio / np.float32(math.log(MAX_DISTANCE / max_exact))
                         * np.float32(N_BUCKETS - max_exact)).astype(np.int32)
    return np.where(d < max_exact, d, np.minimum(large, N_BUCKETS - 1)).astype(np.int32)


def _bias_kernel(rb_ref, idx_ref, o_ref):
    h = pl.program_id(0)
    idx = idx_ref[...]
    acc = jnp.zeros(idx.shape, F32)
    for b in range(N_BUCKETS + 1):
        acc = jnp.where(idx == b, rb_ref[h, b], acc)
    o_ref[...] = acc


def _bias_table(rel_bias_ext, bucket_idx):
    h = rel_bias_ext.shape[0]
    rows = bucket_idx.shape[0]
    return pl.pallas_call(
        _bias_kernel,
        grid=(h,),
        in_specs=[pl.BlockSpec(memory_space=pltpu.SMEM),
                  pl.BlockSpec((rows, LANES), lambda i: (0, 0))],
        out_specs=pl.BlockSpec((rows, LANES), lambda i: (0, i)),
        out_shape=jax.ShapeDtypeStruct((rows, h * LANES), F32),
        compiler_params=_cparams(("arbitrary",)),
        name="bias_table",
    )(rel_bias_ext, jnp.asarray(bucket_idx))


def _compress_kernel(kv_ref, pos_ref, kw1_ref, kw2_ref, vw1_ref, vw2_ref, kc_ref, vct_ref, raw_scr, *, groups):
    half = pos_ref.shape[1] // 2
    nch = kv_ref.shape[1] // CMP_STRIDE
    gd = groups * HEAD_DIM
    g = pl.program_id(1)
    row = lax.broadcasted_iota(jnp.int32, (nch, HEAD_DIM), 0)

    @pl.when(g == 0)
    def _():
        for cb in range(2 * groups):
            raw_scr[cb] = kv_ref[0, :, cb * HEAD_DIM:(cb + 1) * HEAD_DIM].astype(F32)

    def one(col, w1_ref, w2_ref):
        cb = col // HEAD_DIM + g
        x = jnp.concatenate([raw_scr[cb, pl.ds(o, nch, stride=CMP_STRIDE), :] for o in range(CMP_STRIDE)], axis=1)
        lo = (x + pos_ref[:, :half]).astype(BF16)
        hi = pltpu.roll(x + pos_ref[:, half:], nch - 1, 0).astype(BF16)
        h = _dot(lo, w1_ref[:half, :]) + _dot(hi, w1_ref[half:, :])
        h = h * _sigmoid(h)
        o = _dot(h.astype(BF16), w2_ref[...])
        return jnp.where(row < nch - 1, o, 0.0)

    kc_ref[0, 0] = one(0, kw1_ref, kw2_ref).astype(kc_ref.dtype)
    vct_ref[0, 0] = one(gd, vw1_ref, vw2_ref).T.astype(vct_ref.dtype)


def _compress(qkv, col_block, pos, kw1, kw2, vw1, vw2, *, groups):
    b, seq, _ = qkv.shape
    nch = seq // CMP_STRIDE
    wid = 2 * groups * HEAD_DIM
    oblk = pl.BlockSpec((1, 1, nch, HEAD_DIM), lambda i, j: (i, j, 0, 0))

    def const(a):
        return pl.BlockSpec(a.shape, lambda i, j: (0,) * a.ndim)

    out = jax.ShapeDtypeStruct((b, groups, nch, HEAD_DIM), BF16)
    return pl.pallas_call(
        functools.partial(_compress_kernel, groups=groups),
        grid=(b, groups),
        in_specs=[pl.BlockSpec((1, seq, wid), lambda i, j: (i, 0, col_block)),
                  const(pos), const(kw1), const(kw2), const(vw1), const(vw2)],
        out_specs=[oblk, oblk],
        out_shape=[out, out],
        scratch_shapes=[pltpu.VMEM((2 * groups, seq, HEAD_DIM), F32)],
        compiler_params=_cparams(("parallel", "arbitrary")),
        name="compress",
    )(qkv, pos, kw1, kw2, vw1, vw2)


def _split3(x):
    a = x.astype(BF16)
    r = x - a.astype(F32)
    b = r.astype(BF16)
    c = (r - b.astype(F32)).astype(BF16)
    return a, b, c


def _nsa_kernel(q_ref, kc_ref, vct_ref, ks_ref, vs_ref, kw_ref, vw_ref, bc_ref, bt_ref, gl_ref,
                zc_ref, zs_ref, zw_ref, ovl_ref, o_ref, sel_scr, acc_scr, s_scr, p_scr, acc2_scr, s2_scr, p2_scr,
                vst_scr, vwt_scr, gate_scr,
                *, group, n_slc, n_sel):
    R = group
    QB = LANES
    W = R * QB
    i = pl.program_id(2)
    q0 = i * QB

    @pl.when(i == 0)
    def _():
        for c in range(vs_ref.shape[1] // QB):
            cols = slice(c * QB, (c + 1) * QB)
            vst_scr[:, cols] = vs_ref[0, cols, :].T
            vwt_scr[:, cols] = vw_ref[0, cols, :].T

    q = q_ref[0]
    qt = jnp.concatenate([q[:, r * HEAD_DIM:(r + 1) * HEAD_DIM].T for r in range(R)], axis=1)

    s = _dot(kc_ref[0, 0], qt) + bc_ref[...]
    m = jnp.max(s, axis=0, keepdims=True)
    e = jnp.exp2(s - m)
    denom = jnp.maximum(jnp.sum(e, axis=0, keepdims=True), 1e-30)
    p_c = e * jnp.where(m > 0.5 * NEG_BIG, 1.0 / denom, 0.0)
    o_c = _dot(vct_ref[0, 0], p_c.astype(BF16))

    psum = p_c[:, 0:QB]
    for r in range(1, R):
        psum = psum + p_c[:, r * QB:(r + 1) * QB]
    ovl = ovl_ref[...]
    imp = sum(_dot(ovl, part) for part in _split3(psum))[:n_slc]
    j = lax.broadcasted_iota(jnp.int32, (n_slc, QB), 0)
    t = q0 + lax.broadcasted_iota(jnp.int32, (n_slc, QB), 1)
    cur = t // SLC_BLOCK
    forced = (j == 0) | (j == cur) | (j == cur - 1)
    score = jnp.where(forced, FORCE_SCORE, jnp.where(j * SLC_BLOCK <= t, imp, -1.0))
    rank = jnp.zeros((n_slc, QB), F32)
    for jp in range(n_slc):
        other = score[jp:jp + 1, :]
        beats = (other > score) | ((other == score) & (j > jp))
        rank = rank + jnp.where(beats, 1.0, 0.0)
    sel_scr[...] = jnp.where(rank < n_sel, 0.0, NEG_BIG)

    far = WINDOW // QB
    lo_w = jnp.maximum(i - far, 0)

    def tile_slice(kt):
        return pl.ds(pl.multiple_of(kt * QB, QB), QB)

    def bias(tile):
        return bt_ref[tile_slice(tile), :]

    class Chain:
        def __init__(self, k_ref, vt_ref, s_ref, p_ref, a_ref, block_mask):
            self.k_ref, self.vt_ref, self.s_ref, self.p_ref, self.a_ref = k_ref, vt_ref, s_ref, p_ref, a_ref
            self.block_mask = block_mask

        def raw_scores(self, kt):
            return _dot(self.k_ref[0, tile_slice(kt), :], qt)

        def start(self, first):
            s = self.raw_scores(i) + bias(0)
            m = jnp.max(s, axis=0, keepdims=True)
            p = jnp.exp2(s - m)
            l = jnp.sum(p, axis=0, keepdims=True)
            self.a_ref[...] = jnp.zeros_like(self.a_ref)
            self.p_ref[...] = p.astype(BF16)
            self.s_ref[...] = self.raw_scores(first)
            return m, l, jnp.ones((1, W), F32)

        def accumulate(self, kt, alpha):
            self.a_ref[...] = alpha * self.a_ref[...] + _dot(self.vt_ref[:, tile_slice(kt)], self.p_ref[...])

        def step(self, carry, prev, cur, nxt, tile):
            m, l, alpha_prev = carry
            self.accumulate(prev, alpha_prev)
            s = self.s_ref[...] + bias(tile)
            if self.block_mask:
                per = QB // SLC_BLOCK
                add = jnp.concatenate(
                    [jnp.broadcast_to(sel_scr[pl.ds(cur * per + c, 1), :], (SLC_BLOCK, QB)) for c in range(per)],
                    axis=0)
                s = s + jnp.concatenate([add] * R, axis=1)
            m_new = jnp.maximum(m, jnp.max(s, axis=0, keepdims=True))
            alpha = jnp.exp2(m - m_new)
            p = jnp.exp2(s - m_new)
            l = alpha * l + jnp.sum(p, axis=0, keepdims=True)
            self.p_ref[...] = p.astype(BF16)
            self.s_ref[...] = self.raw_scores(nxt)
            return m_new, l, alpha

        def finish(self, carry, last):
            _, l, alpha = carry
            self.accumulate(last, alpha)
            return self.a_ref[...] * (1.0 / l)

    sel = Chain(ks_ref, vst_scr, s_scr, p_scr, acc_scr, True)
    win = Chain(kw_ref, vwt_scr, s2_scr, p2_scr, acc2_scr, False)
    carry_s = sel.start(lo_w)
    carry_w = win.start(lo_w)

    def both(kt, carry):
        carry_s, carry_w = carry
        prev = jnp.where(kt == lo_w, i, kt - 1)
        d = i - kt
        carry_s = sel.step(carry_s, prev, kt, jnp.where(kt == i - 1, 0, kt + 1), jnp.minimum(d, 2))
        carry_w = win.step(carry_w, prev, kt, jnp.minimum(kt + 1, i), jnp.where(d == far, 3, jnp.minimum(d, 2)))
        return carry_s, carry_w

    carry_s, carry_w = lax.fori_loop(lo_w, i, both, (carry_s, carry_w))
    o_w = win.finish(carry_w, jnp.where(i > 0, i - 1, i))

    def sel_only(kt, carry):
        return sel.step(carry, jnp.where(kt == 0, i - 1, kt - 1), kt, jnp.minimum(kt + 1, i), 2)

    carry_s = lax.fori_loop(0, lo_w, sel_only, carry_s)
    o_s = sel.finish(carry_s, jnp.where(lo_w > 0, lo_w - 1, jnp.where(i > 0, i - 1, i)))

    n_heads = R * pl.num_programs(0)
    head0 = pl.program_id(0) * R
    gate_scr[...] = _sigmoid(gl_ref[0]).T
    for r in range(R):
        cols = slice(r * QB, (r + 1) * QB)
        mixed = jnp.zeros((QB, HEAD_DIM), F32)
        for c, (o, z_ref) in enumerate(((o_c, zc_ref), (o_s, zs_ref), (o_w, zw_ref))):
            z = z_ref[0, :, cols]
            gated = o[:, cols] * gate_scr[pl.ds(c * n_heads + head0 + r, 1), :]
            mixed = mixed + gated.T * (z * _sigmoid(z))
        o_ref[0, :, cols] = mixed.astype(o_ref.dtype)


def _nsa_attention(qkv, kc, vct, bias_c, bias_t, gl, z, ovl, *, batch, seq):
    G, R = N_KV_GROUPS, N_HEADS // N_KV_GROUPS
    QB = LANES
    W = R * QB
    nqb = seq // QB
    n_slc = seq // SLC_BLOCK
    n_sel = min(N_SELECT, n_slc)

    def kv_spec(which):
        return pl.BlockSpec((1, seq, HEAD_DIM), lambda g, b, i: (b, 0, N_HEADS + which * G + g))

    def z_spec(c):
        return pl.BlockSpec((1, QB, R * HEAD_DIM), lambda g, b, i: (b, i, c * G + g))

    cmp_spec = pl.BlockSpec((1, 1, LANES, HEAD_DIM), lambda g, b, i: (b, g, 0, 0))
    in_specs = [
        pl.BlockSpec((1, QB, R * HEAD_DIM), lambda g, b, i: (b, i, g)),
        cmp_spec, cmp_spec,
        kv_spec(2), kv_spec(3), kv_spec(4), kv_spec(5),
        pl.BlockSpec((QB, W), lambda g, b, i: (i, g)),
        pl.BlockSpec((4 * QB, W), lambda g, b, i: (0, g)),
        pl.BlockSpec((1, QB, LANES), lambda g, b, i: (b, i, 0)),
        z_spec(0), z_spec(1), z_spec(2),
        pl.BlockSpec(ovl.shape, lambda g, b, i: (0, 0)),
    ]
    return pl.pallas_call(
        functools.partial(_nsa_kernel, group=R, n_slc=n_slc, n_sel=n_sel),
        grid=(G, batch, nqb),
        in_specs=in_specs,
        out_specs=pl.BlockSpec((1, QB, R * HEAD_DIM), lambda g, b, i: (b, i, g)),
        out_shape=jax.ShapeDtypeStruct((batch, seq, N_HEADS * HEAD_DIM), BF16),
        scratch_shapes=[pltpu.VMEM((n_slc, QB), F32), pltpu.VMEM((HEAD_DIM, W), F32),
                        pltpu.VMEM((QB, W), F32), pltpu.VMEM((QB, W), BF16),
                        pltpu.VMEM((HEAD_DIM, W), F32), pltpu.VMEM((QB, W), F32), pltpu.VMEM((QB, W), BF16),
                        pltpu.VMEM((HEAD_DIM, seq), BF16), pltpu.VMEM((HEAD_DIM, seq), BF16),
                        pltpu.VMEM((LANES, QB), F32)],
        compiler_params=_cparams(("parallel", "parallel", "arbitrary")),
        name="nsa_attention",
    )(qkv, kc, vct, qkv, qkv, qkv, qkv, bias_c, bias_t, gl, z, z, z, ovl)


def _sb_kernel(q_ref, k_ref, v_ref, z_ref, tri_ref, o_ref, vt_scr, l_scr, d_scr, w_scr, acc_scr, *, tq, hp):
    i = pl.program_id(2)
    tk = tq
    tri = tri_ref[...]
    n_sub = tk // LANES
    heads = range(hp)

    @pl.when(i == 0)
    def _():
        for g in heads:
            for c in range(v_ref.shape[1] // LANES):
                cols = slice(c * LANES, (c + 1) * LANES)
                vt_scr[g, :, cols] = v_ref[0, cols, g * HEAD_DIM:(g + 1) * HEAD_DIM].T

    qts = [q_ref[0, :, g * HEAD_DIM:(g + 1) * HEAD_DIM].T for g in heads]

    def key0(t):
        return pl.multiple_of(jnp.maximum(i - t, 0) * tk, tk)

    def logits(t):
        k0 = key0(t)
        for g in heads:
            l_scr[g] = _dot(k_ref[0, pl.ds(k0, tk), g * HEAD_DIM:(g + 1) * HEAD_DIM], qts[g])

    def softplus_sums(diagonal):
        totals = []
        for g in heads:
            logit = l_scr[g]
            neg_abs = pltpu.bitcast(pltpu.bitcast(logit, jnp.uint32) | jnp.uint32(0x80000000), F32)
            sp = jnp.maximum(logit, 0.0) + jnp.log(1.0 + jnp.exp2(neg_abs)) * LOG2E
            if diagonal:
                ok = (lax.broadcasted_iota(jnp.int32, (tk, tq), 0) < lax.broadcasted_iota(jnp.int32, (tk, tq), 1))
                sp = jnp.where(ok, sp, 0.0)
                logit = jnp.where(ok, logit, NEG_BIG)
            d_scr[g] = logit
            hi = sp.astype(BF16)
            lo = (sp - hi.astype(F32)).astype(BF16)
            tot = []
            for c in range(n_sub):
                rows = slice(c * LANES, (c + 1) * LANES)
                within = _dot(tri, jnp.concatenate([hi[rows], lo[rows]], axis=0))
                w_scr[g, rows, :] = within
                tot.append(within[0:1])
            totals.append(tuple(tot))
        return tuple(totals)

    def weigh_values(t, laters, totals):
        k0 = key0(t)
        out = []
        for g in heads:
            later = laters[g]
            parts = [None] * n_sub
            for c in reversed(range(n_sub)):
                rows = slice(c * LANES, (c + 1) * LANES)
                parts[c] = jnp.exp2(d_scr[g, rows, :] - w_scr[g, rows, :] - later)
                later = later + totals[g][c]
            a = jnp.concatenate(parts, axis=0).astype(BF16)
            acc_scr[g] += _dot(vt_scr[g, :, pl.ds(k0, tk)], a)
            out.append(later)
        return tuple(out)

    acc_scr[...] = jnp.zeros_like(acc_scr)
    logits(0)
    totals = softplus_sums(True)
    logits(1)

    def body(t, carry):
        laters, totals = carry
        laters = weigh_values(t - 1, laters, totals)
        totals = softplus_sums(False)
        logits(t + 1)
        return laters, totals

    laters, totals = lax.fori_loop(1, i + 1, body, ((jnp.zeros((1, tq), F32),) * hp, totals))
    weigh_values(i, laters, totals)
    for g in heads:
        cols = slice(g * HEAD_DIM, (g + 1) * HEAD_DIM)
        z = z_ref[0, :, cols]
        o_ref[0, :, cols] = (acc_scr[g].T * (z * _sigmoid(z))).astype(o_ref.dtype)


def _sb_attention(q, kv, z, tri, *, batch, seq, tq=512, hp=4):
    H = N_HEADS
    tq = min(tq, seq)
    wid = hp * HEAD_DIM
    return pl.pallas_call(
        functools.partial(_sb_kernel, tq=tq, hp=hp),
        grid=(batch, H // hp, seq // tq),
        in_specs=[
            pl.BlockSpec((1, tq, wid), lambda b, h, i: (b, i, h)),
            pl.BlockSpec((1, seq, wid), lambda b, h, i: (b, 0, h)),
            pl.BlockSpec((1, seq, wid), lambda b, h, i: (b, 0, H // hp + h)),
            pl.BlockSpec((1, tq, wid), lambda b, h, i: (b, i, h)),
            pl.BlockSpec(tri.shape, lambda b, h, i: (0, 0)),
        ],
        out_specs=pl.BlockSpec((1, tq, wid), lambda b, h, i: (b, i, h)),
        out_shape=jax.ShapeDtypeStruct((batch, seq, H * HEAD_DIM), BF16),
        scratch_shapes=[pltpu.VMEM((hp, HEAD_DIM, seq), BF16), pltpu.VMEM((hp, tq, tq), F32),
                        pltpu.VMEM((hp, tq, tq), F32), pltpu.VMEM((hp, tq, tq), F32),
                        pltpu.VMEM((hp, HEAD_DIM, tq), F32)],
        compiler_params=_cparams(("parallel", "parallel", "arbitrary")),
        name="sb_attention",
    )(q, kv, kv, z, tri)


def _static_tables(seq):
    QB = LANES
    nch = seq // CMP_STRIDE
    n_slc = seq // SLC_BLOCK
    assert nch == LANES and n_slc <= LANES
    q = np.arange(QB)[None, :]
    rows = np.arange(seq)[:, None]
    dist_c = (rows // QB) * QB + q - ((rows % QB) * CMP_STRIDE + CMP_BLOCK - 1)
    idx_c = np.where(dist_c >= 0, _t5_bucket_np(dist_c), MASKED_BUCKET).astype(np.int32)
    k = np.arange(QB)[:, None]
    far = WINDOW // QB
    tiles = []
    for delta, keep in ((0, k <= q), (1, None), (2, None), (far, k > q)):
        idx = _t5_bucket_np(delta * QB + q - k)
        tiles.append(idx if keep is None else np.where(keep, idx, MASKED_BUCKET))
    idx_t = np.concatenate(tiles, axis=0).astype(np.int32)
    cmp_start = np.arange(LANES) * CMP_STRIDE
    slc_start = np.arange(LANES) * SLC_BLOCK
    ovl = ((cmp_start[None, :] < slc_start[:, None] + SLC_BLOCK)
           & (cmp_start[None, :] + CMP_BLOCK - 1 >= slc_start[:, None]))
    ovl = ovl & (np.arange(LANES)[None, :] < nch - 1) & (np.arange(LANES)[:, None] < n_slc)
    return idx_c, idx_t, ovl.astype(np.float32)


def _nsa_layer(xf, tabs, norm, w_in, cmp_pos, kw1, kw2, vw1, vw2, w_out, *, batch, seq):
    bias_c, bias_t, ovl = tabs
    H, G, Dh = N_HEADS, N_KV_GROUPS, HEAD_DIM
    HD, GD = H * Dh, G * Dh
    n_qkv = HD + 6 * GD
    hn, = _rmsnorm(xf, [norm], BF16)
    qscale = jnp.concatenate([jnp.full((1, HD), LOG2E / math.sqrt(Dh), F32), jnp.ones((1, 6 * GD), F32)], axis=1)
    w_in_t = w_in.T
    qkv = _matmul(hn, w_in_t, BF16, n=n_qkv, colscale=qscale, w_rows=True)
    assert 3 * H <= LANES
    gl = _matmul(hn, w_in_t, F32, col0=n_qkv, n=LANES, w_rows=True, tm=2048, tn=LANES)
    z = _matmul(hn, w_in_t, F32, col0=n_qkv + 3 * H, n=3 * HD, w_rows=True)

    qkv = qkv.reshape(batch, seq, n_qkv)
    assert HD % (2 * GD) == 0
    kc, vct = _compress(qkv, HD // (2 * GD), cmp_pos.reshape(1, CMP_BLOCK * Dh),
                        kw1.astype(BF16), kw2.astype(BF16), vw1.astype(BF16), vw2.astype(BF16), groups=G)
    mixed = _nsa_attention(qkv, kc, vct, bias_c, bias_t,
                           gl.reshape(batch, seq, LANES), z.reshape(batch, seq, 3 * HD),
                           ovl, batch=batch, seq=seq)
    return _matmul(mixed.reshape(batch * seq, HD), w_out, F32, res=xf)


def _sb_layer(xf, hn, kv, tri, w_in, w_out, *, batch, seq):
    H, Dh = N_HEADS, HEAD_DIM
    HD = H * Dh
    q = _matmul(hn, w_in, BF16, n=HD, colscale=jnp.full((1, HD), LOG2E / math.sqrt(Dh), F32))
    z = _matmul(hn, w_in, F32, col0=HD, n=HD)
    o = _sb_attention(q.reshape(batch, seq, HD), kv, z.reshape(batch, seq, HD), tri, batch=batch, seq=seq)
    return _matmul(o.reshape(batch * seq, HD), w_out, F32, res=xf)


def kernel(x, rel_bias, a0_norm, a0_w_in, a0_cmp_pos, a0_cmp_k_w1, a0_cmp_k_w2, a0_cmp_v_w1, a0_cmp_v_w2, a0_w_out, a1_norm, a1_w_in, a1_cmp_pos, a1_cmp_k_w1, a1_cmp_k_w2, a1_cmp_v_w1, a1_cmp_v_w2, a1_w_out, kv_norm, w_kv, b2_norm, b2_w_in, b2_w_out, b3_norm, b3_w_in, b3_w_out, final_norm):
    batch, seq, d = x.shape
    HD = N_HEADS * HEAD_DIM
    xf = x.reshape(batch * seq, d)

    idx_c, idx_t, ovl = _static_tables(seq)
    rel_bias_ext = jnp.concatenate([rel_bias.T * LOG2E, jnp.full((N_HEADS, 1), NEG_BIG, F32)], axis=1)
    tabs = (_bias_table(rel_bias_ext, idx_c), _bias_table(rel_bias_ext, idx_t), jnp.asarray(ovl, BF16))

    xf = _nsa_layer(xf, tabs, a0_norm, a0_w_in, a0_cmp_pos, a0_cmp_k_w1, a0_cmp_k_w2, a0_cmp_v_w1, a0_cmp_v_w2,
                    a0_w_out, batch=batch, seq=seq)
    xf = _nsa_layer(xf, tabs, a1_norm, a1_w_in, a1_cmp_pos, a1_cmp_k_w1, a1_cmp_k_w2, a1_cmp_v_w1, a1_cmp_v_w2,
                    a1_w_out, batch=batch, seq=seq)

    hn_kv, hn_b2 = _rmsnorm(xf, [kv_norm, b2_norm], BF16)
    kv = _matmul(hn_kv, w_kv, BF16).reshape(batch, seq, 2 * HD)
    m = np.arange(LANES)[None, :] >= np.arange(LANES)[:, None]
    tri = jnp.asarray(np.concatenate([m, m], axis=1), BF16)
    xf = _sb_layer(xf, hn_b2, kv, tri, b2_w_in, b2_w_out, batch=batch, seq=seq)
    hn_b3, = _rmsnorm(xf, [b3_norm], BF16)
    xf = _sb_layer(xf, hn_b3, kv, tri, b3_w_in, b3_w_out, batch=batch, seq=seq)

    out, = _rmsnorm(xf, [final_norm], F32)
    return out.reshape(batch, seq, d)
```

```python
import functools
import math

import numpy as np
import jax
import jax.numpy as jnp
from jax import lax
from jax.experimental import pallas as pl
from jax.experimental.pallas import tpu as pltpu

N_HEADS = 32
HEAD_DIM = 128
N_KV_GROUPS = 4
CMP_BLOCK = 32
CMP_STRIDE = 16
SLC_BLOCK = 64
N_SELECT = 16
WINDOW = 512
N_BUCKETS = 32
MAX_DISTANCE = 128
RMS_EPS = 1e-6
FORCE_SCORE = 1e6

LANES = 128
NEG_BIG = -1e30
VMEM_LIMIT = 62 * 1024 * 1024
LOG2E = math.log2(math.e)

F32 = jnp.float32
BF16 = jnp.bfloat16


def _cparams(sem):
    return pltpu.CompilerParams(dimension_semantics=sem, vmem_limit_bytes=VMEM_LIMIT)


def _dot(a, b):
    return jnp.dot(a, b, preferred_element_type=F32)


def _sigmoid(x):
    return 0.5 * jnp.tanh(0.5 * x) + 0.5


def _rmsnorm_kernel(x_ref, *refs):
    n_out = len(refs) // 2
    x = x_ref[...]
    ms = jnp.mean(x * x, axis=-1, keepdims=True)
    y = x * lax.rsqrt(ms + RMS_EPS)
    for g_ref, o_ref in zip(refs[:n_out], refs[n_out:]):
        o_ref[...] = (y * g_ref[...]).astype(o_ref.dtype)


def _rmsnorm(x, gains, out_dtype, tm=512):
    m, d = x.shape
    tm = min(tm, m)
    row = pl.BlockSpec((tm, d), lambda i: (i, 0))
    outs = pl.pallas_call(
        _rmsnorm_kernel,
        grid=(m // tm,),
        in_specs=[row] + [pl.BlockSpec((1, d), lambda i: (0, 0))] * len(gains),
        out_specs=[row] * len(gains),
        out_shape=[jax.ShapeDtypeStruct((m, d), out_dtype)] * len(gains),
        compiler_params=_cparams(("parallel",)),
        name="rmsnorm",
    )(x, *[g.reshape(1, d) for g in gains])
    return outs


W_SLABS = 4


def _matmul_kernel(*refs, has_res, has_scale, cast_w, shift, w_rows):
    it = iter(refs)
    x_ref = next(it)
    w_refs = [next(it) for _ in range(W_SLABS if cast_w else 1)]
    wn_ref = next(it) if shift else None
    s_ref = next(it) if has_scale else None
    r_ref = next(it) if has_res else None
    o_ref = next(it)
    if cast_w:
        wb_ref = next(it)

        @pl.when(pl.program_id(1) == 0)
        def _():
            ks = wb_ref.shape[0] // W_SLABS
            for s, w_ref in enumerate(w_refs):
                w = w_ref[...]
                if shift:
                    ax = 0 if w_rows else 1
                    nxt = wn_ref[:, s * ks:(s + 1) * ks] if w_rows else wn_ref[s * ks:(s + 1) * ks, :]
                    w = jnp.concatenate([lax.slice_in_dim(w, shift, w.shape[ax], axis=ax),
                                         lax.slice_in_dim(nxt, 0, shift, axis=ax)], axis=ax)
                if w_rows:
                    w = w.T
                if has_scale:
                    w = w * s_ref[...]
                wb_ref[s * ks:(s + 1) * ks, :] = w.astype(BF16)
    else:
        wb_ref = w_refs[0]
    acc = _dot(x_ref[...], wb_ref[...])
    if has_res:
        acc = r_ref[...] + acc
    o_ref[...] = acc.astype(o_ref.dtype)


def _matmul(x, w, out_dtype, *, col0=0, n=None, res=None, colscale=None, w_rows=False, tm=None, tn=None):
    m, k = x.shape
    n_total = w.shape[0] if w_rows else w.shape[1]
    n = n_total - col0 if n is None else n
    if tm is None:
        tm, tn = 512, 1024
    tm, tn = min(tm, m), min(tn, n)
    cast_w = w.dtype != BF16
    shift = col0 % LANES
    base = col0 - shift
    assert m % tm == 0 and n % tn == 0 and base % tn == 0 and tn % LANES == 0
    assert cast_w or (colscale is None and shift == 0 and not w_rows)
    joff = base // tn
    per = tn // LANES
    n_i = m // tm
    in_specs = [pl.BlockSpec((tm, k), lambda j, i: (i, 0))]
    args = [x]
    if cast_w:
        assert k % (W_SLABS * LANES) == 0
        ks = k // W_SLABS
        stagger = n_i > W_SLABS
        for s in range(W_SLABS):
            def tile(j, i, s=s):
                ahead = (i > s).astype(jnp.int32) if stagger else 0
                return jnp.minimum(j + ahead, n // tn - 1) + joff
            if w_rows:
                in_specs.append(pl.BlockSpec((tn, ks), lambda j, i, s=s, tile=tile: (tile(j, i), s)))
            else:
                in_specs.append(pl.BlockSpec((ks, tn), lambda j, i, s=s, tile=tile: (s, tile(j, i))))
            args.append(w)
    else:
        in_specs.append(pl.BlockSpec((k, tn), lambda j, i: (0, j + joff)))
        args.append(w)
    if shift:
        if w_rows:
            assert shift % 8 == 0
            in_specs.append(pl.BlockSpec((LANES, k), lambda j, i: ((j + joff + 1) * per, 0)))
        else:
            in_specs.append(pl.BlockSpec((k, LANES), lambda j, i: (0, (j + joff + 1) * per)))
        args.append(w)
    if colscale is not None:
        in_specs.append(pl.BlockSpec((1, tn), lambda j, i: (0, j)))
        args.append(colscale)
    if res is not None:
        in_specs.append(pl.BlockSpec((tm, tn), lambda j, i: (i, j)))
        args.append(res)
    return pl.pallas_call(
        functools.partial(_matmul_kernel, has_res=res is not None, has_scale=colscale is not None,
                          cast_w=cast_w, shift=shift, w_rows=w_rows),
        grid=(n // tn, n_i),
        in_specs=in_specs,
        out_specs=pl.BlockSpec((tm, tn), lambda j, i: (i, j)),
        out_shape=jax.ShapeDtypeStruct((m, n), out_dtype),
        scratch_shapes=[pltpu.VMEM((k, tn), BF16)] if cast_w else [],
        compiler_params=_cparams(("arbitrary", "arbitrary")),
        name="matmul_res" if res is not None else "matmul",
    )(*args)


MASKED_BUCKET = N_BUCKETS


def _t5_bucket_np(dist):
    max_exact = N_BUCKETS // 2
    d = np.maximum(dist, 0)
    log_ratio = np.log(np.maximum(d, max_exact).astype(np.float32) / np.float32(max_exact))
    large = max_exact + (log_ratio / np.float32(math.log(MAX_DISTANCE / max_exact))
                         * np.float32(N_BUCKETS - max_exact)).astype(np.int32)
    return np.where(d < max_exact, d, np.minimum(large, N_BUCKETS - 1)).astype(np.int32)


def _bias_kernel(rb_ref, idx_ref, o_ref):
    h = pl.program_id(0)
    idx = idx_ref[...]
    acc = jnp.zeros(idx.shape, F32)
    for b in range(N_BUCKETS + 1):
        acc = jnp.where(idx == b, rb_ref[h, b], acc)
    o_ref[...] = acc


def _bias_table(rel_bias_ext, bucket_idx):
    h = rel_bias_ext.shape[0]
    rows = bucket_idx.shape[0]
    return pl.pallas_call(
        _bias_kernel,
        grid=(h,),
        in_specs=[pl.BlockSpec(memory_space=pltpu.SMEM),
                  pl.BlockSpec((rows, LANES), lambda i: (0, 0))],
        out_specs=pl.BlockSpec((rows, LANES), lambda i: (0, i)),
        out_shape=jax.ShapeDtypeStruct((rows, h * LANES), F32),
        compiler_params=_cparams(("arbitrary",)),
        name="bias_table",
    )(rel_bias_ext, jnp.asarray(bucket_idx))


def _compress_kernel(kv_ref, pos_ref, kw1_ref, kw2_ref, vw1_ref, vw2_ref, kc_ref, vct_ref, raw_scr, *, groups):
    half = pos_ref.shape[1] // 2
    nch = kv_ref.shape[1] // CMP_STRIDE
    gd = groups * HEAD_DIM
    g = pl.program_id(1)
    row = lax.broadcasted_iota(jnp.int32, (nch, HEAD_DIM), 0)

    @pl.when(g == 0)
    def _():
        for cb in range(2 * groups):
            raw_scr[cb] = kv_ref[0, :, cb * HEAD_DIM:(cb + 1) * HEAD_DIM].astype(F32)

    def one(col, w1_ref, w2_ref):
        cb = col // HEAD_DIM + g
        x = jnp.concatenate([raw_scr[cb, pl.ds(o, nch, stride=CMP_STRIDE), :] for o in range(CMP_STRIDE)], axis=1)
        lo = (x + pos_ref[:, :half]).astype(BF16)
        hi = pltpu.roll(x + pos_ref[:, half:], nch - 1, 0).astype(BF16)
        h = _dot(lo, w1_ref[:half, :]) + _dot(hi, w1_ref[half:, :])
        h = h * _sigmoid(h)
        o = _dot(h.astype(BF16), w2_ref[...])
        return jnp.where(row < nch - 1, o, 0.0)

    kc_ref[0, 0] = one(0, kw1_ref, kw2_ref).astype(kc_ref.dtype)
    vct_ref[0, 0] = one(gd, vw1_ref, vw2_ref).T.astype(vct_ref.dtype)


def _compress(qkv, col_block, pos, kw1, kw2, vw1, vw2, *, groups):
    b, seq, _ = qkv.shape
    nch = seq // CMP_STRIDE
    wid = 2 * groups * HEAD_DIM
    oblk = pl.BlockSpec((1, 1, nch, HEAD_DIM), lambda i, j: (i, j, 0, 0))

    def const(a):
        return pl.BlockSpec(a.shape, lambda i, j: (0,) * a.ndim)

    out = jax.ShapeDtypeStruct((b, groups, nch, HEAD_DIM), BF16)
    return pl.pallas_call(
        functools.partial(_compress_kernel, groups=groups),
        grid=(b, groups),
        in_specs=[pl.BlockSpec((1, seq, wid), lambda i, j: (i, 0, col_block)),
                  const(pos), const(kw1), const(kw2), const(vw1), const(vw2)],
        out_specs=[oblk, oblk],
        out_shape=[out, out],
        scratch_shapes=[pltpu.VMEM((2 * groups, seq, HEAD_DIM), F32)],
        compiler_params=_cparams(("parallel", "arbitrary")),
        name="compress",
    )(qkv, pos, kw1, kw2, vw1, vw2)


def _split3(x):
    a = x.astype(BF16)
    r = x - a.astype(F32)
    b = r.astype(BF16)
    c = (r - b.astype(F32)).astype(BF16)
    return a, b, c


def _nsa_kernel(q_ref, kc_ref, vct_ref, ks_ref, vs_ref, kw_ref, vw_ref, bc_ref, bt_ref, gl_ref,
                zc_ref, zs_ref, zw_ref, ovl_ref, o_ref, sel_scr, acc_scr, s_scr, p_scr, acc2_scr, s2_scr, p2_scr,
                vst_scr, vwt_scr, gate_scr,
                *, group, n_slc, n_sel):
    R = group
    QB = LANES
    W = R * QB
    i = pl.program_id(2)
    q0 = i * QB

    @pl.when(i == 0)
    def _():
        for c in range(vs_ref.shape[1] // QB):
            cols = slice(c * QB, (c + 1) * QB)
            vst_scr[:, cols] = vs_ref[0, cols, :].T
            vwt_scr[:, cols] = vw_ref[0, cols, :].T

    q = q_ref[0]
    qt = jnp.concatenate([q[:, r * HEAD_DIM:(r + 1) * HEAD_DIM].T for r in range(R)], axis=1)

    per_blk = QB // CMP_STRIDE
    bc_row0 = pl.multiple_of((pl.num_programs(2) - 1 - i) * per_blk, per_blk)
    s = _dot(kc_ref[0, 0], qt) + bc_ref[pl.ds(bc_row0, QB), :]
    m = jnp.max(s, axis=0, keepdims=True)
    e = jnp.exp2(s - m)
    denom = jnp.maximum(jnp.sum(e, axis=0, keepdims=True), 1e-30)
    p_c = e * jnp.where(m > 0.5 * NEG_BIG, 1.0 / denom, 0.0)
    o_c = _dot(vct_ref[0, 0], p_c.astype(BF16))

    psum = p_c[:, 0:QB]
    for r in range(1, R):
        psum = psum + p_c[:, r * QB:(r + 1) * QB]
    ovl = ovl_ref[...]
    imp = sum(_dot(ovl, part) for part in _split3(psum))[:n_slc]
    j = lax.broadcasted_iota(jnp.int32, (n_slc, QB), 0)
    t = q0 + lax.broadcasted_iota(jnp.int32, (n_slc, QB), 1)
    cur = t // SLC_BLOCK
    forced = (j == 0) | (j == cur) | (j == cur - 1)
    score = jnp.where(forced, FORCE_SCORE, jnp.where(j * SLC_BLOCK <= t, imp, -1.0))
    rank = jnp.zeros((n_slc, QB), F32)
    for jp in range(n_slc):
        other = score[jp:jp + 1, :]
        beats = (other > score) | ((other == score) & (j > jp))
        rank = rank + jnp.where(beats, 1.0, 0.0)
    sel_scr[...] = jnp.where(rank < n_sel, 0.0, NEG_BIG)

    far = WINDOW // QB
    lo_w = jnp.maximum(i - far, 0)

    def tile_slice(kt):
        return pl.ds(pl.multiple_of(kt * QB, QB), QB)

    def bias(tile):
        return bt_ref[tile_slice(tile), :]

    class Chain:
        def __init__(self, k_ref, vt_ref, s_ref, p_ref, a_ref, block_mask):
            self.k_ref, self.vt_ref, self.s_ref, self.p_ref, self.a_ref = k_ref, vt_ref, s_ref, p_ref, a_ref
            self.block_mask = block_mask

        def raw_scores(self, kt):
            return _dot(self.k_ref[0, tile_slice(kt), :], qt)

        def start(self, first):
            s = self.raw_scores(i) + bias(0)
            m = jnp.max(s, axis=0, keepdims=True)
            p = jnp.exp2(s - m)
            l = jnp.sum(p, axis=0, keepdims=True)
            self.a_ref[...] = jnp.zeros_like(self.a_ref)
            self.p_ref[...] = p.astype(BF16)
            self.s_ref[...] = self.raw_scores(first)
            return m, l, jnp.ones((1, W), F32)

        def accumulate(self, kt, alpha):
            self.a_ref[...] = alpha * self.a_ref[...] + _dot(self.vt_ref[:, tile_slice(kt)], self.p_ref[...])

        def step(self, carry, prev, cur, nxt, tile):
            m, l, alpha_prev = carry
            self.accumulate(prev, alpha_prev)
            s = self.s_ref[...] + bias(tile)
            if self.block_mask:
                per = QB // SLC_BLOCK
                add = jnp.concatenate(
                    [jnp.broadcast_to(sel_scr[pl.ds(cur * per + c, 1), :], (SLC_BLOCK, QB)) for c in range(per)],
                    axis=0)
                s = s + jnp.concatenate([add] * R, axis=1)
            m_new = jnp.maximum(m, jnp.max(s, axis=0, keepdims=True))
            alpha = jnp.exp2(m - m_new)
            p = jnp.exp2(s - m_new)
            l = alpha * l + jnp.sum(p, axis=0, keepdims=True)
            self.p_ref[...] = p.astype(BF16)
            self.s_ref[...] = self.raw_scores(nxt)
            return m_new, l, alpha

        def finish(self, carry, last):
            _, l, alpha = carry
            self.accumulate(last, alpha)
            return self.a_ref[...] * (1.0 / l)

    sel = Chain(ks_ref, vst_scr, s_scr, p_scr, acc_scr, True)
    win = Chain(kw_ref, vwt_scr, s2_scr, p2_scr, acc2_scr, False)
    carry_s = sel.start(lo_w)
    carry_w = win.start(lo_w)

    def both(kt, carry):
        carry_s, carry_w = carry
        prev = jnp.where(kt == lo_w, i, kt - 1)
        d = i - kt
        carry_s = sel.step(carry_s, prev, kt, jnp.where(kt == i - 1, 0, kt + 1), jnp.minimum(d, 2))
        carry_w = win.step(carry_w, prev, kt, jnp.minimum(kt + 1, i), jnp.where(d == far, 3, jnp.minimum(d, 2)))
        return carry_s, carry_w

    carry_s, carry_w = lax.fori_loop(lo_w, i, both, (carry_s, carry_w))
    o_w = win.finish(carry_w, jnp.where(i > 0, i - 1, i))

    def sel_only(kt, carry):
        return sel.step(carry, jnp.where(kt == 0, i - 1, kt - 1), kt, jnp.minimum(kt + 1, i), 2)

    carry_s = lax.fori_loop(0, lo_w, sel_only, carry_s)
    o_s = sel.finish(carry_s, jnp.where(lo_w > 0, lo_w - 1, jnp.where(i > 0, i - 1, i)))

    n_heads = R * pl.num_programs(0)
    head0 = pl.program_id(0) * R
    gate_scr[...] = _sigmoid(gl_ref[0]).T
    for r in range(R):
        cols = slice(r * QB, (r + 1) * QB)
        mixed = jnp.zeros((QB, HEAD_DIM), F32)
        for c, (o, z_ref) in enumerate(((o_c, zc_ref), (o_s, zs_ref), (o_w, zw_ref))):
            z = z_ref[0, :, cols]
            gated = o[:, cols] * gate_scr[pl.ds(c * n_heads + head0 + r, 1), :]
            mixed = mixed + gated.T * (z * _sigmoid(z))
        o_ref[0, :, cols] = mixed.astype(o_ref.dtype)


def _nsa_attention(qkv, kc, vct, bias_c, bias_t, gl, z, ovl, *, batch, seq):
    G, R = N_KV_GROUPS, N_HEADS // N_KV_GROUPS
    QB = LANES
    W = R * QB
    nqb = seq // QB
    n_slc = seq // SLC_BLOCK
    n_sel = min(N_SELECT, n_slc)

    def kv_spec(which):
        return pl.BlockSpec((1, seq, HEAD_DIM), lambda g, b, i: (b, 0, N_HEADS + which * G + g))

    def z_spec(c):
        return pl.BlockSpec((1, QB, R * HEAD_DIM), lambda g, b, i: (b, i, c * G + g))

    cmp_spec = pl.BlockSpec((1, 1, LANES, HEAD_DIM), lambda g, b, i: (b, g, 0, 0))
    in_specs = [
        pl.BlockSpec((1, QB, R * HEAD_DIM), lambda g, b, i: (b, i, g)),
        cmp_spec, cmp_spec,
        kv_spec(2), kv_spec(3), kv_spec(4), kv_spec(5),
        pl.BlockSpec((bias_c.shape[0], W), lambda g, b, i: (0, g)),
        pl.BlockSpec((4 * QB, W), lambda g, b, i: (0, g)),
        pl.BlockSpec((1, QB, LANES), lambda g, b, i: (b, i, 0)),
        z_spec(0), z_spec(1), z_spec(2),
        pl.BlockSpec(ovl.shape, lambda g, b, i: (0, 0)),
    ]
    return pl.pallas_call(
        functools.partial(_nsa_kernel, group=R, n_slc=n_slc, n_sel=n_sel),
        grid=(G, batch, nqb),
        in_specs=in_specs,
        out_specs=pl.BlockSpec((1, QB, R * HEAD_DIM), lambda g, b, i: (b, i, g)),
        out_shape=jax.ShapeDtypeStruct((batch, seq, N_HEADS * HEAD_DIM), BF16),
        scratch_shapes=[pltpu.VMEM((n_slc, QB), F32), pltpu.VMEM((HEAD_DIM, W), F32),
                        pltpu.VMEM((QB, W), F32), pltpu.VMEM((QB, W), BF16),
                        pltpu.VMEM((HEAD_DIM, W), F32), pltpu.VMEM((QB, W), F32), pltpu.VMEM((QB, W), BF16),
                        pltpu.VMEM((HEAD_DIM, seq), BF16), pltpu.VMEM((HEAD_DIM, seq), BF16),
                        pltpu.VMEM((LANES, QB), F32)],
        compiler_params=_cparams(("parallel", "parallel", "arbitrary")),
        name="nsa_attention",
    )(qkv, kc, vct, qkv, qkv, qkv, qkv, bias_c, bias_t, gl, z, z, z, ovl)


def _sb_kernel(q_ref, k_ref, v_ref, z_ref, tri_ref, o_ref, vt_scr, l_scr, d_scr, w_scr, acc_scr, *, tq, hp):
    i = pl.program_id(2)
    tk = tq
    tri = tri_ref[...]
    n_sub = tk // LANES
    heads = range(hp)

    @pl.when(i == 0)
    def _():
        for g in heads:
            for c in range(v_ref.shape[1] // LANES):
                cols = slice(c * LANES, (c + 1) * LANES)
                vt_scr[g, :, cols] = v_ref[0, cols, g * HEAD_DIM:(g + 1) * HEAD_DIM].T

    qts = [q_ref[0, :, g * HEAD_DIM:(g + 1) * HEAD_DIM].T for g in heads]

    def key0(t):
        return pl.multiple_of(jnp.maximum(i - t, 0) * tk, tk)

    def logits(t):
        k0 = key0(t)
        for g in heads:
            l_scr[g] = _dot(k_ref[0, pl.ds(k0, tk), g * HEAD_DIM:(g + 1) * HEAD_DIM], qts[g])

    def softplus_sums(diagonal):
        totals = []
        for g in heads:
            logit = l_scr[g]
            neg_abs = pltpu.bitcast(pltpu.bitcast(logit, jnp.uint32) | jnp.uint32(0x80000000), F32)
            sp = jnp.maximum(logit, 0.0) + jnp.log(1.0 + jnp.exp2(neg_abs)) * LOG2E
            if diagonal:
                ok = (lax.broadcasted_iota(jnp.int32, (tk, tq), 0) < lax.broadcasted_iota(jnp.int32, (tk, tq), 1))
                sp = jnp.where(ok, sp, 0.0)
                logit = jnp.where(ok, logit, NEG_BIG)
            d_scr[g] = logit
            hi = sp.astype(BF16)
            lo = (sp - hi.astype(F32)).astype(BF16)
            tot = []
            for c in range(n_sub):
                rows = slice(c * LANES, (c + 1) * LANES)
                within = _dot(tri, jnp.concatenate([hi[rows], lo[rows]], axis=0))
                w_scr[g, rows, :] = within
                tot.append(within[0:1])
            totals.append(tuple(tot))
        return tuple(totals)

    def weigh_values(t, laters, totals):
        k0 = key0(t)
        out = []
        for g in heads:
            later = laters[g]
            parts = [None] * n_sub
            for c in reversed(range(n_sub)):
                rows = slice(c * LANES, (c + 1) * LANES)
                parts[c] = jnp.exp2(d_scr[g, rows, :] - w_scr[g, rows, :] - later)
                later = later + totals[g][c]
            a = jnp.concatenate(parts, axis=0).astype(BF16)
            acc_scr[g] += _dot(vt_scr[g, :, pl.ds(k0, tk)], a)
            out.append(later)
        return tuple(out)

    acc_scr[...] = jnp.zeros_like(acc_scr)
    logits(0)
    totals = softplus_sums(True)
    logits(1)

    def body(t, carry):
        laters, totals = carry
        laters = weigh_values(t - 1, laters, totals)
        totals = softplus_sums(False)
        logits(t + 1)
        return laters, totals

    laters, totals = lax.fori_loop(1, i + 1, body, ((jnp.zeros((1, tq), F32),) * hp, totals))
    weigh_values(i, laters, totals)
    for g in heads:
        cols = slice(g * HEAD_DIM, (g + 1) * HEAD_DIM)
        z = z_ref[0, :, cols]
        o_ref[0, :, cols] = (acc_scr[g].T * (z * _sigmoid(z))).astype(o_ref.dtype)


def _sb_attention(q, kv, z, tri, *, batch, seq, tq=512, hp=4):
    H = N_HEADS
    tq = min(tq, seq)
    wid = hp * HEAD_DIM
    return pl.pallas_call(
        functools.partial(_sb_kernel, tq=tq, hp=hp),
        grid=(batch, H // hp, seq // tq),
        in_specs=[
            pl.BlockSpec((1, tq, wid), lambda b, h, i: (b, i, h)),
            pl.BlockSpec((1, seq, wid), lambda b, h, i: (b, 0, h)),
            pl.BlockSpec((1, seq, wid), lambda b, h, i: (b, 0, H // hp + h)),
            pl.BlockSpec((1, tq, wid), lambda b, h, i: (b, i, h)),
            pl.BlockSpec(tri.shape, lambda b, h, i: (0, 0)),
        ],
        out_specs=pl.BlockSpec((1, tq, wid), lambda b, h, i: (b, i, h)),
        out_shape=jax.ShapeDtypeStruct((batch, seq, H * HEAD_DIM), BF16),
        scratch_shapes=[pltpu.VMEM((hp, HEAD_DIM, seq), BF16), pltpu.VMEM((hp, tq, tq), F32),
                        pltpu.VMEM((hp, tq, tq), F32), pltpu.VMEM((hp, tq, tq), F32),
                        pltpu.VMEM((hp, HEAD_DIM, tq), F32)],
        compiler_params=_cparams(("parallel", "parallel", "arbitrary")),
        name="sb_attention",
    )(q, kv, kv, z, tri)


def _static_tables(seq):
    QB = LANES
    nch = seq // CMP_STRIDE
    n_slc = seq // SLC_BLOCK
    assert nch == LANES and n_slc <= LANES
    q = np.arange(QB)[None, :]
    per_blk = QB // CMP_STRIDE
    off = (seq // QB - 1) * per_blk
    rel = np.arange(off + QB)[:, None] - off
    dist_c = q - (rel * CMP_STRIDE + CMP_BLOCK - 1)
    idx_c = np.where(dist_c >= 0, _t5_bucket_np(dist_c), MASKED_BUCKET).astype(np.int32)
    k = np.arange(QB)[:, None]
    far = WINDOW // QB
    tiles = []
    for delta, keep in ((0, k <= q), (1, None), (2, None), (far, k > q)):
        idx = _t5_bucket_np(delta * QB + q - k)
        tiles.append(idx if keep is None else np.where(keep, idx, MASKED_BUCKET))
    idx_t = np.concatenate(tiles, axis=0).astype(np.int32)
    cmp_start = np.arange(LANES) * CMP_STRIDE
    slc_start = np.arange(LANES) * SLC_BLOCK
    ovl = ((cmp_start[None, :] < slc_start[:, None] + SLC_BLOCK)
           & (cmp_start[None, :] + CMP_BLOCK - 1 >= slc_start[:, None]))
    ovl = ovl & (np.arange(LANES)[None, :] < nch - 1) & (np.arange(LANES)[:, None] < n_slc)
    return idx_c, idx_t, ovl.astype(np.float32)


def _nsa_layer(xf, tabs, norm, w_in, cmp_pos, kw1, kw2, vw1, vw2, w_out, *, batch, seq):
    bias_c, bias_t, ovl = tabs
    H, G, Dh = N_HEADS, N_KV_GROUPS, HEAD_DIM
    HD, GD = H * Dh, G * Dh
    n_qkv = HD + 6 * GD
    hn, = _rmsnorm(xf, [norm], BF16)
    qscale = jnp.concatenate([jnp.full((1, HD), LOG2E / math.sqrt(Dh), F32), jnp.ones((1, 6 * GD), F32)], axis=1)
    w_in_t = w_in.T
    qkv = _matmul(hn, w_in_t, BF16, n=n_qkv, colscale=qscale, w_rows=True)
    assert 3 * H <= LANES
    gl = _matmul(hn, w_in_t, F32, col0=n_qkv, n=LANES, w_rows=True, tm=2048, tn=LANES)
    z = _matmul(hn, w_in_t, F32, col0=n_qkv + 3 * H, n=3 * HD, w_rows=True)

    qkv = qkv.reshape(batch, seq, n_qkv)
    assert HD % (2 * GD) == 0
    kc, vct = _compress(qkv, HD // (2 * GD), cmp_pos.reshape(1, CMP_BLOCK * Dh),
                        kw1.astype(BF16), kw2.astype(BF16), vw1.astype(BF16), vw2.astype(BF16), groups=G)
    mixed = _nsa_attention(qkv, kc, vct, bias_c, bias_t,
                           gl.reshape(batch, seq, LANES), z.reshape(batch, seq, 3 * HD),
                           ovl, batch=batch, seq=seq)
    return _matmul(mixed.reshape(batch * seq, HD), w_out, F32, res=xf)


def _sb_layer(xf, hn, kv, tri, w_in, w_out, *, batch, seq):
    H, Dh = N_HEADS, HEAD_DIM
    HD = H * Dh
    q = _matmul(hn, w_in, BF16, n=HD, colscale=jnp.full((1, HD), LOG2E / math.sqrt(Dh), F32))
    z = _matmul(hn, w_in, F32, col0=HD, n=HD)
    o = _sb_attention(q.reshape(batch, seq, HD), kv, z.reshape(batch, seq, HD), tri, batch=batch, seq=seq)
    return _matmul(o.reshape(batch * seq, HD), w_out, F32, res=xf)


def kernel(x, rel_bias, a0_norm, a0_w_in, a0_cmp_pos, a0_cmp_k_w1, a0_cmp_k_w2, a0_cmp_v_w1, a0_cmp_v_w2, a0_w_out, a1_norm, a1_w_in, a1_cmp_pos, a1_cmp_k_w1, a1_cmp_k_w2, a1_cmp_v_w1, a1_cmp_v_w2, a1_w_out, kv_norm, w_kv, b2_norm, b2_w_in, b2_w_out, b3_norm, b3_w_in, b3_w_out, final_norm):
    batch, seq, d = x.shape
    HD = N_HEADS * HEAD_DIM
    xf = x.reshape(batch * seq, d)

    idx_c, idx_t, ovl = _static_tables(seq)
    rel_bias_ext = jnp.concatenate([rel_bias.T * LOG2E, jnp.full((N_HEADS, 1), NEG_BIG, F32)], axis=1)
    tabs = (_bias_table(rel_bias_ext, idx_c), _bias_table(rel_bias_ext, idx_t), jnp.asarray(ovl, BF16))

    xf = _nsa_layer(xf, tabs, a0_norm, a0_w_in, a0_cmp_pos, a0_cmp_k_w1, a0_cmp_k_w2, a0_cmp_v_w1, a0_cmp_v_w2,
                    a0_w_out, batch=batch, seq=seq)
    xf = _nsa_layer(xf, tabs, a1_norm, a1_w_in, a1_cmp_pos, a1_cmp_k_w1, a1_cmp_k_w2, a1_cmp_v_w1, a1_cmp_v_w2,
                    a1_w_out, batch=batch, seq=seq)

    hn_kv, hn_b2 = _rmsnorm(xf, [kv_norm, b2_norm], BF16)
    kv = _matmul(hn_kv, w_kv, BF16).reshape(batch, seq, 2 * HD)
    m = np.arange(LANES)[None, :] >= np.arange(LANES)[:, None]
    tri = jnp.asarray(np.concatenate([m, m], axis=1), BF16)
    xf = _sb_layer(xf, hn_b2, kv, tri, b2_w_in, b2_w_out, batch=batch, seq=seq)
    hn_b3, = _rmsnorm(xf, [b3_norm], BF16)
    xf = _sb_layer(xf, hn_b3, kv, tri, b3_w_in, b3_w_out, batch=batch, seq=seq)

    out, = _rmsnorm(xf, [final_norm], F32)
    return out.reshape(batch, seq, d)
```
